```python
import math
import jax, jax.numpy as jnp
from jax import lax
import numpy as np

D_MODEL = 1024
BATCH = 8
SEQ = 8192
DEPTH = 2

GRID_W = 64
CTX_LEN = 256

SSD_EXPAND = 2
D_INNER = SSD_EXPAND * D_MODEL
SSD_HEADDIM = 64
SSD_HEADS = D_INNER // SSD_HEADDIM
SSD_GROUPS = 4
SSD_STATE = 128
SSD_CONV = 5
SSD_CHUNK = 128
D_BC = SSD_GROUPS * SSD_STATE
D_XBC = D_INNER + 2 * D_BC

POOL_WINDOWS = (2, 4, 8, 16)
POOL_WIDTH = D_MODEL
POOL_GROUPS = len(POOL_WINDOWS)
POOL_GROUP_DIM = POOL_WIDTH // POOL_GROUPS

IN_SPLITS = (D_INNER, D_INNER + D_XBC, D_INNER + D_XBC + 2 * SSD_HEADS,
             D_INNER + D_XBC + 2 * SSD_HEADS + POOL_WIDTH)
IN_COLS = IN_SPLITS[-1] + 2 * D_MODEL

MOE_GROUPS = 4
MOE_EXPERTS_PER_GROUP = 8
MOE_EXPERTS = MOE_GROUPS * MOE_EXPERTS_PER_GROUP
MOE_TOP_K = 2
D_EXPERT = 512
MOE_BLOCK = 256

DEEPNORM_ALPHA = (2.0 * DEPTH) ** 0.25
DEEPNORM_BETA = (8.0 * DEPTH) ** -0.25
NORM_EPS = 1e-5

kernel_name = 'hybrid_ssd_pool_hmoe_prefix_dit'


def layer_norm(x, g, b):
    xf = x.astype(jnp.float32)
    mu = xf.mean(-1, keepdims=True)
    var = jnp.square(xf - mu).mean(-1, keepdims=True)
    return ((xf - mu) * lax.rsqrt(var + NORM_EPS) * g.astype(jnp.float32) + b.astype(jnp.float32)).astype(x.dtype)


def post_norm(x, y, g, b):
    return layer_norm(DEEPNORM_ALPHA * x + y, g, b)


def modulate(x, shift, scale):
    return x * (1 + scale) + shift


def rms_norm_gated(y, z, g):
    yz = y.astype(jnp.float32) * jax.nn.silu(z.astype(jnp.float32))
    return yz * lax.rsqrt(jnp.mean(yz * yz, -1, keepdims=True) + NORM_EPS) * g.astype(jnp.float32)


def flip_seq(t):
    return t[:, ::-1]


def conv_centred(u, w, b):
    ch = u.shape[-1]
    y = lax.conv_general_dilated(u, w[:, None, :].astype(u.dtype), window_strides=(1,),
                                 padding=((SSD_CONV // 2, SSD_CONV // 2),),
                                 dimension_numbers=('NWC', 'WIO', 'NWC'), feature_group_count=ch)
    return y + b


def ssd_decay(dt_raw, lp):
    b, l, _ = dt_raw.shape
    dt = jax.nn.softplus(dt_raw.reshape(b, l, 2, SSD_HEADS).astype(jnp.float32) + lp['dt_bias'].astype(jnp.float32))
    a = dt * -jnp.exp(lp['a_log'].astype(jnp.float32))
    return dt, a


def ssd_scan(xdt, a, bm, cm, h0):
    b, l = a.shape[:2]
    r = SSD_HEADS // SSD_GROUPS
    q = SSD_CHUNK
    nc = l // q

    def chunks(t):
        return jnp.moveaxis(t.reshape((b, nc, q) + t.shape[2:]), 1, 0)

    xs = (chunks(xdt.reshape(b, l, SSD_GROUPS, r, SSD_HEADDIM)), chunks(a.reshape(b, l, SSD_GROUPS, r)),
          chunks(bm), chunks(cm))
    lower = jnp.tril(jnp.ones((q, q), dtype=bool))[None, :, :, None, None]

    def step(state, inp):
        xc, ac, bc, cc = inp
        acs = jnp.cumsum(ac, axis=1)
        seg = acs[:, :, None] - acs[:, None, :]
        decay = jnp.exp(jnp.where(lower, seg, -jnp.inf))
        cb = jnp.einsum('bign,bjgn->bijg', cc, bc)
        y_diag = jnp.einsum('bijg,bijgr,bjgrp->bigrp', cb, decay, xc)
        y_off = jnp.einsum('bign,bgrpn->bigrp', cc, state) * jnp.exp(acs)[..., None]
        tail = jnp.exp(acs[:, -1:] - acs)
        state = state * jnp.exp(acs[:, -1])[..., None, None] + jnp.einsum('bjgn,bjgr,bjgrp->bgrpn', bc, tail, xc)
        return state, y_diag + y_off

    h_final, ys = lax.scan(step, h0.reshape(b, SSD_GROUPS, r, SSD_HEADDIM, SSD_STATE), xs)
    y = jnp.moveaxis(ys, 0, 1).reshape(b, l, SSD_HEADS, SSD_HEADDIM)
    return y, h_final.reshape(b, SSD_HEADS, SSD_HEADDIM, SSD_STATE)


def ssd_final_state(xdt, a, bm):
    b, l = a.shape[:2]
    r = SSD_HEADS // SSD_GROUPS
    acs = jnp.cumsum(a, axis=1)
    tail = jnp.exp(acs[:, -1:] - acs).reshape(b, l, SSD_GROUPS, r)
    st = jnp.einsum('blgn,blgr,blgrp->bgrpn', bm, tail, xdt.reshape(b, l, SSD_GROUPS, r, SSD_HEADDIM))
    return st.reshape(b, SSD_HEADS, SSD_HEADDIM, SSD_STATE)


def ssd_branch(z, xbc, dt_raw, lp, h0_f, h0_b):
    b, l, _ = z.shape
    xbc = jax.nn.silu(conv_centred(xbc, lp['conv_w'], lp['conv_b'])).astype(jnp.float32)
    xs, bm, cm = jnp.split(xbc, (D_INNER, D_INNER + D_BC), axis=-1)
    xs = xs.reshape(b, l, SSD_HEADS, SSD_HEADDIM)
    bm = bm.reshape(b, l, SSD_GROUPS, SSD_STATE)
    cm = cm.reshape(b, l, SSD_GROUPS, SSD_STATE)
    dt, a = ssd_decay(dt_raw, lp)
    y_f, s_f = ssd_scan(xs * dt[:, :, 0, :, None], a[:, :, 0], bm, cm, h0_f)
    y_b, s_b = ssd_scan(flip_seq(xs * dt[:, :, 1, :, None]), flip_seq(a[:, :, 1]), flip_seq(bm), flip_seq(cm), h0_b)
    y = y_f + flip_seq(y_b) + xs * lp['d_skip'].astype(jnp.float32)[:, None]
    y = rms_norm_gated(y.reshape(b, l, D_INNER), z, lp['ssd_norm_g'])
    return y.astype(z.dtype), s_f, s_b


def ssd_context_states(hc, lp):
    b, l, _ = hc.shape
    w = lp['w_in']
    xb = hc @ w[:, IN_SPLITS[0]:IN_SPLITS[0] + D_INNER + D_BC]
    dt_raw = hc @ w[:, IN_SPLITS[1]:IN_SPLITS[2]]
    xb = jax.nn.silu(conv_centred(xb, lp['conv_w'][:, :D_INNER + D_BC], lp['conv_b'][:D_INNER + D_BC])).astype(jnp.float32)
    xs, bm = jnp.split(xb, (D_INNER,), axis=-1)
    xs = xs.reshape(b, l, SSD_HEADS, SSD_HEADDIM)
    bm = bm.reshape(b, l, SSD_GROUPS, SSD_STATE)
    dt, a = ssd_decay(dt_raw, lp)
    s_f = ssd_final_state(xs * dt[:, :, 0, :, None], a[:, :, 0], bm)
    s_b = ssd_final_state(flip_seq(xs * dt[:, :, 1, :, None]), flip_seq(a[:, :, 1]), flip_seq(bm))
    return s_f, s_b


def window_sum(u, axis, k):
    n = u.shape[axis]
    pad = [(0, 0)] * u.ndim
    pad[axis] = (1, 0)
    cs = jnp.cumsum(jnp.pad(u, pad), axis=axis)
    pos = jnp.arange(n)
    lo = jnp.clip(pos - k // 2, 0, n)
    hi = jnp.clip(pos - k // 2 + k, 0, n)
    s = jnp.take(cs, hi, axis=axis) - jnp.take(cs, lo, axis=axis)
    return s, (hi - lo).astype(jnp.float32)


def pool_grid(uf):
    b, l = uf.shape[:2]
    rows = l // GRID_W
    ug_all = uf.reshape(b, rows, GRID_W, POOL_GROUPS, POOL_GROUP_DIM)
    outs = []
    for gi, k in enumerate(POOL_WINDOWS):
        ug = ug_all[:, :, :, gi]
        s, cnt_r = window_sum(ug, 1, k)
        s, cnt_c = window_sum(s, 2, k)
        outs.append(s / (cnt_r[:, None] * cnt_c[None, :])[None, :, :, None])
    return jnp.stack(outs, axis=3).reshape(b, l, POOL_GROUPS, POOL_GROUP_DIM)


def pool_seq(uf):
    outs = []
    for gi, k in enumerate(POOL_WINDOWS):
        s, cnt = window_sum(uf[:, :, gi], 1, k)
        outs.append(s / cnt[None, :, None])
    return jnp.stack(outs, axis=2)


def pool_branch(u, lp, on_grid):
    b, l, _ = u.shape
    uf = u.astype(jnp.float32).reshape(b, l, POOL_GROUPS, POOL_GROUP_DIM)
    mean = pool_grid(uf) if on_grid else pool_seq(uf)
    y = jnp.einsum('blgc,gcd->blgd', mean - uf, lp['pool_w'].astype(jnp.float32)).reshape(b, l, POOL_WIDTH)
    return (y * lp['pool_scale'].astype(jnp.float32)).astype(u.dtype)


def mix_stream(h, lp, on_grid, h0_f, h0_b):
    z, xbc, dt_raw, u, gates = jnp.split(h @ lp['w_in'], IN_SPLITS, axis=-1)
    y_a, s_f, s_b = ssd_branch(z, xbc, dt_raw, lp, h0_f, h0_b)
    y_p = pool_branch(u, lp, on_grid)
    g = jax.nn.sigmoid((gates + lp['b_gate']).astype(jnp.float32)).astype(h.dtype)
    g_a, g_p = jnp.split(g, 2, axis=-1)
    merged = g_a * (y_a @ lp['w_branch_a']) + g_p * (y_p @ lp['w_branch_b'])
    return merged @ lp['w_out'], s_f, s_b


def swiglu(xb, wg, wu, wd):
    return (jax.nn.silu(xb @ wg) * (xb @ wu)) @ wd


def hier_moe(h, lp):
    t, d = h.shape
    hf = h.astype(jnp.float32)
    lg = hf @ lp['w_rg'].astype(jnp.float32) + lp['b_rg'].astype(jnp.float32)
    grp = jnp.argmax(lg, axis=-1)
    p_grp = jnp.take_along_axis(jax.nn.softmax(lg, axis=-1), grp[:, None], axis=-1)
    le = jnp.einsum('td,gde->tge', hf, lp['w_re'].astype(jnp.float32)) + lp['b_re'].astype(jnp.float32)
    le = jnp.take_along_axis(le, grp[:, None, None], axis=1)[:, 0]
    top_v, top_i = lax.top_k(le, MOE_TOP_K)
    w = p_grp * jax.nn.softmax(top_v, axis=-1)
    eid = grp[:, None].astype(jnp.int32) * MOE_EXPERTS_PER_GROUP + top_i.astype(jnp.int32)
    n_assign = t * MOE_TOP_K
    flat_e = eid.reshape(-1)
    flat_t = jnp.repeat(jnp.arange(t, dtype=jnp.int32), MOE_TOP_K)
    order = jnp.argsort(flat_e)
    se, st, sw = flat_e[order], flat_t[order], w.reshape(-1)[order]
    counts = jnp.bincount(flat_e, length=MOE_EXPERTS)
    padded = (counts + MOE_BLOCK - 1) // MOE_BLOCK * MOE_BLOCK
    pad_end = jnp.cumsum(padded)
    pad_start = pad_end - padded
    start = jnp.cumsum(counts) - counts
    dest = pad_start[se] + jnp.arange(n_assign, dtype=jnp.int32) - start[se]
    n_blocks = -(-n_assign // MOE_BLOCK) + MOE_EXPERTS
    n_rows = n_blocks * MOE_BLOCK
    row_tok = jnp.full((n_rows,), t, jnp.int32).at[dest].set(st)
    block_e = jnp.minimum(jnp.searchsorted(pad_end, jnp.arange(n_blocks, dtype=jnp.int32) * MOE_BLOCK, side='right'),
                          MOE_EXPERTS - 1)
    h_pad = jnp.concatenate([h, jnp.zeros((1, d), h.dtype)], axis=0)
    xin = h_pad[row_tok].reshape(n_blocks, MOE_BLOCK, d)

    def expert_block(args):
        xb, e = args
        return swiglu(xb, lp['w_eg'][e], lp['w_eu'][e], lp['w_ed'][e])

    yb = lax.map(expert_block, (xin, block_e)).reshape(n_rows, d)
    return jax.ops.segment_sum(yb[dest] * sw[:, None].astype(h.dtype), st, num_segments=t)


def setup_inputs(seed: int = 0) -> dict:
    key = jax.random.key(seed)
    ks = jax.random.split(key, 32)
    f32 = jnp.float32
    L, D = DEPTH, D_MODEL

    def nrm(k, shape, scale):
        return jax.random.normal(k, shape, f32) * scale

    dt0 = jnp.exp(jax.random.uniform(ks[8], (L, 2, SSD_HEADS), f32, math.log(1e-3), math.log(1e-1)))
    return {
        'x': nrm(ks[0], (BATCH, SEQ, D), 1.0),
        'c': nrm(ks[1], (BATCH, D), 1.0),
        'ctx': nrm(ks[2], (BATCH, CTX_LEN, D), 1.0),
        'c_ctx': nrm(ks[3], (D,), 1.0),
        'w_ada': nrm(ks[4], (L, D, 6 * D), D ** -0.5),
        'b_ada': nrm(ks[5], (L, 6 * D), 0.02),
        'w_in': nrm(ks[6], (L, D, IN_COLS), D ** -0.5),
        'b_gate': nrm(ks[7], (L, 2 * D), 0.1),
        'conv_w': nrm(ks[9], (L, SSD_CONV, D_XBC), SSD_CONV ** -0.5),
        'conv_b': nrm(ks[10], (L, D_XBC), 0.02),
        'dt_bias': dt0 + jnp.log(-jnp.expm1(-dt0)),
        'a_log': jnp.log(jax.random.uniform(ks[11], (L, 2, SSD_HEADS), f32, 1.0, 16.0)),
        'd_skip': 1.0 + nrm(ks[12], (L, SSD_HEADS), 0.1),
        'ssd_norm_g': 1.0 + nrm(ks[13], (L, D_INNER), 0.1),
        'pool_w': nrm(ks[14], (L, POOL_GROUPS, POOL_GROUP_DIM, POOL_GROUP_DIM), POOL_GROUP_DIM ** -0.5),
        'pool_scale': 1.0 + nrm(ks[15], (L, POOL_WIDTH), 0.1),
        'w_branch_a': nrm(ks[16], (L, D_INNER, D), D_INNER ** -0.5),
        'w_branch_b': nrm(ks[17], (L, POOL_WIDTH, D), POOL_WIDTH ** -0.5),
        'w_out': nrm(ks[18], (L, D, D), D ** -0.5 * DEEPNORM_BETA),
        'ln1_g': 1.0 + nrm(ks[19], (L, D), 0.1),
        'ln1_b': nrm(ks[20], (L, D), 0.02),
        'ln2_g': 1.0 + nrm(ks[21], (L, D), 0.1),
        'ln2_b': nrm(ks[22], (L, D), 0.02),
        'w_router_group': nrm(ks[23], (L, D, MOE_GROUPS), D ** -0.5),
        'b_router_group': nrm(ks[24], (L, MOE_GROUPS), 0.01),
        'w_router_expert': nrm(ks[25], (L, MOE_GROUPS, D, MOE_EXPERTS_PER_GROUP), D ** -0.5),
        'b_router_expert': nrm(ks[26], (L, MOE_GROUPS, MOE_EXPERTS_PER_GROUP), 0.01),
        'w_expert_gate': nrm(ks[27], (L, MOE_EXPERTS, D, D_EXPERT), D ** -0.5),
        'w_expert_up': nrm(ks[28], (L, MOE_EXPERTS, D, D_EXPERT), D ** -0.5),
        'w_expert_down': nrm(ks[29], (L, MOE_EXPERTS, D_EXPERT, D), D_EXPERT ** -0.5 * DEEPNORM_BETA),
    }


def reference(x, c, ctx, c_ctx, w_ada, b_ada, w_in, b_gate, conv_w, conv_b, dt_bias, a_log, d_skip,
              ssd_norm_g, pool_w, pool_scale, w_branch_a, w_branch_b, w_out, ln1_g, ln1_b, ln2_g, ln2_b,
              w_router_group, b_router_group, w_router_expert, b_router_expert,
              w_expert_gate, w_expert_up, w_expert_down):
    b, l, d = x.shape
    lc = ctx.shape[1]
    xl, xc = x, ctx
    zero_state = jnp.zeros((b, SSD_HEADS, SSD_HEADDIM, SSD_STATE), jnp.float32)
    for i in range(DEPTH):
        last = i == DEPTH - 1
        lp = {'w_in': w_in[i], 'b_gate': b_gate[i], 'conv_w': conv_w[i], 'conv_b': conv_b[i],
              'dt_bias': dt_bias[i], 'a_log': a_log[i], 'd_skip': d_skip[i], 'ssd_norm_g': ssd_norm_g[i],
              'pool_w': pool_w[i], 'pool_scale': pool_scale[i], 'w_branch_a': w_branch_a[i],
              'w_branch_b': w_branch_b[i], 'w_out': w_out[i],
              'w_rg': w_router_group[i], 'b_rg': b_router_group[i],
              'w_re': w_router_expert[i], 'b_re': b_router_expert[i],
              'w_eg': w_expert_gate[i], 'w_eu': w_expert_up[i], 'w_ed': w_expert_down[i]}
        mod_l = jnp.split((jax.nn.silu(c) @ w_ada[i] + b_ada[i])[:, None, :], 6, axis=-1)
        mod_c = jnp.split((jax.nn.silu(c_ctx) @ w_ada[i] + b_ada[i])[None, None, :], 6, axis=-1)

        hc = modulate(xc, mod_c[0], mod_c[1])
        hl = modulate(xl, mod_l[0], mod_l[1])
        if last:
            s_f, s_b = ssd_context_states(hc, lp)
        else:
            out_c, s_f, s_b = mix_stream(hc, lp, False, zero_state, zero_state)
            xc = post_norm(xc, mod_c[2] * out_c, ln1_g[i], ln1_b[i])
        out_l, _, _ = mix_stream(hl, lp, True, s_f, s_b)
        xl = post_norm(xl, mod_l[2] * out_l, ln1_g[i], ln1_b[i])

        hl = modulate(xl, mod_l[3], mod_l[4]).reshape(b * l, d)
        if last:
            yl = hier_moe(hl, lp)
        else:
            hc = modulate(xc, mod_c[3], mod_c[4]).reshape(b * lc, d)
            y = hier_moe(jnp.concatenate([hl, hc], axis=0), lp)
            yl, yc = y[:b * l], y[b * l:]
            xc = post_norm(xc, mod_c[5] * yc.reshape(b, lc, d), ln2_g[i], ln2_b[i])
        xl = post_norm(xl, mod_l[5] * yl.reshape(b, l, d), ln2_g[i], ln2_b[i])
    return xl
```

```python
import functools
import math

import jax
import jax.numpy as jnp
from jax import lax
from jax.experimental import pallas as pl
from jax.experimental.pallas import tpu as pltpu

F32 = jnp.float32
BF16 = jnp.bfloat16
I32 = jnp.int32

D_MODEL = 1024
D_INNER = 2048
HEADS = 32
HEADDIM = 64
GROUPS = 4
STATE = 128
D_BC = GROUPS * STATE
D_XBC = D_INNER + 2 * D_BC
CONV_K = 5
CHUNK = 128
GRID_W = 64
POOL_WINDOWS = (2, 4, 8, 16)
POOL_GROUP_DIM = 256
MOE_GROUPS = 4
MOE_EPG = 8
MOE_EXPERTS = 32
D_EXPERT = 512
MOE_BLOCK = 256
DEPTH_ALPHA_POW = 0.25
NORM_EPS = 1e-5
LANES = 128
ROUTE_LANE0 = MOE_GROUPS
VMEM_LIMIT = 56 * 1024 * 1024


def _cparams(sem, vmem=VMEM_LIMIT):
    return pltpu.CompilerParams(dimension_semantics=sem, vmem_limit_bytes=vmem)


def _silu(v):
    return v * jax.nn.sigmoid(v)


def _layer_norm(v, g, b):
    mu = jnp.mean(v, axis=-1, keepdims=True)
    d = v - mu
    var = jnp.mean(d * d, axis=-1, keepdims=True)
    return d * lax.rsqrt(var + NORM_EPS) * g + b


def _split_bf16(v):
    hi = v.astype(BF16)
    lo = (v - hi.astype(F32)).astype(BF16)
    return hi, lo


def _dot(a, b):
    return jnp.dot(a, b, preferred_element_type=F32)


def _ada_kernel(c_ref, w_ref, b_ref, o_ref):
    s = _silu(c_ref[...]).astype(BF16)
    o_ref[0] = _dot(s, w_ref[0].astype(BF16)) + b_ref[0]


def _ada_mods(cvec, w_ada, b_ada):
    depth, d, n = w_ada.shape
    r = cvec.shape[0]
    tn = D_MODEL
    return pl.pallas_call(
        _ada_kernel,
        grid=(depth, n // tn),
        in_specs=[pl.BlockSpec((r, d), lambda l, j: (0, 0)),
                  pl.BlockSpec((1, d, tn), lambda l, j: (l, 0, j)),
                  pl.BlockSpec((1, 1, tn), lambda l, j: (l, 0, j))],
        out_specs=pl.BlockSpec((1, r, tn), lambda l, j: (l, 0, j)),
        out_shape=jax.ShapeDtypeStruct((depth, r, n), F32),
        compiler_params=_cparams(("arbitrary", "arbitrary")),
    )(cvec, w_ada, b_ada.reshape(depth, 1, n))


def _inproj_kernel(x_ref, sh_ref, sc_ref, wz_ref, wx_ref, wu_ref, wg_ref, wdt_ref,
                   z_ref, xbc_ref, u_ref, g_ref, dt_ref):
    h = (x_ref[0] * (1.0 + sc_ref[0]) + sh_ref[0]).astype(BF16)
    z_ref[0] = _dot(h, wz_ref[...]).astype(BF16)
    xbc_ref[0] = _dot(h, wx_ref[...]).astype(BF16)
    u_ref[0] = _dot(h, wu_ref[...]).astype(BF16)
    g_ref[0] = _dot(h, wg_ref[...]).astype(BF16)
    dt_ref[0] = _dot(h, wdt_ref[...])


def _in_proj(x, shift, scale, wts):
    b, l, d = x.shape
    tm = min(256, l)
    wz, wx, wu, wg, wdt = wts
    tok = lambda n: pl.BlockSpec((1, tm, n), lambda i, j: (i, j, 0))
    mod = pl.BlockSpec((1, 1, d), lambda i, j: (i, 0, 0))
    wspec = lambda w: pl.BlockSpec(w.shape, lambda i, j: (0, 0), pipeline_mode=pl.Buffered(1))
    return pl.pallas_call(
        _inproj_kernel,
        grid=(b, l // tm),
        in_specs=[tok(d), mod, mod, wspec(wz), wspec(wx), wspec(wu), wspec(wg), wspec(wdt)],
        out_specs=[tok(D_INNER), tok(D_XBC), tok(D_MODEL), tok(2 * D_MODEL), tok(LANES)],
        out_shape=[jax.ShapeDtypeStruct((b, l, D_INNER), BF16),
                   jax.ShapeDtypeStruct((b, l, D_XBC), BF16),
                   jax.ShapeDtypeStruct((b, l, D_MODEL), BF16),
                   jax.ShapeDtypeStruct((b, l, 2 * D_MODEL), BF16),
                   jax.ShapeDtypeStruct((b, l, LANES), F32)],
        compiler_params=_cparams(("arbitrary", "arbitrary")),
    )(x, shift, scale, wz, wx, wu, wg, wdt)


HALO = 16


def _conv_act(prev_ref, cur_ref, next_ref, ext_ref, cw_ref, cb_ref, c, nc, ncols):
    q = CHUNK
    prev8 = prev_ref[0].astype(F32)[HALO - 8:HALO, :ncols]
    next8 = next_ref[0].astype(F32)[0:8, :ncols]
    ext_ref[0:8, :ncols] = jnp.where(c > 0, prev8, 0.0)
    ext_ref[8:8 + q, :ncols] = cur_ref[0].astype(F32)[:, :ncols]
    ext_ref[8 + q:16 + q, :ncols] = jnp.where(c < nc - 1, next8, 0.0)
    acc = jnp.broadcast_to(cb_ref[:, :ncols], (q, ncols))
    for k in range(CONV_K):
        off = 8 - CONV_K // 2 + k
        acc = acc + ext_ref[off:off + q, :ncols] * cw_ref[k:k + 1, :ncols]
    return _silu(acc)


def _dt_prep(dt_ref, dtb_ref, alog_ref):
    q = CHUNK
    lane = lax.broadcasted_iota(I32, (q, LANES), 1)
    raw = dt_ref[0] + dtb_ref[...]
    dt = jnp.maximum(raw, 0.0) + jnp.log1p(jnp.exp(-jnp.abs(raw)))
    dt = jnp.where(lane < 2 * HEADS, dt, 0.0)
    a = dt * (-jnp.exp(alog_ref[...]))
    ii = lax.broadcasted_iota(I32, (q, q), 0)
    jj = lax.broadcasted_iota(I32, (q, q), 1)
    tri = (jj <= ii).astype(BF16)
    a_hi, a_lo = _split_bf16(a)
    acs = _dot(tri, a_hi) + _dot(tri, a_lo)
    tot = acs[q - 1:q, :]
    return dt, a, acs, tot


def _expand(v, lane0, r_ref):
    lane = lax.broadcasted_iota(I32, v.shape, 1)
    vm = jnp.where((lane >= lane0) & (lane < lane0 + HEADS), v, 0.0)
    hi, lo = _split_bf16(vm)
    return _dot(hi, r_ref[...]) + _dot(lo, r_ref[...])


def _state_update(st_ref, bm, xw_bf, decay_x):
    gw = D_INNER // GROUPS
    for g in range(GROUPS):
        bgt = bm[:, g * STATE:(g + 1) * STATE].T.astype(BF16)
        cols = slice(g * gw, (g + 1) * gw)
        st_ref[:, cols] = st_ref[:, cols] * decay_x[:, cols] + _dot(bgt, xw_bf[:, cols])


def _ssd_state_kernel(*refs, rev, emit):
    (prev_ref, cur_ref, next_ref, dt_ref, h0_ref, cw_ref, cb_ref, dtb_ref, alog_ref, r_ref) = refs[:10]
    if emit:
        stout_ref, sfin_ref, ext_ref, st_ref = refs[10:]
    else:
        sfin_ref, ext_ref, st_ref = refs[10:]
    s = pl.program_id(1)
    nc = pl.num_programs(1)
    c = (nc - 1 - s) if rev else s
    ncols = D_INNER + D_BC

    @pl.when(s == 0)
    def _():
        st_ref[...] = h0_ref[0]

    if emit:
        stout_ref[0, 0] = st_ref[...].astype(BF16)

    act = _conv_act(prev_ref, cur_ref, next_ref, ext_ref, cw_ref, cb_ref, c, nc, ncols)
    xs = act[:, :D_INNER]
    bm = act[:, D_INNER:]
    dt, a, acs, tot = _dt_prep(dt_ref, dtb_ref, alog_ref)
    if rev:
        w = jnp.exp(acs - a) * dt
        lane0 = HEADS
    else:
        w = jnp.exp(tot - acs) * dt
        lane0 = 0
    w_x = _expand(w, lane0, r_ref)
    dec_x = _expand(jnp.broadcast_to(jnp.exp(tot), (16, LANES)), lane0, r_ref)[0:1]
    _state_update(st_ref, bm, (w_x * xs).astype(BF16), dec_x)

    @pl.when(s == nc - 1)
    def _():
        sfin_ref[0] = st_ref[...]


def _ssd_state_scan(xbc, dt, h0, lp, r_mat, *, rev, emit):
    b, l, _ = xbc.shape
    nc = l // CHUNK
    nh = l // HALO
    hpc = CHUNK // HALO
    seq = (lambda s: nc - 1 - s) if rev else (lambda s: s)
    par = lambda shape: pl.BlockSpec(shape, lambda i, s: (0,) * len(shape))
    in_specs = [
        pl.BlockSpec((1, HALO, D_XBC), lambda i, s: (i, jnp.maximum(seq(s) * hpc - 1, 0), 0)),
        pl.BlockSpec((1, CHUNK, D_XBC), lambda i, s: (i, seq(s), 0)),
        pl.BlockSpec((1, HALO, D_XBC), lambda i, s: (i, jnp.minimum((seq(s) + 1) * hpc, nh - 1), 0)),
        pl.BlockSpec((1, CHUNK, LANES), lambda i, s: (i, seq(s), 0)),
        pl.BlockSpec((1, STATE, D_INNER), lambda i, s: (i, 0, 0)),
        par((CONV_K, D_XBC)), par((1, D_XBC)), par((1, LANES)), par((1, LANES)), par((LANES, D_INNER)),
    ]
    out_specs = [pl.BlockSpec((1, STATE, D_INNER), lambda i, s: (i, 0, 0))]
    out_shape = [jax.ShapeDtypeStruct((b, STATE, D_INNER), F32)]
    if emit:
        out_specs.insert(0, pl.BlockSpec((1, 1, STATE, D_INNER), lambda i, s: (i, seq(s), 0, 0)))
        out_shape.insert(0, jax.ShapeDtypeStruct((b, nc, STATE, D_INNER), BF16))
    return pl.pallas_call(
        functools.partial(_ssd_state_kernel, rev=rev, emit=emit),
        grid=(b, nc),
        in_specs=in_specs, out_specs=out_specs, out_shape=out_shape,
        scratch_shapes=[pltpu.VMEM((CHUNK + 16, D_XBC), F32), pltpu.VMEM((STATE, D_INNER), F32)],
        compiler_params=_cparams(("arbitrary", "arbitrary")),
    )(xbc, xbc, xbc, dt, h0, lp["conv_w"], lp["conv_b"], lp["dt_bias"], lp["a_log"], r_mat)


def _ssd_main_kernel(prev_ref, cur_ref, next_ref, dt_ref, z_ref, stb_ref, h0_ref,
                     cw_ref, cb_ref, dtb_ref, alog_ref, dsk_ref, ng_ref, rf_ref, rb_ref,
                     y_ref, sfin_ref, ext_ref, st_ref, ys_ref):
    q = CHUNK
    c = pl.program_id(1)
    nc = pl.num_programs(1)

    @pl.when(c == 0)
    def _():
        st_ref[...] = h0_ref[0]

    act = _conv_act(prev_ref, cur_ref, next_ref, ext_ref, cw_ref, cb_ref, c, nc, D_XBC)
    xs = act[:, :D_INNER]
    bm = act[:, D_INNER:D_INNER + D_BC]
    cm = act[:, D_INNER + D_BC:]
    dt, a, acs, tot = _dt_prep(dt_ref, dtb_ref, alog_ref)
    eb = acs - a

    ef_x = _expand(jnp.exp(acs), 0, rf_ref)
    eb_x = _expand(jnp.exp(tot - eb), HEADS, rb_ref)
    tf_x = _expand(jnp.exp(tot - acs) * dt, 0, rf_ref)
    dec_x = _expand(jnp.broadcast_to(jnp.exp(tot), (16, LANES)), 0, rf_ref)[0:1]

    lane = lax.broadcasted_iota(I32, (q, LANES), 1)
    qm = jnp.where(lane < HEADS, acs, jnp.where(lane < 2 * HEADS, eb, pltpu.roll(dt, 2 * HEADS, 1)))
    qt = qm.T
    ii = lax.broadcasted_iota(I32, (q, q), 0)
    jj = lax.broadcasted_iota(I32, (q, q), 1)
    lower = jj <= ii
    diag = jj == ii
    lane2 = lax.broadcasted_iota(I32, (q, LANES), 1)
    left = lane2 < HEADDIM

    xs_bf = xs.astype(BF16)
    stf_bf = st_ref[...].astype(BF16)
    gw = D_INNER // GROUPS
    hpg = HEADS // GROUPS
    for g in range(GROUPS):
        bg = bm[:, g * STATE:(g + 1) * STATE].astype(BF16)
        cg = cm[:, g * STATE:(g + 1) * STATE].astype(BF16)
        cb = lax.dot_general(cg, bg, (((1,), (1,)), ((), ())), preferred_element_type=F32)
        cols = slice(g * gw, (g + 1) * gw)
        y_off = ef_x[:, cols] * _dot(cg, stf_bf[:, cols]) + eb_x[:, cols] * _dot(cg, stb_ref[0, 0][:, cols])
        for pr in range(hpg // 2):
            ms = []
            for hh in range(2):
                h = g * hpg + pr * 2 + hh
                afc = acs[:, h:h + 1]
                afr = qt[h:h + 1, :]
                ebc = eb[:, HEADS + h:HEADS + h + 1]
                ebr = qt[HEADS + h:HEADS + h + 1, :]
                wf = qt[2 * HEADS + h:2 * HEADS + h + 1, :]
                wb = qt[3 * HEADS + h:3 * HEADS + h + 1, :]
                arg = jnp.where(lower, afc - afr, ebr - ebc)
                wgt = jnp.where(lower, wf, wb) + jnp.where(diag, wb, 0.0)
                ms.append((cb * jnp.exp(arg) * wgt).astype(BF16))
            lhs = jnp.concatenate(ms, axis=1)
            c0 = g * gw + pr * 2 * HEADDIM
            xp = xs_bf[:, c0:c0 + 2 * HEADDIM]
            zero = jnp.zeros_like(xp)
            rhs = jnp.concatenate([jnp.where(left, xp, zero), jnp.where(left, zero, xp)], axis=0)
            y_pair = _dot(lhs, rhs) + y_off[:, pr * 2 * HEADDIM:(pr + 1) * 2 * HEADDIM]
            y_pair = y_pair + dsk_ref[:, c0:c0 + 2 * HEADDIM] * xs[:, c0:c0 + 2 * HEADDIM]
            ys_ref[:, c0:c0 + 2 * HEADDIM] = y_pair

    y = ys_ref[...]
    yz = y * _silu(z_ref[0].astype(F32))
    ms2 = jnp.mean(yz * yz, axis=-1, keepdims=True)
    y_ref[0] = (yz * lax.rsqrt(ms2 + NORM_EPS) * ng_ref[...]).astype(BF16)

    _state_update(st_ref, bm, (tf_x * xs).astype(BF16), dec_x)

    @pl.when(c == nc - 1)
    def _():
        sfin_ref[0] = st_ref[...]


def _ssd_main(xbc, dt, z, stb, h0, lp, r_f, r_b):
    b, l, _ = xbc.shape
    nc = l // CHUNK
    nh = l // HALO
    hpc = CHUNK // HALO
    par = lambda shape: pl.BlockSpec(shape, lambda i, s: (0,) * len(shape))
    in_specs = [
        pl.BlockSpec((1, HALO, D_XBC), lambda i, s: (i, jnp.maximum(s * hpc - 1, 0), 0)),
        pl.BlockSpec((1, CHUNK, D_XBC), lambda i, s: (i, s, 0)),
        pl.BlockSpec((1, HALO, D_XBC), lambda i, s: (i, jnp.minimum((s + 1) * hpc, nh - 1), 0)),
        pl.BlockSpec((1, CHUNK, LANES), lambda i, s: (i, s, 0)),
        pl.BlockSpec((1, CHUNK, D_INNER), lambda i, s: (i, s, 0)),
        pl.BlockSpec((1, 1, STATE, D_INNER), lambda i, s: (i, s, 0, 0)),
        pl.BlockSpec((1, STATE, D_INNER), lambda i, s: (i, 0, 0)),
        par((CONV_K, D_XBC)), par((1, D_XBC)), par((1, LANES)), par((1, LANES)),
        par((1, D_INNER)), par((1, D_INNER)), par((LANES, D_INNER)), par((LANES, D_INNER)),
    ]
    return pl.pallas_call(
        _ssd_main_kernel,
        grid=(b, nc),
        in_specs=in_specs,
        out_specs=[pl.BlockSpec((1, CHUNK, D_INNER), lambda i, s: (i, s, 0)),
                   pl.BlockSpec((1, STATE, D_INNER), lambda i, s: (i, 0, 0))],
        out_shape=[jax.ShapeDtypeStruct((b, l, D_INNER), BF16),
                   jax.ShapeDtypeStruct((b, STATE, D_INNER), F32)],
        scratch_shapes=[pltpu.VMEM((CHUNK + 16, D_XBC), F32), pltpu.VMEM((STATE, D_INNER), F32),
                        pltpu.VMEM((CHUNK, D_INNER), F32)],
        compiler_params=_cparams(("arbitrary", "arbitrary")),
    )(xbc, xbc, xbc, dt, z, stb, h0, lp["conv_w"], lp["conv_b"], lp["dt_bias"], lp["a_log"],
      lp["d_skip_x"], lp["ssd_norm_g"], r_f, r_b)


def _pool_kernel(*refs, width, vertical, halo, hblock):
    if vertical:
        prev_ref, cur_ref, next_ref, pw_ref, ps_ref, o_ref = refs
    else:
        cur_ref, pw_ref, ps_ref, o_ref = refs
    t = pl.program_id(1)
    nt = pl.num_programs(1)
    tp = cur_ref.shape[1]
    n_tok = nt * tp
    shift = int(math.log2(width))
    idx = lax.broadcasted_iota(I32, (tp, 1), 0) + t * tp
    colpos = idx & (width - 1)
    rowpos = idx >> shift
    n_rows = n_tok // width
    bi = lax.broadcasted_iota(I32, (hblock, hblock), 0)
    bj = lax.broadcasted_iota(I32, (hblock, hblock), 1)
    same_row = (bi >> shift) == (bj >> shift)
    cur = cur_ref[0].astype(F32)
    if vertical:
        prev = jnp.where(t > 0, prev_ref[0].astype(F32), 0.0)
        nxt = jnp.where(t < nt - 1, next_ref[0].astype(F32), 0.0)
    for gi, k in enumerate(POOL_WINDOWS):
        cols = slice(gi * POOL_GROUP_DIM, (gi + 1) * POOL_GROUP_DIM)
        ug = cur[:, cols]
        lo = k // 2
        if vertical:
            e = jnp.concatenate([prev[:, cols], ug, nxt[:, cols]], axis=0)
            step = width
            m = 1
            while m < k:
                n = e.shape[0] - step
                e = e[0:n] + e[step:step + n]
                step *= 2
                m *= 2
            start = halo - lo * width
            s = e[start:start + tp]
            cnt_r = (jnp.minimum(rowpos - lo + k, n_rows) - jnp.maximum(rowpos - lo, 0)).astype(F32)
        else:
            s = ug
            cnt_r = jnp.ones((tp, 1), F32)
        band = (same_row & (bj - bi >= -lo) & (bj - bi < k - lo)).astype(BF16)
        s_bf = s.astype(BF16)
        hs = [_dot(band, s_bf[r * hblock:(r + 1) * hblock]) for r in range(tp // hblock)]
        hsum = hs[0] if len(hs) == 1 else jnp.concatenate(hs, axis=0)
        cnt_c = (jnp.minimum(colpos - lo + k, width) - jnp.maximum(colpos - lo, 0)).astype(F32)
        mean = hsum / (cnt_r * cnt_c)
        y = _dot((mean - ug).astype(BF16), pw_ref[gi])
        o_ref[0, :, cols] = (y * ps_ref[:, cols]).astype(BF16)


def _pool_branch(u, lp, on_grid):
    b, l, d = u.shape
    pw, ps = lp["pool_w"], lp["pool_scale"]
    if on_grid:
        width, halo = GRID_W, (max(POOL_WINDOWS) // 2) * GRID_W
        tp = min(1024, l)
        hb = tp // halo
        nhb = l // halo
        in_specs = [pl.BlockSpec((1, halo, d), lambda i, t: (i, jnp.maximum(t * hb - 1, 0), 0)),
                    pl.BlockSpec((1, tp, d), lambda i, t: (i, t, 0)),
                    pl.BlockSpec((1, halo, d), lambda i, t: (i, jnp.minimum((t + 1) * hb, nhb - 1), 0))]
        args = (u, u, u)
        kern = functools.partial(_pool_kernel, width=width, vertical=True, halo=halo, hblock=LANES)
    else:
        tp = l
        in_specs = [pl.BlockSpec((1, tp, d), lambda i, t: (i, t, 0))]
        args = (u,)
        kern = functools.partial(_pool_kernel, width=l, vertical=False, halo=0, hblock=l)
    in_specs += [pl.BlockSpec(pw.shape, lambda i, t: (0, 0, 0)), pl.BlockSpec(ps.shape, lambda i, t: (0, 0))]
    return pl.pallas_call(
        kern,
        grid=(b, l // tp),
        in_specs=in_specs,
        out_specs=pl.BlockSpec((1, tp, d), lambda i, t: (i, t, 0)),
        out_shape=jax.ShapeDtypeStruct((b, l, d), BF16),
        compiler_params=_cparams(("arbitrary", "arbitrary")),
    )(*args, pw, ps)


def _merge_kernel(ya_ref, yp_ref, gt_ref, x_ref, g1_ref, sh2_ref, sc2_ref, bg_ref,
                  wa_ref, wb_ref, wo_ref, lng_ref, lnb_ref, wrh_ref, wrl_ref, br_ref, cin_ref,
                  xo_ref, h2_ref, rt_ref, cout_ref, cnt_ref, *, alpha):
    i = pl.program_id(0)
    j = pl.program_id(1)
    tm = x_ref.shape[1]

    @pl.when((i == 0) & (j == 0))
    def _():
        cnt_ref[...] = cin_ref[...]

    gate = jax.nn.sigmoid(gt_ref[0].astype(F32) + bg_ref[...])
    merged = (gate[:, :D_MODEL] * _dot(ya_ref[0], wa_ref[...])
              + gate[:, D_MODEL:] * _dot(yp_ref[0], wb_ref[...]))
    out = _dot(merged.astype(BF16), wo_ref[...])
    xn = _layer_norm(alpha * x_ref[0] + g1_ref[0] * out, lng_ref[...], lnb_ref[...])
    xo_ref[0] = xn
    h2 = xn * (1.0 + sc2_ref[0]) + sh2_ref[0]
    h2_ref[0] = h2

    h_hi, h_lo = _split_bf16(h2)
    logits = _dot(h_hi, wrh_ref[...]) + _dot(h_lo, wrh_ref[...]) + _dot(h_hi, wrl_ref[...]) + br_ref[...]
    lane = lax.broadcasted_iota(I32, (tm, LANES), 1)
    neg = jnp.float32(-jnp.inf)
    lg = jnp.where(lane < MOE_GROUPS, logits, neg)
    mg = jnp.max(lg, axis=-1, keepdims=True)
    grp = jnp.min(jnp.where(lg == mg, lane, LANES), axis=-1, keepdims=True)
    p_grp = 1.0 / jnp.sum(jnp.exp(lg - mg), axis=-1, keepdims=True)
    lo_lane = ROUTE_LANE0 + grp * MOE_EPG
    le = jnp.where((lane >= lo_lane) & (lane < lo_lane + MOE_EPG), logits, neg)
    v1 = jnp.max(le, axis=-1, keepdims=True)
    i1 = jnp.min(jnp.where(le == v1, lane, LANES), axis=-1, keepdims=True)
    le2 = jnp.where(lane == i1, neg, le)
    v2 = jnp.max(le2, axis=-1, keepdims=True)
    i2 = jnp.min(jnp.where(le2 == v2, lane, LANES), axis=-1, keepdims=True)
    e2 = jnp.exp(v2 - v1)
    w1 = p_grp / (1.0 + e2)
    w2 = p_grp * e2 / (1.0 + e2)

    oh1 = (lane == i1).astype(F32)
    oh2 = (lane == i2).astype(F32)
    oh = oh1 + oh2
    ri = lax.broadcasted_iota(I32, (tm, tm), 0)
    rj = lax.broadcasted_iota(I32, (tm, tm), 1)
    before = _dot((rj < ri).astype(BF16), oh.astype(BF16)) + cnt_ref[...]
    rank1 = jnp.sum(oh1 * before, axis=-1, keepdims=True)
    rank2 = jnp.sum(oh2 * before, axis=-1, keepdims=True)
    cnt_ref[...] = cnt_ref[...] + jnp.sum(oh, axis=0, keepdims=True)
    cout_ref[...] = cnt_ref[...]

    slab = jnp.where(lane == 0, (i1 - ROUTE_LANE0).astype(F32),
           jnp.where(lane == 1, (i2 - ROUTE_LANE0).astype(F32),
           jnp.where(lane == 2, w1,
           jnp.where(lane == 3, w2,
           jnp.where(lane == 4, rank1,
           jnp.where(lane == 5, rank2, 0.0))))))
    rt_ref[0] = slab


def _merge(ya, yp, gates, x, g1, sh2, sc2, lp, cnt_in, alpha):
    b, l, d = x.shape
    tm = min(512, l)
    tok = lambda n: pl.BlockSpec((1, tm, n), lambda i, j: (i, j, 0))
    mod = pl.BlockSpec((1, 1, d), lambda i, j: (i, 0, 0))
    par = lambda a: pl.BlockSpec(a.shape, lambda i, j: (0,) * a.ndim)
    params = (lp["b_gate"], lp["w_branch_a"], lp["w_branch_b"], lp["w_out"], lp["ln1_g"], lp["ln1_b"],
              lp["w_r_hi"], lp["w_r_lo"], lp["b_r"], cnt_in)
    return pl.pallas_call(
        functools.partial(_merge_kernel, alpha=alpha),
        grid=(b, l // tm),
        in_specs=[tok(D_INNER), tok(d), tok(2 * d), tok(d), mod, mod, mod] + [par(a) for a in params],
        out_specs=[tok(d), tok(d), tok(LANES), pl.BlockSpec((1, LANES), lambda i, j: (0, 0))],
        out_shape=[jax.ShapeDtypeStruct((b, l, d), F32), jax.ShapeDtypeStruct((b, l, d), F32),
                   jax.ShapeDtypeStruct((b, l, LANES), F32), jax.ShapeDtypeStruct((1, LANES), F32)],
        scratch_shapes=[pltpu.VMEM((1, LANES), F32)],
        compiler_params=_cparams(("arbitrary", "arbitrary")),
    )(ya, yp, gates, x, g1, sh2, sc2, *params)


def _row_copy(src_ref, src_row, dst_ref, dst_row, sem):
    return pltpu.make_async_copy(src_ref.at[pl.ds(src_row, 1)], dst_ref.at[pl.ds(dst_row, 1)], sem)


def _dispatch_kernel(h_ref, dest_ref, xin_any, xout_any, idx_smem, sem_idx, sem_row):
    del xin_any
    td = h_ref.shape[0]
    cp = pltpu.make_async_copy(dest_ref.at[0, 0], idx_smem, sem_idx)
    cp.start()
    cp.wait()

    def issue(r, carry):
        _row_copy(h_ref, r, xout_any, idx_smem[2 * r], sem_row).start()
        _row_copy(h_ref, r, xout_any, idx_smem[2 * r + 1], sem_row).start()
        return carry

    lax.fori_loop(0, td, issue, 0)

    def drain(r, carry):
        _row_copy(h_ref, 0, xout_any, 0, sem_row).wait()
        _row_copy(h_ref, 0, xout_any, 0, sem_row).wait()
        return carry

    lax.fori_loop(0, td, drain, 0)


def _dispatch(h2, dest, xin):
    t, d = h2.shape
    td = min(256, t)
    dest3 = dest.reshape(t // td, 1, 2 * td)
    return pl.pallas_call(
        _dispatch_kernel,
        grid=(t // td,),
        in_specs=[pl.BlockSpec((td, d), lambda i: (i, 0)),
                  pl.BlockSpec((1, 1, 2 * td), lambda i: (i, 0, 0)),
                  pl.BlockSpec(memory_space=pl.ANY)],
        out_specs=pl.BlockSpec(memory_space=pl.ANY),
        out_shape=jax.ShapeDtypeStruct(xin.shape, xin.dtype),
        scratch_shapes=[pltpu.SMEM((2 * td,), I32), pltpu.SemaphoreType.DMA(()), pltpu.SemaphoreType.DMA(())],
        input_output_aliases={2: 0},
        compiler_params=_cparams(("arbitrary",)),
    )(h2, dest3, xin)


def _expert_kernel(be_ref, nu_ref, x_ref, wg_ref, wu_ref, wd_ref, o_ref):
    i = pl.program_id(0)

    @pl.when(i < nu_ref[0])
    def _():
        xb = x_ref[...].astype(BF16)
        gte = _dot(xb, wg_ref[0])
        up = _dot(xb, wu_ref[0])
        o_ref[...] = _dot((_silu(gte) * up).astype(BF16), wd_ref[0])

    @pl.when(i >= nu_ref[0])
    def _():
        o_ref[...] = jnp.zeros_like(o_ref)


def _experts(xin, block_e, n_used, lp):
    n_rows, d = xin.shape
    nb = n_rows // MOE_BLOCK
    grid_spec = pltpu.PrefetchScalarGridSpec(
        num_scalar_prefetch=2,
        grid=(nb,),
        in_specs=[pl.BlockSpec((MOE_BLOCK, d), lambda i, be, nu: (i, 0)),
                  pl.BlockSpec((1, d, D_EXPERT), lambda i, be, nu: (be[i], 0, 0)),
                  pl.BlockSpec((1, d, D_EXPERT), lambda i, be, nu: (be[i], 0, 0)),
                  pl.BlockSpec((1, D_EXPERT, d), lambda i, be, nu: (be[i], 0, 0))],
        out_specs=pl.BlockSpec((MOE_BLOCK, d), lambda i, be, nu: (i, 0)),
    )
    return pl.pallas_call(
        _expert_kernel,
        grid_spec=grid_spec,
        out_shape=jax.ShapeDtypeStruct((n_rows, d), F32),
        compiler_params=_cparams(("arbitrary",)),
    )(block_e, n_used, xin, lp["w_eg"], lp["w_eu"], lp["w_ed"])


def _combine_kernel(dest_ref, rt_ref, x_ref, g2_ref, lng_ref, lnb_ref, yb_any,
                    o_ref, buf_ref, idx_smem, sem_idx, sem_row, *, alpha):
    tc = x_ref.shape[1]
    cp = pltpu.make_async_copy(dest_ref.at[0, 0], idx_smem, sem_idx)
    cp.start()
    cp.wait()

    def issue(r, carry):
        _row_copy(yb_any, idx_smem[2 * r], buf_ref.at[0], r, sem_row).start()
        _row_copy(yb_any, idx_smem[2 * r + 1], buf_ref.at[1], r, sem_row).start()
        return carry

    lax.fori_loop(0, tc, issue, 0)

    def drain(r, carry):
        _row_copy(yb_any, 0, buf_ref.at[0], 0, sem_row).wait()
        _row_copy(yb_any, 0, buf_ref.at[1], 0, sem_row).wait()
        return carry

    lax.fori_loop(0, tc, drain, 0)

    rt = rt_ref[0]
    y = rt[:, 2:3] * buf_ref[0] + rt[:, 3:4] * buf_ref[1]
    o_ref[0] = _layer_norm(alpha * x_ref[0] + g2_ref[0] * y, lng_ref[...], lnb_ref[...])


def _combine(dest, route, x, g2, lp, yb, alpha):
    b, l, d = x.shape
    tc = min(256, l)
    npb = l // tc
    dest3 = dest.reshape(b * npb, 1, 2 * tc)
    par = lambda a: pl.BlockSpec(a.shape, lambda i, j: (0,) * a.ndim)
    return pl.pallas_call(
        functools.partial(_combine_kernel, alpha=alpha),
        grid=(b, npb),
        in_specs=[pl.BlockSpec((1, 1, 2 * tc), lambda i, j: (i * npb + j, 0, 0)),
                  pl.BlockSpec((1, tc, LANES), lambda i, j: (i, j, 0)),
                  pl.BlockSpec((1, tc, d), lambda i, j: (i, j, 0)),
                  pl.BlockSpec((1, 1, d), lambda i, j: (i, 0, 0)),
                  par(lp["ln2_g"]), par(lp["ln2_b"]),
                  pl.BlockSpec(memory_space=pl.ANY)],
        out_specs=pl.BlockSpec((1, tc, d), lambda i, j: (i, j, 0)),
        out_shape=jax.ShapeDtypeStruct((b, l, d), F32),
        scratch_shapes=[pltpu.VMEM((2, tc, d), F32), pltpu.SMEM((2 * tc,), I32),
                        pltpu.SemaphoreType.DMA(()), pltpu.SemaphoreType.DMA(())],
        compiler_params=_cparams(("arbitrary", "arbitrary")),
    )(dest3, route, x, g2, lp["ln2_g"], lp["ln2_b"], yb)


def _moe_plan(routes, counts):
    cnt = counts[0, ROUTE_LANE0:ROUTE_LANE0 + MOE_EXPERTS].astype(I32)
    padded = (cnt + MOE_BLOCK - 1) // MOE_BLOCK * MOE_BLOCK
    pad_end = jnp.cumsum(padded)
    pad_start = pad_end - padded
    n_assign = sum(r.shape[0] * r.shape[1] for r in routes) * 2
    n_blocks = -(-n_assign // MOE_BLOCK) + MOE_EXPERTS
    block_e = jnp.minimum(jnp.searchsorted(pad_end, jnp.arange(n_blocks, dtype=I32) * MOE_BLOCK, side="right"),
                          MOE_EXPERTS - 1).astype(I32)
    n_used = (pad_end[-1:] // MOE_BLOCK).astype(I32)
    eidx = jnp.arange(MOE_EXPERTS, dtype=I32)
    dests = []
    for r in routes:
        e = r[..., 0:2].astype(I32)
        rank = r[..., 4:6].astype(I32)
        start = jnp.sum(jnp.where(e[..., None] == eidx, pad_start, 0), axis=-1)
        dests.append((start + rank).reshape(-1, 2))
    return dests, block_e, n_used, n_blocks * MOE_BLOCK


def _layer_params(i, p):
    w_in = p["w_in"][i]
    s0, s1, s2, s3 = D_INNER, D_INNER + D_XBC, D_INNER + D_XBC + 2 * HEADS, D_INNER + D_XBC + 2 * HEADS + D_MODEL
    pad = LANES - 2 * HEADS
    wdt = jnp.pad(w_in[:, s1:s2], ((0, 0), (0, pad)))
    w_r = jnp.concatenate([p["w_router_group"][i],
                           jnp.transpose(p["w_router_expert"][i], (1, 0, 2)).reshape(D_MODEL, MOE_EXPERTS)], axis=1)
    w_r = jnp.pad(w_r, ((0, 0), (0, LANES - w_r.shape[1])))
    w_r_hi = w_r.astype(BF16)
    b_r = jnp.concatenate([p["b_router_group"][i], p["b_router_expert"][i].reshape(-1)])
    return {
        "w_in": tuple(w.astype(BF16) for w in (w_in[:, :s0], w_in[:, s0:s1], w_in[:, s2:s3], w_in[:, s3:], wdt)),
        "conv_w": p["conv_w"][i], "conv_b": p["conv_b"][i][None],
        "dt_bias": jnp.pad(p["dt_bias"][i].reshape(1, -1), ((0, 0), (0, pad))),
        "a_log": jnp.pad(p["a_log"][i].reshape(1, -1), ((0, 0), (0, pad))),
        "d_skip_x": jnp.repeat(p["d_skip"][i], HEADDIM)[None],
        "ssd_norm_g": p["ssd_norm_g"][i][None],
        "pool_w": p["pool_w"][i].astype(BF16), "pool_scale": p["pool_scale"][i][None],
        "b_gate": p["b_gate"][i][None],
        "w_branch_a": p["w_branch_a"][i].astype(BF16), "w_branch_b": p["w_branch_b"][i].astype(BF16),
        "w_out": p["w_out"][i].astype(BF16),
        "ln1_g": p["ln1_g"][i][None], "ln1_b": p["ln1_b"][i][None],
        "ln2_g": p["ln2_g"][i][None], "ln2_b": p["ln2_b"][i][None],
        "w_r_hi": w_r_hi, "w_r_lo": (w_r - w_r_hi.astype(F32)).astype(BF16),
        "b_r": jnp.pad(b_r, (0, LANES - b_r.shape[0]))[None],
        "w_eg": p["w_expert_gate"][i].astype(BF16), "w_eu": p["w_expert_up"][i].astype(BF16),
        "w_ed": p["w_expert_down"][i].astype(BF16),
    }


def _head_expanders():
    col_head = jnp.arange(D_INNER, dtype=I32) // HEADDIM
    row = jnp.arange(LANES, dtype=I32)[:, None]
    return (row == col_head).astype(BF16), (row == col_head + HEADS).astype(BF16)


def kernel(x, c, ctx, c_ctx, w_ada, b_ada, w_in, b_gate, conv_w, conv_b, dt_bias, a_log, d_skip, ssd_norm_g, pool_w, pool_scale, w_branch_a, w_branch_b, w_out, ln1_g, ln1_b, ln2_g, ln2_b, w_router_group, b_router_group, w_router_expert, b_router_expert, w_expert_gate, w_expert_up, w_expert_down):
    p = dict(w_in=w_in, b_gate=b_gate, conv_w=conv_w, conv_b=conv_b, dt_bias=dt_bias, a_log=a_log, d_skip=d_skip,
             ssd_norm_g=ssd_norm_g, pool_w=pool_w, pool_scale=pool_scale, w_branch_a=w_branch_a,
             w_branch_b=w_branch_b, w_out=w_out, ln1_g=ln1_g, ln1_b=ln1_b, ln2_g=ln2_g, ln2_b=ln2_b,
             w_router_group=w_router_group, b_router_group=b_router_group, w_router_expert=w_router_expert,
             b_router_expert=b_router_expert, w_expert_gate=w_expert_gate, w_expert_up=w_expert_up,
             w_expert_down=w_expert_down)
    b, l, d = x.shape
    depth = w_ada.shape[0]
    alpha = (2.0 * depth) ** DEPTH_ALPHA_POW
    n_mod = -(-(b + 1) // 8) * 8
    cvec = jnp.zeros((n_mod, d), F32).at[:b].set(c).at[b].set(c_ctx)
    mods = _ada_mods(cvec, w_ada, b_ada)
    r_f, r_b = _head_expanders()
    zero_state = jnp.zeros((b, STATE, D_INNER), F32)
    zero_cnt = jnp.zeros((1, LANES), F32)
    xl, xc = x, ctx
    for i in range(depth):
        last = i == depth - 1
        lp = _layer_params(i, p)
        mod_l = [mods[i, :b, k * d:(k + 1) * d][:, None, :] for k in range(6)]
        mod_c = [jnp.broadcast_to(mods[i, b, k * d:(k + 1) * d], (b, 1, d)) for k in range(6)]

        zc, xbc_c, uc, gc, dtc = _in_proj(xc, mod_c[0], mod_c[1], lp["w_in"])
        if last:
            (s_f,) = _ssd_state_scan(xbc_c, dtc, zero_state, lp, r_f, rev=False, emit=False)
            (s_b,) = _ssd_state_scan(xbc_c, dtc, zero_state, lp, r_b, rev=True, emit=False)
        else:
            stb_c, s_b = _ssd_state_scan(xbc_c, dtc, zero_state, lp, r_b, rev=True, emit=True)
            ya_c, s_f = _ssd_main(xbc_c, dtc, zc, stb_c, zero_state, lp, r_f, r_b)
            yp_c = _pool_branch(uc, lp, False)

        zl, xbc_l, ul, gl, dtl = _in_proj(xl, mod_l[0], mod_l[1], lp["w_in"])
        stb_l, _ = _ssd_state_scan(xbc_l, dtl, s_b, lp, r_b, rev=True, emit=True)
        ya_l, _ = _ssd_main(xbc_l, dtl, zl, stb_l, s_f, lp, r_f, r_b)
        yp_l = _pool_branch(ul, lp, True)
        xl, h2_l, rt_l, cnt = _merge(ya_l, yp_l, gl, xl, mod_l[2], mod_l[3], mod_l[4], lp, zero_cnt, alpha)
        streams = [(h2_l, rt_l)]
        if not last:
            xc, h2_c, rt_c, cnt = _merge(ya_c, yp_c, gc, xc, mod_c[2], mod_c[3], mod_c[4], lp, cnt, alpha)
            streams.append((h2_c, rt_c))

        dests, block_e, n_used, n_rows = _moe_plan([rt for _, rt in streams], cnt)
        xin = jnp.zeros((n_rows, d), F32)
        for (h2, _), dest in zip(streams, dests):
            xin = _dispatch(h2.reshape(-1, d), dest, xin)
        yb = _experts(xin, block_e, n_used, lp)
        xl = _combine(dests[0], rt_l, xl, mod_l[5], lp, yb, alpha)
        if not last:
            xc = _combine(dests[1], rt_c, xc, mod_c[5], lp, yb, alpha)
    return xl
```

```python
import functools
import math

import jax
import jax.numpy as jnp
from jax import lax
from jax.experimental import pallas as pl
from jax.experimental.pallas import tpu as pltpu

F32 = jnp.float32
BF16 = jnp.bfloat16
I32 = jnp.int32

D_MODEL = 1024
D_INNER = 2048
HEADS = 32
HEADDIM = 64
GROUPS = 4
STATE = 128
D_BC = GROUPS * STATE
D_XBC = D_INNER + 2 * D_BC
CONV_K = 5
CHUNK = 128
GRID_W = 64
POOL_WINDOWS = (2, 4, 8, 16)
POOL_GROUP_DIM = 256
MOE_GROUPS = 4
MOE_EPG = 8
MOE_EXPERTS = 32
D_EXPERT = 512
MOE_BLOCK = 256
DEPTH_ALPHA_POW = 0.25
NORM_EPS = 1e-5
LANES = 128
ROUTE_LANE0 = MOE_GROUPS
VMEM_LIMIT = 56 * 1024 * 1024


def _cparams(sem, vmem=VMEM_LIMIT):
    return pltpu.CompilerParams(dimension_semantics=sem, vmem_limit_bytes=vmem)


def _silu(v):
    return v * jax.nn.sigmoid(v)


def _layer_norm(v, g, b):
    mu = jnp.mean(v, axis=-1, keepdims=True)
    d = v - mu
    var = jnp.mean(d * d, axis=-1, keepdims=True)
    return d * lax.rsqrt(var + NORM_EPS) * g + b


def _split_bf16(v):
    hi = v.astype(BF16)
    lo = (v - hi.astype(F32)).astype(BF16)
    return hi, lo


def _dot(a, b):
    return jnp.dot(a, b, preferred_element_type=F32)


U32 = jnp.uint32
HI16 = 0xFFFF0000


def _pack_rows(v):
    half = v.shape[1] // 2
    lo = lax.bitcast_convert_type(v[:, :half].astype(BF16).astype(F32), U32)
    hi = lax.bitcast_convert_type(v[:, half:].astype(BF16).astype(F32), U32)
    return (lo >> 16) | (hi & U32(HI16))


def _unpack_rows(w):
    lo = lax.bitcast_convert_type(w << 16, F32)
    hi = lax.bitcast_convert_type(w & U32(HI16), F32)
    return jnp.concatenate([lo, hi], axis=1)


def _ada_kernel(c_ref, w_ref, b_ref, o_ref):
    s = _silu(c_ref[...]).astype(BF16)
    o_ref[0] = _dot(s, w_ref[0].astype(BF16)) + b_ref[0]


def _ada_mods(cvec, w_ada, b_ada):
    depth, d, n = w_ada.shape
    r = cvec.shape[0]
    tn = D_MODEL
    return pl.pallas_call(
        _ada_kernel,
        grid=(depth, n // tn),
        in_specs=[pl.BlockSpec((r, d), lambda l, j: (0, 0)),
                  pl.BlockSpec((1, d, tn), lambda l, j: (l, 0, j)),
                  pl.BlockSpec((1, 1, tn), lambda l, j: (l, 0, j))],
        out_specs=pl.BlockSpec((1, r, tn), lambda l, j: (l, 0, j)),
        out_shape=jax.ShapeDtypeStruct((depth, r, n), F32),
        compiler_params=_cparams(("arbitrary", "arbitrary")),
    )(cvec, w_ada, b_ada.reshape(depth, 1, n))


def _inproj_kernel(x_ref, sh_ref, sc_ref, wz_ref, wx_ref, wu_ref, wg_ref, wdt_ref,
                   z_ref, xbc_ref, u_ref, g_ref, dt_ref):
    h = (x_ref[0] * (1.0 + sc_ref[0]) + sh_ref[0]).astype(BF16)
    z_ref[0] = _dot(h, wz_ref[...]).astype(BF16)
    xbc_ref[0] = _dot(h, wx_ref[...]).astype(BF16)
    u_ref[0] = _dot(h, wu_ref[...]).astype(BF16)
    g_ref[0] = _dot(h, wg_ref[...]).astype(BF16)
    dt_ref[0] = _dot(h, wdt_ref[...])


def _in_proj(x, shift, scale, wts):
    b, l, d = x.shape
    tm = min(256, l)
    wz, wx, wu, wg, wdt = wts
    tok = lambda n: pl.BlockSpec((1, tm, n), lambda i, j: (i, j, 0))
    mod = pl.BlockSpec((1, 1, d), lambda i, j: (i, 0, 0))
    wspec = lambda w: pl.BlockSpec(w.shape, lambda i, j: (0, 0), pipeline_mode=pl.Buffered(1))
    return pl.pallas_call(
        _inproj_kernel,
        grid=(b, l // tm),
        in_specs=[tok(d), mod, mod, wspec(wz), wspec(wx), wspec(wu), wspec(wg), wspec(wdt)],
        out_specs=[tok(D_INNER), tok(D_XBC), tok(D_MODEL), tok(2 * D_MODEL), tok(LANES)],
        out_shape=[jax.ShapeDtypeStruct((b, l, D_INNER), BF16),
                   jax.ShapeDtypeStruct((b, l, D_XBC), BF16),
                   jax.ShapeDtypeStruct((b, l, D_MODEL), BF16),
                   jax.ShapeDtypeStruct((b, l, 2 * D_MODEL), BF16),
                   jax.ShapeDtypeStruct((b, l, LANES), F32)],
        compiler_params=_cparams(("arbitrary", "arbitrary")),
    )(x, shift, scale, wz, wx, wu, wg, wdt)


HALO = 16


def _conv_act(prev_ref, cur_ref, next_ref, ext_ref, cw_ref, cb_ref, c, nc, ncols):
    q = CHUNK
    prev8 = prev_ref[0].astype(F32)[HALO - 8:HALO, :ncols]
    next8 = next_ref[0].astype(F32)[0:8, :ncols]
    ext_ref[0:8, :ncols] = jnp.where(c > 0, prev8, 0.0)
    ext_ref[8:8 + q, :ncols] = cur_ref[0].astype(F32)[:, :ncols]
    ext_ref[8 + q:16 + q, :ncols] = jnp.where(c < nc - 1, next8, 0.0)
    acc = jnp.broadcast_to(cb_ref[:, :ncols], (q, ncols))
    for k in range(CONV_K):
        off = 8 - CONV_K // 2 + k
        acc = acc + ext_ref[off:off + q, :ncols] * cw_ref[k:k + 1, :ncols]
    return _silu(acc)


def _dt_prep(dt_ref, dtb_ref, alog_ref):
    q = CHUNK
    lane = lax.broadcasted_iota(I32, (q, LANES), 1)
    raw = dt_ref[0] + dtb_ref[...]
    dt = jnp.maximum(raw, 0.0) + jnp.log1p(jnp.exp(-jnp.abs(raw)))
    dt = jnp.where(lane < 2 * HEADS, dt, 0.0)
    a = dt * (-jnp.exp(alog_ref[...]))
    ii = lax.broadcasted_iota(I32, (q, q), 0)
    jj = lax.broadcasted_iota(I32, (q, q), 1)
    tri = (jj <= ii).astype(BF16)
    a_hi, a_lo = _split_bf16(a)
    acs = _dot(tri, a_hi) + _dot(tri, a_lo)
    tot = acs[q - 1:q, :]
    return dt, a, acs, tot


def _expand(v, lane0, r_ref):
    lane = lax.broadcasted_iota(I32, v.shape, 1)
    vm = jnp.where((lane >= lane0) & (lane < lane0 + HEADS), v, 0.0)
    hi, lo = _split_bf16(vm)
    return _dot(hi, r_ref[...]) + _dot(lo, r_ref[...])


def _state_update(st_ref, bm, xw_bf, decay_x):
    gw = D_INNER // GROUPS
    for g in range(GROUPS):
        bgt = bm[:, g * STATE:(g + 1) * STATE].T.astype(BF16)
        cols = slice(g * gw, (g + 1) * gw)
        st_ref[:, cols] = st_ref[:, cols] * decay_x[:, cols] + _dot(bgt, xw_bf[:, cols])


def _ssd_state_kernel(*refs, rev, emit):
    (prev_ref, cur_ref, next_ref, dt_ref, h0_ref, cw_ref, cb_ref, dtb_ref, alog_ref, r_ref) = refs[:10]
    if emit:
        stout_ref, act_ref, sfin_ref, ext_ref, st_ref = refs[10:]
    else:
        sfin_ref, ext_ref, st_ref = refs[10:]
    s = pl.program_id(1)
    nc = pl.num_programs(1)
    c = (nc - 1 - s) if rev else s
    ncols = D_XBC if emit else D_INNER + D_BC

    @pl.when(s == 0)
    def _():
        st_ref[...] = h0_ref[0]

    if emit:
        stout_ref[0, 0] = st_ref[...].astype(BF16)

    act = _conv_act(prev_ref, cur_ref, next_ref, ext_ref, cw_ref, cb_ref, c, nc, ncols)
    if emit:
        act_ref[0] = act.astype(BF16)
    xs = act[:, :D_INNER]
    bm = act[:, D_INNER:D_INNER + D_BC]
    dt, a, acs, tot = _dt_prep(dt_ref, dtb_ref, alog_ref)
    if rev:
        w = jnp.exp(acs - a) * dt
        lane0 = HEADS
    else:
        w = jnp.exp(tot - acs) * dt
        lane0 = 0
    w_x = _expand(w, lane0, r_ref)
    dec_x = _expand(jnp.broadcast_to(jnp.exp(tot), (16, LANES)), lane0, r_ref)[0:1]
    _state_update(st_ref, bm, (w_x * xs).astype(BF16), dec_x)

    @pl.when(s == nc - 1)
    def _():
        sfin_ref[0] = st_ref[...]


def _ssd_state_scan(xbc, dt, h0, lp, r_mat, *, rev, emit):
    b, l, _ = xbc.shape
    nc = l // CHUNK
    nh = l // HALO
    hpc = CHUNK // HALO
    seq = (lambda s: nc - 1 - s) if rev else (lambda s: s)
    par = lambda shape: pl.BlockSpec(shape, lambda i, s: (0,) * len(shape))
    in_specs = [
        pl.BlockSpec((1, HALO, D_XBC), lambda i, s: (i, jnp.maximum(seq(s) * hpc - 1, 0), 0)),
        pl.BlockSpec((1, CHUNK, D_XBC), lambda i, s: (i, seq(s), 0)),
        pl.BlockSpec((1, HALO, D_XBC), lambda i, s: (i, jnp.minimum((seq(s) + 1) * hpc, nh - 1), 0)),
        pl.BlockSpec((1, CHUNK, LANES), lambda i, s: (i, seq(s), 0)),
        pl.BlockSpec((1, STATE, D_INNER), lambda i, s: (i, 0, 0)),
        par((CONV_K, D_XBC)), par((1, D_XBC)), par((1, LANES)), par((1, LANES)), par((LANES, D_INNER)),
    ]
    out_specs = [pl.BlockSpec((1, STATE, D_INNER), lambda i, s: (i, 0, 0))]
    out_shape = [jax.ShapeDtypeStruct((b, STATE, D_INNER), F32)]
    if emit:
        out_specs = [pl.BlockSpec((1, 1, STATE, D_INNER), lambda i, s: (i, seq(s), 0, 0)),
                     pl.BlockSpec((1, CHUNK, D_XBC), lambda i, s: (i, seq(s), 0))] + out_specs
        out_shape = [jax.ShapeDtypeStruct((b, nc, STATE, D_INNER), BF16),
                     jax.ShapeDtypeStruct((b, l, D_XBC), BF16)] + out_shape
    return pl.pallas_call(
        functools.partial(_ssd_state_kernel, rev=rev, emit=emit),
        grid=(b, nc),
        in_specs=in_specs, out_specs=out_specs, out_shape=out_shape,
        scratch_shapes=[pltpu.VMEM((CHUNK + 16, D_XBC), F32), pltpu.VMEM((STATE, D_INNER), F32)],
        compiler_params=_cparams(("arbitrary", "arbitrary")),
    )(xbc, xbc, xbc, dt, h0, lp["conv_w"], lp["conv_b"], lp["dt_bias"], lp["a_log"], r_mat)


def _ssd_main_kernel(act_ref, dt_ref, z_ref, stb_ref, h0_ref,
                     dtb_ref, alog_ref, dsk_ref, ng_ref, rf_ref, rb_ref,
                     y_ref, sfin_ref, st_ref, ys_ref):
    q = CHUNK
    c = pl.program_id(1)
    nc = pl.num_programs(1)

    @pl.when(c == 0)
    def _():
        st_ref[...] = h0_ref[0]

    xs_bf = act_ref[0, :, :D_INNER]
    xs = xs_bf.astype(F32)
    dt, a, acs, tot = _dt_prep(dt_ref, dtb_ref, alog_ref)
    eb = acs - a

    ef_x = _expand(jnp.exp(acs), 0, rf_ref)
    eb_x = _expand(jnp.exp(tot - eb), HEADS, rb_ref)
    tf_x = _expand(jnp.exp(tot - acs) * dt, 0, rf_ref)
    dec_x = _expand(jnp.broadcast_to(jnp.exp(tot), (16, LANES)), 0, rf_ref)[0:1]

    lane = lax.broadcasted_iota(I32, (q, LANES), 1)
    ldt = jnp.log(dt)
    qm = jnp.where(lane < HEADS, acs - ldt, jnp.where(lane < 2 * HEADS, eb + ldt, pltpu.roll(dt, 2 * HEADS, 1)))
    qt = qm.T
    ii = lax.broadcasted_iota(I32, (q, q), 0)
    jj = lax.broadcasted_iota(I32, (q, q), 1)
    lower = jj <= ii
    diag = jj == ii
    lane2 = lax.broadcasted_iota(I32, (q, LANES), 1)
    left = lane2 < HEADDIM

    stf_bf = st_ref[...].astype(BF16)
    gw = D_INNER // GROUPS
    hpg = HEADS // GROUPS
    for g in range(GROUPS):
        bg = act_ref[0, :, D_INNER + g * STATE:D_INNER + (g + 1) * STATE]
        cg = act_ref[0, :, D_INNER + D_BC + g * STATE:D_INNER + D_BC + (g + 1) * STATE]
        cb = lax.dot_general(cg, bg, (((1,), (1,)), ((), ())), preferred_element_type=F32)
        cols = slice(g * gw, (g + 1) * gw)
        y_off = ef_x[:, cols] * _dot(cg, stf_bf[:, cols]) + eb_x[:, cols] * _dot(cg, stb_ref[0, 0][:, cols])
        for pr in range(hpg // 2):
            ms = []
            for hh in range(2):
                h = g * hpg + pr * 2 + hh
                afc = acs[:, h:h + 1]
                afr = qt[h:h + 1, :]
                ebc = eb[:, HEADS + h:HEADS + h + 1]
                ebr = qt[HEADS + h:HEADS + h + 1, :]
                wb = qt[3 * HEADS + h:3 * HEADS + h + 1, :]
                arg = jnp.where(lower, afc - afr, ebr - ebc)
                ms.append((cb * (jnp.exp(arg) + jnp.where(diag, wb, 0.0))).astype(BF16))
            lhs = jnp.concatenate(ms, axis=1)
            c0 = g * gw + pr * 2 * HEADDIM
            xp = xs_bf[:, c0:c0 + 2 * HEADDIM]
            zero = jnp.zeros_like(xp)
            rhs = jnp.concatenate([jnp.where(left, xp, zero), jnp.where(left, zero, xp)], axis=0)
            y_pair = _dot(lhs, rhs) + y_off[:, pr * 2 * HEADDIM:(pr + 1) * 2 * HEADDIM]
            y_pair = y_pair + dsk_ref[:, c0:c0 + 2 * HEADDIM] * xs[:, c0:c0 + 2 * HEADDIM]
            ys_ref[:, c0:c0 + 2 * HEADDIM] = y_pair

    y = ys_ref[...]
    yz = y * _silu(z_ref[0].astype(F32))
    ms2 = jnp.mean(yz * yz, axis=-1, keepdims=True)
    y_ref[0] = (yz * lax.rsqrt(ms2 + NORM_EPS) * ng_ref[...]).astype(BF16)

    bm = act_ref[0, :, D_INNER:D_INNER + D_BC].astype(F32)
    _state_update(st_ref, bm, (tf_x * xs).astype(BF16), dec_x)

    @pl.when(c == nc - 1)
    def _():
        sfin_ref[0] = st_ref[...]


def _ssd_main(act, dt, z, stb, h0, lp, r_f, r_b):
    b, l, _ = act.shape
    nc = l // CHUNK
    par = lambda shape: pl.BlockSpec(shape, lambda i, s: (0,) * len(shape))
    in_specs = [
        pl.BlockSpec((1, CHUNK, D_XBC), lambda i, s: (i, s, 0)),
        pl.BlockSpec((1, CHUNK, LANES), lambda i, s: (i, s, 0)),
        pl.BlockSpec((1, CHUNK, D_INNER), lambda i, s: (i, s, 0)),
        pl.BlockSpec((1, 1, STATE, D_INNER), lambda i, s: (i, s, 0, 0)),
        pl.BlockSpec((1, STATE, D_INNER), lambda i, s: (i, 0, 0)),
        par((1, LANES)), par((1, LANES)),
        par((1, D_INNER)), par((1, D_INNER)), par((LANES, D_INNER)), par((LANES, D_INNER)),
    ]
    return pl.pallas_call(
        _ssd_main_kernel,
        grid=(b, nc),
        in_specs=in_specs,
        out_specs=[pl.BlockSpec((1, CHUNK, D_INNER), lambda i, s: (i, s, 0)),
                   pl.BlockSpec((1, STATE, D_INNER), lambda i, s: (i, 0, 0))],
        out_shape=[jax.ShapeDtypeStruct((b, l, D_INNER), BF16),
                   jax.ShapeDtypeStruct((b, STATE, D_INNER), F32)],
        scratch_shapes=[pltpu.VMEM((STATE, D_INNER), F32), pltpu.VMEM((CHUNK, D_INNER), F32)],
        compiler_params=_cparams(("arbitrary", "arbitrary")),
    )(act, dt, z, stb, h0, lp["dt_bias"], lp["a_log"], lp["d_skip_x"], lp["ssd_norm_g"], r_f, r_b)


def _pool_kernel(*refs, width, vertical, halo, hblock):
    if vertical:
        prev_ref, cur_ref, next_ref, pw_ref, ps_ref, o_ref = refs
    else:
        cur_ref, pw_ref, ps_ref, o_ref = refs
    t = pl.program_id(1)
    nt = pl.num_programs(1)
    tp = cur_ref.shape[1]
    n_tok = nt * tp
    shift = int(math.log2(width))
    idx = lax.broadcasted_iota(I32, (tp, 1), 0) + t * tp
    colpos = idx & (width - 1)
    rowpos = idx >> shift
    n_rows = n_tok // width
    bi = lax.broadcasted_iota(I32, (hblock, hblock), 0)
    bj = lax.broadcasted_iota(I32, (hblock, hblock), 1)
    same_row = (bi >> shift) == (bj >> shift)
    cur = cur_ref[0].astype(F32)
    if vertical:
        prev = jnp.where(t > 0, prev_ref[0].astype(F32), 0.0)
        nxt = jnp.where(t < nt - 1, next_ref[0].astype(F32), 0.0)
    for gi, k in enumerate(POOL_WINDOWS):
        cols = slice(gi * POOL_GROUP_DIM, (gi + 1) * POOL_GROUP_DIM)
        ug = cur[:, cols]
        lo = k // 2
        if vertical:
            e = jnp.concatenate([prev[:, cols], ug, nxt[:, cols]], axis=0)
            step = width
            m = 1
            while m < k:
                n = e.shape[0] - step
                e = e[0:n] + e[step:step + n]
                step *= 2
                m *= 2
            start = halo - lo * width
            s = e[start:start + tp]
            cnt_r = (jnp.minimum(rowpos - lo + k, n_rows) - jnp.maximum(rowpos - lo, 0)).astype(F32)
        else:
            s = ug
            cnt_r = jnp.ones((tp, 1), F32)
        band = (same_row & (bj - bi >= -lo) & (bj - bi < k - lo)).astype(BF16)
        s_bf = s.astype(BF16)
        hs = [_dot(band, s_bf[r * hblock:(r + 1) * hblock]) for r in range(tp // hblock)]
        hsum = hs[0] if len(hs) == 1 else jnp.concatenate(hs, axis=0)
        cnt_c = (jnp.minimum(colpos - lo + k, width) - jnp.maximum(colpos - lo, 0)).astype(F32)
        mean = hsum / (cnt_r * cnt_c)
        y = _dot((mean - ug).astype(BF16), pw_ref[gi])
        o_ref[0, :, cols] = (y * ps_ref[:, cols]).astype(BF16)


def _pool_branch(u, lp, on_grid):
    b, l, d = u.shape
    pw, ps = lp["pool_w"], lp["pool_scale"]
    if on_grid:
        width, halo = GRID_W, (max(POOL_WINDOWS) // 2) * GRID_W
        tp = min(1024, l)
        hb = tp // halo
        nhb = l // halo
        in_specs = [pl.BlockSpec((1, halo, d), lambda i, t: (i, jnp.maximum(t * hb - 1, 0), 0)),
                    pl.BlockSpec((1, tp, d), lambda i, t: (i, t, 0)),
                    pl.BlockSpec((1, halo, d), lambda i, t: (i, jnp.minimum((t + 1) * hb, nhb - 1), 0))]
        args = (u, u, u)
        kern = functools.partial(_pool_kernel, width=width, vertical=True, halo=halo, hblock=LANES)
    else:
        tp = l
        in_specs = [pl.BlockSpec((1, tp, d), lambda i, t: (i, t, 0))]
        args = (u,)
        kern = functools.partial(_pool_kernel, width=l, vertical=False, halo=0, hblock=l)
    in_specs += [pl.BlockSpec(pw.shape, lambda i, t: (0, 0, 0)), pl.BlockSpec(ps.shape, lambda i, t: (0, 0))]
    return pl.pallas_call(
        kern,
        grid=(b, l // tp),
        in_specs=in_specs,
        out_specs=pl.BlockSpec((1, tp, d), lambda i, t: (i, t, 0)),
        out_shape=jax.ShapeDtypeStruct((b, l, d), BF16),
        compiler_params=_cparams(("arbitrary", "arbitrary")),
    )(*args, pw, ps)


def _merge_kernel(ya_ref, yp_ref, gt_ref, x_ref, g1_ref, sh2_ref, sc2_ref, bg_ref,
                  wa_ref, wb_ref, wo_ref, lng_ref, lnb_ref, wrh_ref, wrl_ref, br_ref, cin_ref,
                  xo_ref, h2_ref, rt_ref, cout_ref, cnt_ref, *, alpha):
    i = pl.program_id(0)
    j = pl.program_id(1)
    tm = x_ref.shape[1]

    @pl.when((i == 0) & (j == 0))
    def _():
        cnt_ref[...] = cin_ref[...]

    gate = jax.nn.sigmoid(gt_ref[0].astype(F32) + bg_ref[...])
    merged = (gate[:, :D_MODEL] * _dot(ya_ref[0], wa_ref[...])
              + gate[:, D_MODEL:] * _dot(yp_ref[0], wb_ref[...]))
    out = _dot(merged.astype(BF16), wo_ref[...])
    xn = _layer_norm(alpha * x_ref[0] + g1_ref[0] * out, lng_ref[...], lnb_ref[...])
    xo_ref[0] = xn
    h2 = xn * (1.0 + sc2_ref[0]) + sh2_ref[0]
    h2_ref[0] = _pack_rows(h2)

    h_hi, h_lo = _split_bf16(h2)
    logits = _dot(h_hi, wrh_ref[...]) + _dot(h_lo, wrh_ref[...]) + _dot(h_hi, wrl_ref[...]) + br_ref[...]
    lane = lax.broadcasted_iota(I32, (tm, LANES), 1)
    neg = jnp.float32(-jnp.inf)
    lg = jnp.where(lane < MOE_GROUPS, logits, neg)
    mg = jnp.max(lg, axis=-1, keepdims=True)
    grp = jnp.min(jnp.where(lg == mg, lane, LANES), axis=-1, keepdims=True)
    p_grp = 1.0 / jnp.sum(jnp.exp(lg - mg), axis=-1, keepdims=True)
    lo_lane = ROUTE_LANE0 + grp * MOE_EPG
    le = jnp.where((lane >= lo_lane) & (lane < lo_lane + MOE_EPG), logits, neg)
    v1 = jnp.max(le, axis=-1, keepdims=True)
    i1 = jnp.min(jnp.where(le == v1, lane, LANES), axis=-1, keepdims=True)
    le2 = jnp.where(lane == i1, neg, le)
    v2 = jnp.max(le2, axis=-1, keepdims=True)
    i2 = jnp.min(jnp.where(le2 == v2, lane, LANES), axis=-1, keepdims=True)
    e2 = jnp.exp(v2 - v1)
    w1 = p_grp / (1.0 + e2)
    w2 = p_grp * e2 / (1.0 + e2)

    oh1 = (lane == i1).astype(F32)
    oh2 = (lane == i2).astype(F32)
    oh = oh1 + oh2
    ri = lax.broadcasted_iota(I32, (tm, tm), 0)
    rj = lax.broadcasted_iota(I32, (tm, tm), 1)
    before = _dot((rj < ri).astype(BF16), oh.astype(BF16)) + cnt_ref[...]
    rank1 = jnp.sum(oh1 * before, axis=-1, keepdims=True)
    rank2 = jnp.sum(oh2 * before, axis=-1, keepdims=True)
    cnt_ref[...] = cnt_ref[...] + jnp.sum(oh, axis=0, keepdims=True)
    cout_ref[...] = cnt_ref[...]

    slab = jnp.where(lane == 0, (i1 - ROUTE_LANE0).astype(F32),
           jnp.where(lane == 1, (i2 - ROUTE_LANE0).astype(F32),
           jnp.where(lane == 2, w1,
           jnp.where(lane == 3, w2,
           jnp.where(lane == 4, rank1,
           jnp.where(lane == 5, rank2, 0.0))))))
    rt_ref[0] = slab


def _merge(ya, yp, gates, x, g1, sh2, sc2, lp, cnt_in, alpha):
    b, l, d = x.shape
    tm = min(512, l)
    tok = lambda n: pl.BlockSpec((1, tm, n), lambda i, j: (i, j, 0))
    mod = pl.BlockSpec((1, 1, d), lambda i, j: (i, 0, 0))
    par = lambda a: pl.BlockSpec(a.shape, lambda i, j: (0,) * a.ndim)
    params = (lp["b_gate"], lp["w_branch_a"], lp["w_branch_b"], lp["w_out"], lp["ln1_g"], lp["ln1_b"],
              lp["w_r_hi"], lp["w_r_lo"], lp["b_r"], cnt_in)
    return pl.pallas_call(
        functools.partial(_merge_kernel, alpha=alpha),
        grid=(b, l // tm),
        in_specs=[tok(D_INNER), tok(d), tok(2 * d), tok(d), mod, mod, mod] + [par(a) for a in params],
        out_specs=[tok(d), tok(d // 2), tok(LANES), pl.BlockSpec((1, LANES), lambda i, j: (0, 0))],
        out_shape=[jax.ShapeDtypeStruct((b, l, d), F32), jax.ShapeDtypeStruct((b, l, d // 2), U32),
                   jax.ShapeDtypeStruct((b, l, LANES), F32), jax.ShapeDtypeStruct((1, LANES), F32)],
        scratch_shapes=[pltpu.VMEM((1, LANES), F32)],
        compiler_params=_cparams(("arbitrary", "arbitrary")),
    )(ya, yp, gates, x, g1, sh2, sc2, *params)


ROW_TILE = 512
DMA_UNROLL = 8


def _row_copy(src_ref, src_row, dst_ref, dst_row, sem):
    return pltpu.make_async_copy(src_ref.at[pl.ds(src_row, 1)], dst_ref.at[pl.ds(dst_row, 1)], sem)


def _dispatch_kernel(h_ref, dest_ref, xin_any, xout_any, idx_smem, sem_idx, sem_row):
    del xin_any
    td = h_ref.shape[0]
    cp = pltpu.make_async_copy(dest_ref.at[0, 0], idx_smem, sem_idx)
    cp.start()
    cp.wait()

    def issue(r, carry):
        _row_copy(h_ref, r, xout_any, idx_smem[2 * r], sem_row).start()
        _row_copy(h_ref, r, xout_any, idx_smem[2 * r + 1], sem_row).start()
        return carry

    lax.fori_loop(0, td, issue, 0, unroll=DMA_UNROLL)

    def drain(r, carry):
        _row_copy(h_ref, 0, xout_any, 0, sem_row).wait()
        _row_copy(h_ref, 0, xout_any, 0, sem_row).wait()
        return carry

    lax.fori_loop(0, td, drain, 0, unroll=DMA_UNROLL)


def _dispatch(h2, dest, xin):
    t, d = h2.shape
    td = min(ROW_TILE, t)
    dest3 = dest.reshape(t // td, 1, 2 * td)
    return pl.pallas_call(
        _dispatch_kernel,
        grid=(t // td,),
        in_specs=[pl.BlockSpec((td, d), lambda i: (i, 0)),
                  pl.BlockSpec((1, 1, 2 * td), lambda i: (i, 0, 0)),
                  pl.BlockSpec(memory_space=pl.ANY)],
        out_specs=pl.BlockSpec(memory_space=pl.ANY),
        out_shape=jax.ShapeDtypeStruct(xin.shape, xin.dtype),
        scratch_shapes=[pltpu.SMEM((2 * td,), I32), pltpu.SemaphoreType.DMA(()), pltpu.SemaphoreType.DMA(())],
        input_output_aliases={2: 0},
        compiler_params=_cparams(("arbitrary",)),
    )(h2, dest3, xin)


def _expert_kernel(be_ref, nu_ref, x_ref, wg_ref, wu_ref, wd_ref, o_ref):
    i = pl.program_id(0)

    @pl.when(i < nu_ref[0])
    def _():
        xb = _unpack_rows(x_ref[...]).astype(BF16)
        gte = _dot(xb, wg_ref[0])
        up = _dot(xb, wu_ref[0])
        o_ref[...] = _pack_rows(_dot((_silu(gte) * up).astype(BF16), wd_ref[0]))

    @pl.when(i >= nu_ref[0])
    def _():
        o_ref[...] = jnp.zeros_like(o_ref)


def _experts(xin, block_e, n_used, lp):
    n_rows, dh = xin.shape
    d = 2 * dh
    nb = n_rows // MOE_BLOCK
    grid_spec = pltpu.PrefetchScalarGridSpec(
        num_scalar_prefetch=2,
        grid=(nb,),
        in_specs=[pl.BlockSpec((MOE_BLOCK, dh), lambda i, be, nu: (i, 0)),
                  pl.BlockSpec((1, d, D_EXPERT), lambda i, be, nu: (be[i], 0, 0)),
                  pl.BlockSpec((1, d, D_EXPERT), lambda i, be, nu: (be[i], 0, 0)),
                  pl.BlockSpec((1, D_EXPERT, d), lambda i, be, nu: (be[i], 0, 0))],
        out_specs=pl.BlockSpec((MOE_BLOCK, dh), lambda i, be, nu: (i, 0)),
    )
    return pl.pallas_call(
        _expert_kernel,
        grid_spec=grid_spec,
        out_shape=jax.ShapeDtypeStruct((n_rows, dh), U32),
        compiler_params=_cparams(("arbitrary",)),
    )(block_e, n_used, xin, lp["w_eg"], lp["w_eu"], lp["w_ed"])


def _combine_kernel(dest_ref, rt_ref, x_ref, g2_ref, lng_ref, lnb_ref, yb_any,
                    o_ref, buf_ref, idx_smem, sem_idx, sem_row, *, alpha):
    tc = x_ref.shape[1]
    cp = pltpu.make_async_copy(dest_ref.at[0, 0], idx_smem, sem_idx)
    cp.start()
    cp.wait()

    def issue(r, carry):
        _row_copy(yb_any, idx_smem[2 * r], buf_ref.at[0], r, sem_row).start()
        _row_copy(yb_any, idx_smem[2 * r + 1], buf_ref.at[1], r, sem_row).start()
        return carry

    lax.fori_loop(0, tc, issue, 0, unroll=DMA_UNROLL)

    def drain(r, carry):
        _row_copy(yb_any, 0, buf_ref.at[0], 0, sem_row).wait()
        _row_copy(yb_any, 0, buf_ref.at[1], 0, sem_row).wait()
        return carry

    lax.fori_loop(0, tc, drain, 0, unroll=DMA_UNROLL)

    rt = rt_ref[0]
    y = rt[:, 2:3] * _unpack_rows(buf_ref[0]) + rt[:, 3:4] * _unpack_rows(buf_ref[1])
    o_ref[0] = _layer_norm(alpha * x_ref[0] + g2_ref[0] * y, lng_ref[...], lnb_ref[...])


def _combine(dest, route, x, g2, lp, yb, alpha):
    b, l, d = x.shape
    tc = min(ROW_TILE, l)
    npb = l // tc
    dest3 = dest.reshape(b * npb, 1, 2 * tc)
    par = lambda a: pl.BlockSpec(a.shape, lambda i, j: (0,) * a.ndim)
    return pl.pallas_call(
        functools.partial(_combine_kernel, alpha=alpha),
        grid=(b, npb),
        in_specs=[pl.BlockSpec((1, 1, 2 * tc), lambda i, j: (i * npb + j, 0, 0)),
                  pl.BlockSpec((1, tc, LANES), lambda i, j: (i, j, 0)),
                  pl.BlockSpec((1, tc, d), lambda i, j: (i, j, 0)),
                  pl.BlockSpec((1, 1, d), lambda i, j: (i, 0, 0)),
                  par(lp["ln2_g"]), par(lp["ln2_b"]),
                  pl.BlockSpec(memory_space=pl.ANY)],
        out_specs=pl.BlockSpec((1, tc, d), lambda i, j: (i, j, 0)),
        out_shape=jax.ShapeDtypeStruct((b, l, d), F32),
        scratch_shapes=[pltpu.VMEM((2, tc, d // 2), U32), pltpu.SMEM((2 * tc,), I32),
                        pltpu.SemaphoreType.DMA(()), pltpu.SemaphoreType.DMA(())],
        compiler_params=_cparams(("arbitrary", "arbitrary")),
    )(dest3, route, x, g2, lp["ln2_g"], lp["ln2_b"], yb)


def _moe_plan(routes, counts):
    cnt = counts[0, ROUTE_LANE0:ROUTE_LANE0 + MOE_EXPERTS].astype(I32)
    padded = (cnt + MOE_BLOCK - 1) // MOE_BLOCK * MOE_BLOCK
    pad_end = jnp.cumsum(padded)
    pad_start = pad_end - padded
    n_assign = sum(r.shape[0] * r.shape[1] for r in routes) * 2
    n_blocks = -(-n_assign // MOE_BLOCK) + MOE_EXPERTS
    first_row = jnp.arange(n_blocks, dtype=I32)[:, None] * MOE_BLOCK
    block_e = jnp.minimum(jnp.sum((pad_end[None, :] <= first_row).astype(I32), axis=1), MOE_EXPERTS - 1)
    n_used = (pad_end[-1:] // MOE_BLOCK).astype(I32)
    eidx = jnp.arange(MOE_EXPERTS, dtype=I32)
    dests = []
    for r in routes:
        e = r[..., 0:2].astype(I32)
        rank = r[..., 4:6].astype(I32)
        start = jnp.sum(jnp.where(e[..., None] == eidx, pad_start, 0), axis=-1)
        dests.append((start + rank).reshape(-1, 2))
    return dests, block_e, n_used, n_blocks * MOE_BLOCK


def _layer_params(i, p):
    w_in = p["w_in"][i]
    s0, s1, s2, s3 = D_INNER, D_INNER + D_XBC, D_INNER + D_XBC + 2 * HEADS, D_INNER + D_XBC + 2 * HEADS + D_MODEL
    pad = LANES - 2 * HEADS
    wdt = jnp.pad(w_in[:, s1:s2], ((0, 0), (0, pad)))
    w_r = jnp.concatenate([p["w_router_group"][i],
                           jnp.transpose(p["w_router_expert"][i], (1, 0, 2)).reshape(D_MODEL, MOE_EXPERTS)], axis=1)
    w_r = jnp.pad(w_r, ((0, 0), (0, LANES - w_r.shape[1])))
    w_r_hi = w_r.astype(BF16)
    b_r = jnp.concatenate([p["b_router_group"][i], p["b_router_expert"][i].reshape(-1)])
    return {
        "w_in": tuple(w.astype(BF16) for w in (w_in[:, :s0], w_in[:, s0:s1], w_in[:, s2:s3], w_in[:, s3:], wdt)),
        "conv_w": p["conv_w"][i], "conv_b": p["conv_b"][i][None],
        "dt_bias": jnp.pad(p["dt_bias"][i].reshape(1, -1), ((0, 0), (0, pad))),
        "a_log": jnp.pad(p["a_log"][i].reshape(1, -1), ((0, 0), (0, pad))),
        "d_skip_x": jnp.repeat(p["d_skip"][i], HEADDIM)[None],
        "ssd_norm_g": p["ssd_norm_g"][i][None],
        "pool_w": p["pool_w"][i].astype(BF16), "pool_scale": p["pool_scale"][i][None],
        "b_gate": p["b_gate"][i][None],
        "w_branch_a": p["w_branch_a"][i].astype(BF16), "w_branch_b": p["w_branch_b"][i].astype(BF16),
        "w_out": p["w_out"][i].astype(BF16),
        "ln1_g": p["ln1_g"][i][None], "ln1_b": p["ln1_b"][i][None],
        "ln2_g": p["ln2_g"][i][None], "ln2_b": p["ln2_b"][i][None],
        "w_r_hi": w_r_hi, "w_r_lo": (w_r - w_r_hi.astype(F32)).astype(BF16),
        "b_r": jnp.pad(b_r, (0, LANES - b_r.shape[0]))[None],
        "w_eg": p["w_expert_gate"][i].astype(BF16), "w_eu": p["w_expert_up"][i].astype(BF16),
        "w_ed": p["w_expert_down"][i].astype(BF16),
    }


def _head_expanders():
    col_head = jnp.arange(D_INNER, dtype=I32) // HEADDIM
    row = jnp.arange(LANES, dtype=I32)[:, None]
    return (row == col_head).astype(BF16), (row == col_head + HEADS).astype(BF16)


def kernel(x, c, ctx, c_ctx, w_ada, b_ada, w_in, b_gate, conv_w, conv_b, dt_bias, a_log, d_skip, ssd_norm_g, pool_w, pool_scale, w_branch_a, w_branch_b, w_out, ln1_g, ln1_b, ln2_g, ln2_b, w_router_group, b_router_group, w_router_expert, b_router_expert, w_expert_gate, w_expert_up, w_expert_down):
    p = dict(w_in=w_in, b_gate=b_gate, conv_w=conv_w, conv_b=conv_b, dt_bias=dt_bias, a_log=a_log, d_skip=d_skip,
             ssd_norm_g=ssd_norm_g, pool_w=pool_w, pool_scale=pool_scale, w_branch_a=w_branch_a,
             w_branch_b=w_branch_b, w_out=w_out, ln1_g=ln1_g, ln1_b=ln1_b, ln2_g=ln2_g, ln2_b=ln2_b,
             w_router_group=w_router_group, b_router_group=b_router_group, w_router_expert=w_router_expert,
             b_router_expert=b_router_expert, w_expert_gate=w_expert_gate, w_expert_up=w_expert_up,
             w_expert_down=w_expert_down)
    b, l, d = x.shape
    depth = w_ada.shape[0]
    alpha = (2.0 * depth) ** DEPTH_ALPHA_POW
    n_mod = -(-(b + 1) // 8) * 8
    cvec = jnp.zeros((n_mod, d), F32).at[:b].set(c).at[b].set(c_ctx)
    mods = _ada_mods(cvec, w_ada, b_ada)
    r_f, r_b = _head_expanders()
    zero_state = jnp.zeros((b, STATE, D_INNER), F32)
    zero_cnt = jnp.zeros((1, LANES), F32)
    xl, xc = x, ctx
    for i in range(depth):
        last = i == depth - 1
        lp = _layer_params(i, p)
        mod_l = [mods[i, :b, k * d:(k + 1) * d][:, None, :] for k in range(6)]
        mod_c = [jnp.broadcast_to(mods[i, b, k * d:(k + 1) * d], (b, 1, d)) for k in range(6)]

        zc, xbc_c, uc, gc, dtc = _in_proj(xc, mod_c[0], mod_c[1], lp["w_in"])
        if last:
            (s_f,) = _ssd_state_scan(xbc_c, dtc, zero_state, lp, r_f, rev=False, emit=False)
            (s_b,) = _ssd_state_scan(xbc_c, dtc, zero_state, lp, r_b, rev=True, emit=False)
        else:
            stb_c, act_c, s_b = _ssd_state_scan(xbc_c, dtc, zero_state, lp, r_b, rev=True, emit=True)
            ya_c, s_f = _ssd_main(act_c, dtc, zc, stb_c, zero_state, lp, r_f, r_b)
            yp_c = _pool_branch(uc, lp, False)

        zl, xbc_l, ul, gl, dtl = _in_proj(xl, mod_l[0], mod_l[1], lp["w_in"])
        stb_l, act_l, _ = _ssd_state_scan(xbc_l, dtl, s_b, lp, r_b, rev=True, emit=True)
        ya_l, _ = _ssd_main(act_l, dtl, zl, stb_l, s_f, lp, r_f, r_b)
        yp_l = _pool_branch(ul, lp, True)
        xl, h2_l, rt_l, cnt = _merge(ya_l, yp_l, gl, xl, mod_l[2], mod_l[3], mod_l[4], lp, zero_cnt, alpha)
        streams = [(h2_l, rt_l)]
        if not last:
            xc, h2_c, rt_c, cnt = _merge(ya_c, yp_c, gc, xc, mod_c[2], mod_c[3], mod_c[4], lp, cnt, alpha)
            streams.append((h2_c, rt_c))

        dests, block_e, n_used, n_rows = _moe_plan([rt for _, rt in streams], cnt)
        xin = jnp.zeros((n_rows, d // 2), U32)
        for (h2, _), dest in zip(streams, dests):
            xin = _dispatch(h2.reshape(-1, d // 2), dest, xin)
        yb = _experts(xin, block_e, n_used, lp)
        xl = _combine(dests[0], rt_l, xl, mod_l[5], lp, yb, alpha)
        if not last:
            xc = _combine(dests[1], rt_c, xc, mod_c[5], lp, yb, alpha)
    return xl
```

```python
import functools
import math

import jax
import jax.numpy as jnp
from jax import lax
from jax.experimental import pallas as pl
from jax.experimental.pallas import tpu as pltpu

F32 = jnp.float32
BF16 = jnp.bfloat16
I32 = jnp.int32

D_MODEL = 1024
D_INNER = 2048
HEADS = 32
HEADDIM = 64
GROUPS = 4
STATE = 128
D_BC = GROUPS * STATE
D_XBC = D_INNER + 2 * D_BC
CONV_K = 5
CHUNK = 128
GRID_W = 64
POOL_WINDOWS = (2, 4, 8, 16)
POOL_GROUP_DIM = 256
MOE_GROUPS = 4
MOE_EPG = 8
MOE_EXPERTS = 32
D_EXPERT = 512
MOE_BLOCK = 256
DEPTH_ALPHA_POW = 0.25
NORM_EPS = 1e-5
LANES = 128
ROUTE_LANE0 = MOE_GROUPS
VMEM_LIMIT = 56 * 1024 * 1024


def _cparams(sem, vmem=VMEM_LIMIT):
    return pltpu.CompilerParams(dimension_semantics=sem, vmem_limit_bytes=vmem)


def _silu(v):
    return v * jax.nn.sigmoid(v)


def _layer_norm(v, g, b):
    mu = jnp.mean(v, axis=-1, keepdims=True)
    d = v - mu
    var = jnp.mean(d * d, axis=-1, keepdims=True)
    return d * lax.rsqrt(var + NORM_EPS) * g + b


def _split_bf16(v):
    hi = v.astype(BF16)
    lo = (v - hi.astype(F32)).astype(BF16)
    return hi, lo


def _dot(a, b):
    return jnp.dot(a, b, preferred_element_type=F32)


U32 = jnp.uint32
HI16 = 0xFFFF0000


def _pack_rows(v):
    half = v.shape[1] // 2
    lo = lax.bitcast_convert_type(v[:, :half].astype(BF16).astype(F32), U32)
    hi = lax.bitcast_convert_type(v[:, half:].astype(BF16).astype(F32), U32)
    return (lo >> 16) | (hi & U32(HI16))


def _unpack_rows(w):
    lo = lax.bitcast_convert_type(w << 16, F32)
    hi = lax.bitcast_convert_type(w & U32(HI16), F32)
    return jnp.concatenate([lo, hi], axis=1)


def _ada_kernel(c_ref, w_ref, b_ref, o_ref):
    s = _silu(c_ref[...]).astype(BF16)
    o_ref[0] = _dot(s, w_ref[0].astype(BF16)) + b_ref[0]


def _ada_mods(cvec, w_ada, b_ada):
    depth, d, n = w_ada.shape
    r = cvec.shape[0]
    tn = D_MODEL
    return pl.pallas_call(
        _ada_kernel,
        grid=(depth, n // tn),
        in_specs=[pl.BlockSpec((r, d), lambda l, j: (0, 0)),
                  pl.BlockSpec((1, d, tn), lambda l, j: (l, 0, j)),
                  pl.BlockSpec((1, 1, tn), lambda l, j: (l, 0, j))],
        out_specs=pl.BlockSpec((1, r, tn), lambda l, j: (l, 0, j)),
        out_shape=jax.ShapeDtypeStruct((depth, r, n), F32),
        compiler_params=_cparams(("arbitrary", "arbitrary")),
    )(cvec, w_ada, b_ada.reshape(depth, 1, n))


def _inproj_kernel(x_ref, sh_ref, sc_ref, wz_ref, wx_ref, wu_ref, wg_ref, wdt_ref,
                   z_ref, xbc_ref, u_ref, g_ref, dt_ref):
    h = (x_ref[0] * (1.0 + sc_ref[0]) + sh_ref[0]).astype(BF16)
    z_ref[0] = _dot(h, wz_ref[...]).astype(BF16)
    xbc_ref[0] = _dot(h, wx_ref[...]).astype(BF16)
    u_ref[0] = _dot(h, wu_ref[...]).astype(BF16)
    g_ref[0] = _dot(h, wg_ref[...]).astype(BF16)
    dt_ref[0] = _dot(h, wdt_ref[...])


def _in_proj(x, shift, scale, wts):
    b, l, d = x.shape
    tm = min(256, l)
    wz, wx, wu, wg, wdt = wts
    tok = lambda n: pl.BlockSpec((1, tm, n), lambda i, j: (i, j, 0))
    mod = pl.BlockSpec((1, 1, d), lambda i, j: (i, 0, 0))
    wspec = lambda w: pl.BlockSpec(w.shape, lambda i, j: (0, 0), pipeline_mode=pl.Buffered(1))
    return pl.pallas_call(
        _inproj_kernel,
        grid=(b, l // tm),
        in_specs=[tok(d), mod, mod, wspec(wz), wspec(wx), wspec(wu), wspec(wg), wspec(wdt)],
        out_specs=[tok(D_INNER), tok(D_XBC), tok(D_MODEL), tok(2 * D_MODEL), tok(LANES)],
        out_shape=[jax.ShapeDtypeStruct((b, l, D_INNER), BF16),
                   jax.ShapeDtypeStruct((b, l, D_XBC), BF16),
                   jax.ShapeDtypeStruct((b, l, D_MODEL), BF16),
                   jax.ShapeDtypeStruct((b, l, 2 * D_MODEL), BF16),
                   jax.ShapeDtypeStruct((b, l, LANES), F32)],
        compiler_params=_cparams(("arbitrary", "arbitrary")),
    )(x, shift, scale, wz, wx, wu, wg, wdt)


HALO = 16


def _conv_act(prev_ref, cur_ref, next_ref, cw_ref, cb_ref, c, nc, ncols):
    q = CHUNK
    n = q + 16
    prev8 = prev_ref[0].astype(F32)[HALO - 8:HALO, :ncols]
    next8 = next_ref[0].astype(F32)[0:8, :ncols]
    e = jnp.concatenate([jnp.where(c > 0, prev8, 0.0), cur_ref[0].astype(F32)[:, :ncols],
                         jnp.where(c < nc - 1, next8, 0.0)], axis=0)
    y = [e * cw_ref[k:k + 1, :ncols] for k in range(CONV_K)]
    down = lambda v: pltpu.roll(v, 1, 0)
    up = lambda v: pltpu.roll(v, n - 1, 0)
    acc = y[2] + down(y[1] + down(y[0])) + up(y[3] + up(y[4]))
    return _silu(acc[8:8 + q] + cb_ref[:, :ncols])


def _dt_prep(dt_ref, dtb_ref, alog_ref):
    q = CHUNK
    lane = lax.broadcasted_iota(I32, (q, LANES), 1)
    raw = dt_ref[0] + dtb_ref[...]
    dt = jnp.maximum(raw, 0.0) + jnp.log1p(jnp.exp(-jnp.abs(raw)))
    dt = jnp.where(lane < 2 * HEADS, dt, 0.0)
    a = dt * (-jnp.exp(alog_ref[...]))
    ii = lax.broadcasted_iota(I32, (q, q), 0)
    jj = lax.broadcasted_iota(I32, (q, q), 1)
    tri = (jj <= ii).astype(BF16)
    a_hi, a_lo = _split_bf16(a)
    acs = _dot(tri, a_hi) + _dot(tri, a_lo)
    tot = acs[q - 1:q, :]
    return dt, a, acs, tot


def _expand(v, lane0, r_ref):
    lane = lax.broadcasted_iota(I32, v.shape, 1)
    vm = jnp.where((lane >= lane0) & (lane < lane0 + HEADS), v, 0.0)
    hi = vm.astype(BF16).astype(F32)
    both = hi + pltpu.roll(vm - hi, 2 * HEADS, 1)
    return _dot(both.astype(BF16), r_ref[...])


def _state_update(st_ref, bm, xw_bf, decay_x):
    gw = D_INNER // GROUPS
    for g in range(GROUPS):
        bgt = bm[:, g * STATE:(g + 1) * STATE].T.astype(BF16)
        cols = slice(g * gw, (g + 1) * gw)
        st_ref[:, cols] = st_ref[:, cols] * decay_x[:, cols] + _dot(bgt, xw_bf[:, cols])


def _ssd_state_kernel(*refs, rev, emit):
    (prev_ref, cur_ref, next_ref, dt_ref, h0_ref, cw_ref, cb_ref, dtb_ref, alog_ref, r_ref) = refs[:10]
    if emit:
        stout_ref, act_ref, sfin_ref, st_ref = refs[10:]
    else:
        sfin_ref, st_ref = refs[10:]
    s = pl.program_id(1)
    nc = pl.num_programs(1)
    c = (nc - 1 - s) if rev else s
    ncols = D_XBC if emit else D_INNER + D_BC

    @pl.when(s == 0)
    def _():
        st_ref[...] = h0_ref[0]

    if emit:
        stout_ref[0, 0] = st_ref[...].astype(BF16)

    act = _conv_act(prev_ref, cur_ref, next_ref, cw_ref, cb_ref, c, nc, ncols)
    if emit:
        act_ref[0] = act.astype(BF16)
    xs = act[:, :D_INNER]
    bm = act[:, D_INNER:D_INNER + D_BC]
    dt, a, acs, tot = _dt_prep(dt_ref, dtb_ref, alog_ref)
    if rev:
        w = jnp.exp(acs - a) * dt
        lane0 = HEADS
    else:
        w = jnp.exp(tot - acs) * dt
        lane0 = 0
    w_x = _expand(w, lane0, r_ref)
    dec_x = _expand(jnp.broadcast_to(jnp.exp(tot), (16, LANES)), lane0, r_ref)[0:1]
    _state_update(st_ref, bm, (w_x * xs).astype(BF16), dec_x)

    @pl.when(s == nc - 1)
    def _():
        sfin_ref[0] = st_ref[...]


def _ssd_state_scan(xbc, dt, h0, lp, r_mat, *, rev, emit):
    b, l, _ = xbc.shape
    nc = l // CHUNK
    nh = l // HALO
    hpc = CHUNK // HALO
    seq = (lambda s: nc - 1 - s) if rev else (lambda s: s)
    par = lambda shape: pl.BlockSpec(shape, lambda i, s: (0,) * len(shape))
    in_specs = [
        pl.BlockSpec((1, HALO, D_XBC), lambda i, s: (i, jnp.maximum(seq(s) * hpc - 1, 0), 0)),
        pl.BlockSpec((1, CHUNK, D_XBC), lambda i, s: (i, seq(s), 0)),
        pl.BlockSpec((1, HALO, D_XBC), lambda i, s: (i, jnp.minimum((seq(s) + 1) * hpc, nh - 1), 0)),
        pl.BlockSpec((1, CHUNK, LANES), lambda i, s: (i, seq(s), 0)),
        pl.BlockSpec((1, STATE, D_INNER), lambda i, s: (i, 0, 0)),
        par((CONV_K, D_XBC)), par((1, D_XBC)), par((1, LANES)), par((1, LANES)), par((LANES, D_INNER)),
    ]
    out_specs = [pl.BlockSpec((1, STATE, D_INNER), lambda i, s: (i, 0, 0))]
    out_shape = [jax.ShapeDtypeStruct((b, STATE, D_INNER), F32)]
    if emit:
        out_specs = [pl.BlockSpec((1, 1, STATE, D_INNER), lambda i, s: (i, seq(s), 0, 0)),
                     pl.BlockSpec((1, CHUNK, D_XBC), lambda i, s: (i, seq(s), 0))] + out_specs
        out_shape = [jax.ShapeDtypeStruct((b, nc, STATE, D_INNER), BF16),
                     jax.ShapeDtypeStruct((b, l, D_XBC), BF16)] + out_shape
    return pl.pallas_call(
        functools.partial(_ssd_state_kernel, rev=rev, emit=emit),
        grid=(b, nc),
        in_specs=in_specs, out_specs=out_specs, out_shape=out_shape,
        scratch_shapes=[pltpu.VMEM((STATE, D_INNER), F32)],
        compiler_params=_cparams(("arbitrary", "arbitrary")),
    )(xbc, xbc, xbc, dt, h0, lp["conv_w"], lp["conv_b"], lp["dt_bias"], lp["a_log"], r_mat)


def _ssd_main_kernel(act_ref, dt_ref, z_ref, stb_ref, h0_ref,
                     dtb_ref, alog_ref, dsk_ref, ng_ref, rf_ref, rb_ref,
                     y_ref, sfin_ref, st_ref, ys_ref):
    q = CHUNK
    c = pl.program_id(1)
    nc = pl.num_programs(1)

    @pl.when(c == 0)
    def _():
        st_ref[...] = h0_ref[0]

    xs_bf = act_ref[0, :, :D_INNER]
    xs = xs_bf.astype(F32)
    dt, a, acs, tot = _dt_prep(dt_ref, dtb_ref, alog_ref)
    eb = acs - a

    ef_x = _expand(jnp.exp(acs), 0, rf_ref)
    eb_x = _expand(jnp.exp(tot - eb), HEADS, rb_ref)
    tf_x = _expand(jnp.exp(tot - acs) * dt, 0, rf_ref)
    dec_x = _expand(jnp.broadcast_to(jnp.exp(tot), (16, LANES)), 0, rf_ref)[0:1]

    lane = lax.broadcasted_iota(I32, (q, LANES), 1)
    ldt = jnp.log(dt)
    qm = jnp.where(lane < HEADS, acs - ldt, jnp.where(lane < 2 * HEADS, eb + ldt, pltpu.roll(dt, 2 * HEADS, 1)))
    qt = qm.T
    ii = lax.broadcasted_iota(I32, (q, q), 0)
    jj = lax.broadcasted_iota(I32, (q, q), 1)
    lower = jj <= ii
    diag = jj == ii
    lane2 = lax.broadcasted_iota(I32, (q, LANES), 1)
    left = lane2 < HEADDIM

    stf_bf = st_ref[...].astype(BF16)
    gw = D_INNER // GROUPS
    hpg = HEADS // GROUPS
    for g in range(GROUPS):
        bg = act_ref[0, :, D_INNER + g * STATE:D_INNER + (g + 1) * STATE]
        cg = act_ref[0, :, D_INNER + D_BC + g * STATE:D_INNER + D_BC + (g + 1) * STATE]
        cb = lax.dot_general(cg, bg, (((1,), (1,)), ((), ())), preferred_element_type=F32)
        cols = slice(g * gw, (g + 1) * gw)
        y_off = ef_x[:, cols] * _dot(cg, stf_bf[:, cols]) + eb_x[:, cols] * _dot(cg, stb_ref[0, 0][:, cols])
        for pr in range(hpg // 2):
            ms = []
            for hh in range(2):
                h = g * hpg + pr * 2 + hh
                afc = acs[:, h:h + 1]
                afr = qt[h:h + 1, :]
                ebc = eb[:, HEADS + h:HEADS + h + 1]
                ebr = qt[HEADS + h:HEADS + h + 1, :]
                wb = qt[3 * HEADS + h:3 * HEADS + h + 1, :]
                arg = jnp.where(lower, afc - afr, ebr - ebc)
                ms.append((cb * (jnp.exp(arg) + jnp.where(diag, wb, 0.0))).astype(BF16))
            lhs = jnp.concatenate(ms, axis=1)
            c0 = g * gw + pr * 2 * HEADDIM
            xp = xs_bf[:, c0:c0 + 2 * HEADDIM]
            zero = jnp.zeros_like(xp)
            rhs = jnp.concatenate([jnp.where(left, xp, zero), jnp.where(left, zero, xp)], axis=0)
            y_pair = _dot(lhs, rhs) + y_off[:, pr * 2 * HEADDIM:(pr + 1) * 2 * HEADDIM]
            y_pair = y_pair + dsk_ref[:, c0:c0 + 2 * HEADDIM] * xs[:, c0:c0 + 2 * HEADDIM]
            ys_ref[:, c0:c0 + 2 * HEADDIM] = y_pair

    y = ys_ref[...]
    yz = y * _silu(z_ref[0].astype(F32))
    ms2 = jnp.mean(yz * yz, axis=-1, keepdims=True)
    y_ref[0] = (yz * lax.rsqrt(ms2 + NORM_EPS) * ng_ref[...]).astype(BF16)

    bm = act_ref[0, :, D_INNER:D_INNER + D_BC].astype(F32)
    _state_update(st_ref, bm, (tf_x * xs).astype(BF16), dec_x)

    @pl.when(c == nc - 1)
    def _():
        sfin_ref[0] = st_ref[...]


def _ssd_main(act, dt, z, stb, h0, lp, r_f, r_b):
    b, l, _ = act.shape
    nc = l // CHUNK
    par = lambda shape: pl.BlockSpec(shape, lambda i, s: (0,) * len(shape))
    in_specs = [
        pl.BlockSpec((1, CHUNK, D_XBC), lambda i, s: (i, s, 0)),
        pl.BlockSpec((1, CHUNK, LANES), lambda i, s: (i, s, 0)),
        pl.BlockSpec((1, CHUNK, D_INNER), lambda i, s: (i, s, 0)),
        pl.BlockSpec((1, 1, STATE, D_INNER), lambda i, s: (i, s, 0, 0)),
        pl.BlockSpec((1, STATE, D_INNER), lambda i, s: (i, 0, 0)),
        par((1, LANES)), par((1, LANES)),
        par((1, D_INNER)), par((1, D_INNER)), par((LANES, D_INNER)), par((LANES, D_INNER)),
    ]
    return pl.pallas_call(
        _ssd_main_kernel,
        grid=(b, nc),
        in_specs=in_specs,
        out_specs=[pl.BlockSpec((1, CHUNK, D_INNER), lambda i, s: (i, s, 0)),
                   pl.BlockSpec((1, STATE, D_INNER), lambda i, s: (i, 0, 0))],
        out_shape=[jax.ShapeDtypeStruct((b, l, D_INNER), BF16),
                   jax.ShapeDtypeStruct((b, STATE, D_INNER), F32)],
        scratch_shapes=[pltpu.VMEM((STATE, D_INNER), F32), pltpu.VMEM((CHUNK, D_INNER), F32)],
        compiler_params=_cparams(("arbitrary", "arbitrary")),
    )(act, dt, z, stb, h0, lp["dt_bias"], lp["a_log"], lp["d_skip_x"], lp["ssd_norm_g"], r_f, r_b)


def _pool_kernel(*refs, width, vertical, halo, hblock):
    if vertical:
        prev_ref, cur_ref, next_ref, pw_ref, ps_ref, o_ref = refs
    else:
        cur_ref, pw_ref, ps_ref, o_ref = refs
    t = pl.program_id(1)
    nt = pl.num_programs(1)
    tp = cur_ref.shape[1]
    n_tok = nt * tp
    shift = int(math.log2(width))
    idx = lax.broadcasted_iota(I32, (tp, 1), 0) + t * tp
    colpos = idx & (width - 1)
    rowpos = idx >> shift
    n_rows = n_tok // width
    bi = lax.broadcasted_iota(I32, (hblock, hblock), 0)
    bj = lax.broadcasted_iota(I32, (hblock, hblock), 1)
    same_row = (bi >> shift) == (bj >> shift)
    cur = cur_ref[0].astype(F32)
    if vertical:
        prev = jnp.where(t > 0, prev_ref[0].astype(F32), 0.0)
        nxt = jnp.where(t < nt - 1, next_ref[0].astype(F32), 0.0)
    for gi, k in enumerate(POOL_WINDOWS):
        cols = slice(gi * POOL_GROUP_DIM, (gi + 1) * POOL_GROUP_DIM)
        ug = cur[:, cols]
        lo = k // 2
        if vertical:
            e = jnp.concatenate([prev[:, cols], ug, nxt[:, cols]], axis=0)
            step = width
            m = 1
            while m < k:
                n = e.shape[0] - step
                e = e[0:n] + e[step:step + n]
                step *= 2
                m *= 2
            start = halo - lo * width
            s = e[start:start + tp]
            cnt_r = (jnp.minimum(rowpos - lo + k, n_rows) - jnp.maximum(rowpos - lo, 0)).astype(F32)
        else:
            s = ug
            cnt_r = jnp.ones((tp, 1), F32)
        band = (same_row & (bj - bi >= -lo) & (bj - bi < k - lo)).astype(BF16)
        s_bf = s.astype(BF16)
        hs = [_dot(band, s_bf[r * hblock:(r + 1) * hblock]) for r in range(tp // hblock)]
        hsum = hs[0] if len(hs) == 1 else jnp.concatenate(hs, axis=0)
        cnt_c = (jnp.minimum(colpos - lo + k, width) - jnp.maximum(colpos - lo, 0)).astype(F32)
        mean = hsum / (cnt_r * cnt_c)
        y = _dot((mean - ug).astype(BF16), pw_ref[gi])
        o_ref[0, :, cols] = (y * ps_ref[:, cols]).astype(BF16)


def _pool_branch(u, lp, on_grid):
    b, l, d = u.shape
    pw, ps = lp["pool_w"], lp["pool_scale"]
    if on_grid:
        width, halo = GRID_W, (max(POOL_WINDOWS) // 2) * GRID_W
        tp = min(1024, l)
        hb = tp // halo
        nhb = l // halo
        in_specs = [pl.BlockSpec((1, halo, d), lambda i, t: (i, jnp.maximum(t * hb - 1, 0), 0)),
                    pl.BlockSpec((1, tp, d), lambda i, t: (i, t, 0)),
                    pl.BlockSpec((1, halo, d), lambda i, t: (i, jnp.minimum((t + 1) * hb, nhb - 1), 0))]
        args = (u, u, u)
        kern = functools.partial(_pool_kernel, width=width, vertical=True, halo=halo, hblock=LANES)
    else:
        tp = l
        in_specs = [pl.BlockSpec((1, tp, d), lambda i, t: (i, t, 0))]
        args = (u,)
        kern = functools.partial(_pool_kernel, width=l, vertical=False, halo=0, hblock=l)
    in_specs += [pl.BlockSpec(pw.shape, lambda i, t: (0, 0, 0)), pl.BlockSpec(ps.shape, lambda i, t: (0, 0))]
    return pl.pallas_call(
        kern,
        grid=(b, l // tp),
        in_specs=in_specs,
        out_specs=pl.BlockSpec((1, tp, d), lambda i, t: (i, t, 0)),
        out_shape=jax.ShapeDtypeStruct((b, l, d), BF16),
        compiler_params=_cparams(("arbitrary", "arbitrary")),
    )(*args, pw, ps)


def _merge_kernel(ya_ref, yp_ref, gt_ref, x_ref, g1_ref, sh2_ref, sc2_ref, bg_ref,
                  wa_ref, wb_ref, wo_ref, lng_ref, lnb_ref, wrh_ref, wrl_ref, br_ref, cin_ref,
                  xo_ref, h2_ref, rt_ref, cout_ref, cnt_ref, *, alpha):
    i = pl.program_id(0)
    j = pl.program_id(1)
    tm = x_ref.shape[1]

    @pl.when((i == 0) & (j == 0))
    def _():
        cnt_ref[...] = cin_ref[...]

    gate = jax.nn.sigmoid(gt_ref[0].astype(F32) + bg_ref[...])
    merged = (gate[:, :D_MODEL] * _dot(ya_ref[0], wa_ref[...])
              + gate[:, D_MODEL:] * _dot(yp_ref[0], wb_ref[...]))
    out = _dot(merged.astype(BF16), wo_ref[...])
    xn = _layer_norm(alpha * x_ref[0] + g1_ref[0] * out, lng_ref[...], lnb_ref[...])
    xo_ref[0] = xn
    h2 = xn * (1.0 + sc2_ref[0]) + sh2_ref[0]
    h2_ref[0] = _pack_rows(h2)

    h_hi, h_lo = _split_bf16(h2)
    logits = _dot(h_hi, wrh_ref[...]) + _dot(h_lo, wrh_ref[...]) + _dot(h_hi, wrl_ref[...]) + br_ref[...]
    lane = lax.broadcasted_iota(I32, (tm, LANES), 1)
    neg = jnp.float32(-jnp.inf)
    lg = jnp.where(lane < MOE_GROUPS, logits, neg)
    mg = jnp.max(lg, axis=-1, keepdims=True)
    grp = jnp.min(jnp.where(lg == mg, lane, LANES), axis=-1, keepdims=True)
    p_grp = 1.0 / jnp.sum(jnp.exp(lg - mg), axis=-1, keepdims=True)
    lo_lane = ROUTE_LANE0 + grp * MOE_EPG
    le = jnp.where((lane >= lo_lane) & (lane < lo_lane + MOE_EPG), logits, neg)
    v1 = jnp.max(le, axis=-1, keepdims=True)
    i1 = jnp.min(jnp.where(le == v1, lane, LANES), axis=-1, keepdims=True)
    le2 = jnp.where(lane == i1, neg, le)
    v2 = jnp.max(le2, axis=-1, keepdims=True)
    i2 = jnp.min(jnp.where(le2 == v2, lane, LANES), axis=-1, keepdims=True)
    e2 = jnp.exp(v2 - v1)
    w1 = p_grp / (1.0 + e2)
    w2 = p_grp * e2 / (1.0 + e2)

    oh1 = (lane == i1).astype(F32)
    oh2 = (lane == i2).astype(F32)
    oh = oh1 + oh2
    ri = lax.broadcasted_iota(I32, (tm, tm), 0)
    rj = lax.broadcasted_iota(I32, (tm, tm), 1)
    before = _dot((rj < ri).astype(BF16), oh.astype(BF16)) + cnt_ref[...]
    rank1 = jnp.sum(oh1 * before, axis=-1, keepdims=True)
    rank2 = jnp.sum(oh2 * before, axis=-1, keepdims=True)
    cnt_ref[...] = cnt_ref[...] + jnp.sum(oh, axis=0, keepdims=True)
    cout_ref[...] = cnt_ref[...]

    slab = jnp.where(lane == 0, (i1 - ROUTE_LANE0).astype(F32),
           jnp.where(lane == 1, (i2 - ROUTE_LANE0).astype(F32),
           jnp.where(lane == 2, w1,
           jnp.where(lane == 3, w2,
           jnp.where(lane == 4, rank1,
           jnp.where(lane == 5, rank2, 0.0))))))
    rt_ref[0] = slab


def _merge(ya, yp, gates, x, g1, sh2, sc2, lp, cnt_in, alpha):
    b, l, d = x.shape
    tm = min(512, l)
    tok = lambda n: pl.BlockSpec((1, tm, n), lambda i, j: (i, j, 0))
    mod = pl.BlockSpec((1, 1, d), lambda i, j: (i, 0, 0))
    par = lambda a: pl.BlockSpec(a.shape, lambda i, j: (0,) * a.ndim)
    params = (lp["b_gate"], lp["w_branch_a"], lp["w_branch_b"], lp["w_out"], lp["ln1_g"], lp["ln1_b"],
              lp["w_r_hi"], lp["w_r_lo"], lp["b_r"], cnt_in)
    return pl.pallas_call(
        functools.partial(_merge_kernel, alpha=alpha),
        grid=(b, l // tm),
        in_specs=[tok(D_INNER), tok(d), tok(2 * d), tok(d), mod, mod, mod] + [par(a) for a in params],
        out_specs=[tok(d), tok(d // 2), tok(LANES), pl.BlockSpec((1, LANES), lambda i, j: (0, 0))],
        out_shape=[jax.ShapeDtypeStruct((b, l, d), F32), jax.ShapeDtypeStruct((b, l, d // 2), U32),
                   jax.ShapeDtypeStruct((b, l, LANES), F32), jax.ShapeDtypeStruct((1, LANES), F32)],
        scratch_shapes=[pltpu.VMEM((1, LANES), F32)],
        compiler_params=_cparams(("arbitrary", "arbitrary")),
    )(ya, yp, gates, x, g1, sh2, sc2, *params)


ROW_TILE = 512
DMA_UNROLL = 8


def _row_copy(src_ref, src_row, dst_ref, dst_row, sem):
    return pltpu.make_async_copy(src_ref.at[pl.ds(src_row, 1)], dst_ref.at[pl.ds(dst_row, 1)], sem)


def _dispatch_kernel(h_ref, dest_ref, xin_any, xout_any, idx_smem, sem_idx, sem_row):
    del xin_any
    td = h_ref.shape[0]
    cp = pltpu.make_async_copy(dest_ref.at[0, 0], idx_smem, sem_idx)
    cp.start()
    cp.wait()

    def issue(r, carry):
        _row_copy(h_ref, r, xout_any, idx_smem[2 * r], sem_row).start(priority=0)
        _row_copy(h_ref, r, xout_any, idx_smem[2 * r + 1], sem_row).start(priority=1)
        return carry

    lax.fori_loop(0, td, issue, 0, unroll=DMA_UNROLL)

    def drain(r, carry):
        _row_copy(h_ref, 0, xout_any, 0, sem_row).wait()
        _row_copy(h_ref, 0, xout_any, 0, sem_row).wait()
        return carry

    lax.fori_loop(0, td, drain, 0, unroll=DMA_UNROLL)


def _dispatch(h2, dest, xin):
    t, d = h2.shape
    td = min(ROW_TILE, t)
    dest3 = dest.reshape(t // td, 1, 2 * td)
    return pl.pallas_call(
        _dispatch_kernel,
        grid=(t // td,),
        in_specs=[pl.BlockSpec((td, d), lambda i: (i, 0)),
                  pl.BlockSpec((1, 1, 2 * td), lambda i: (i, 0, 0)),
                  pl.BlockSpec(memory_space=pl.ANY)],
        out_specs=pl.BlockSpec(memory_space=pl.ANY),
        out_shape=jax.ShapeDtypeStruct(xin.shape, xin.dtype),
        scratch_shapes=[pltpu.SMEM((2 * td,), I32), pltpu.SemaphoreType.DMA(()), pltpu.SemaphoreType.DMA(())],
        input_output_aliases={2: 0},
        compiler_params=_cparams(("arbitrary",)),
    )(h2, dest3, xin)


def _expert_kernel(be_ref, nu_ref, x_ref, wg_ref, wu_ref, wd_ref, o_ref, wgb_ref, wub_ref, wdb_ref):
    i = pl.program_id(0)

    @pl.when((i == 0) | (be_ref[i] != be_ref[jnp.maximum(i - 1, 0)]))
    def _():
        wgb_ref[...] = wg_ref[0, 0].astype(BF16)
        wub_ref[...] = wu_ref[0, 0].astype(BF16)
        wdb_ref[...] = wd_ref[0, 0].astype(BF16)

    @pl.when(i < nu_ref[0])
    def _():
        xb = _unpack_rows(x_ref[...]).astype(BF16)
        gte = _dot(xb, wgb_ref[...])
        up = _dot(xb, wub_ref[...])
        o_ref[...] = _pack_rows(_dot((_silu(gte) * up).astype(BF16), wdb_ref[...]))

    @pl.when(i >= nu_ref[0])
    def _():
        o_ref[...] = jnp.zeros_like(o_ref)


def _experts(xin, block_e, n_used, w_eg, w_eu, w_ed, layer):
    n_rows, dh = xin.shape
    d = 2 * dh
    nb = n_rows // MOE_BLOCK
    grid_spec = pltpu.PrefetchScalarGridSpec(
        num_scalar_prefetch=2,
        grid=(nb,),
        in_specs=[pl.BlockSpec((MOE_BLOCK, dh), lambda i, be, nu: (i, 0)),
                  pl.BlockSpec((1, 1, d, D_EXPERT), lambda i, be, nu: (layer, be[i], 0, 0)),
                  pl.BlockSpec((1, 1, d, D_EXPERT), lambda i, be, nu: (layer, be[i], 0, 0)),
                  pl.BlockSpec((1, 1, D_EXPERT, d), lambda i, be, nu: (layer, be[i], 0, 0))],
        out_specs=pl.BlockSpec((MOE_BLOCK, dh), lambda i, be, nu: (i, 0)),
        scratch_shapes=[pltpu.VMEM((d, D_EXPERT), BF16), pltpu.VMEM((d, D_EXPERT), BF16),
                        pltpu.VMEM((D_EXPERT, d), BF16)],
    )
    return pl.pallas_call(
        _expert_kernel,
        grid_spec=grid_spec,
        out_shape=jax.ShapeDtypeStruct((n_rows, dh), U32),
        compiler_params=_cparams(("arbitrary",)),
    )(block_e, n_used, xin, w_eg, w_eu, w_ed)


def _combine_kernel(dest_ref, rt_ref, x_ref, g2_ref, lng_ref, lnb_ref, yb_any,
                    o_ref, buf_ref, idx_smem, sem_idx, sem_row, *, alpha):
    tc = x_ref.shape[1]
    cp = pltpu.make_async_copy(dest_ref.at[0, 0], idx_smem, sem_idx)
    cp.start()
    cp.wait()

    def issue(r, carry):
        _row_copy(yb_any, idx_smem[2 * r], buf_ref.at[0], r, sem_row).start(priority=0)
        _row_copy(yb_any, idx_smem[2 * r + 1], buf_ref.at[1], r, sem_row).start(priority=1)
        return carry

    lax.fori_loop(0, tc, issue, 0, unroll=DMA_UNROLL)

    def drain(r, carry):
        _row_copy(yb_any, 0, buf_ref.at[0], 0, sem_row).wait()
        _row_copy(yb_any, 0, buf_ref.at[1], 0, sem_row).wait()
        return carry

    lax.fori_loop(0, tc, drain, 0, unroll=DMA_UNROLL)

    rt = rt_ref[0]
    y = rt[:, 2:3] * _unpack_rows(buf_ref[0]) + rt[:, 3:4] * _unpack_rows(buf_ref[1])
    o_ref[0] = _layer_norm(alpha * x_ref[0] + g2_ref[0] * y, lng_ref[...], lnb_ref[...])


def _combine(dest, route, x, g2, lp, yb, alpha):
    b, l, d = x.shape
    tc = min(ROW_TILE, l)
    npb = l // tc
    dest3 = dest.reshape(b * npb, 1, 2 * tc)
    par = lambda a: pl.BlockSpec(a.shape, lambda i, j: (0,) * a.ndim)
    return pl.pallas_call(
        functools.partial(_combine_kernel, alpha=alpha),
        grid=(b, npb),
        in_specs=[pl.BlockSpec((1, 1, 2 * tc), lambda i, j: (i * npb + j, 0, 0)),
                  pl.BlockSpec((1, tc, LANES), lambda i, j: (i, j, 0)),
                  pl.BlockSpec((1, tc, d), lambda i, j: (i, j, 0)),
                  pl.BlockSpec((1, 1, d), lambda i, j: (i, 0, 0)),
                  par(lp["ln2_g"]), par(lp["ln2_b"]),
                  pl.BlockSpec(memory_space=pl.ANY)],
        out_specs=pl.BlockSpec((1, tc, d), lambda i, j: (i, j, 0)),
        out_shape=jax.ShapeDtypeStruct((b, l, d), F32),
        scratch_shapes=[pltpu.VMEM((2, tc, d // 2), U32), pltpu.SMEM((2 * tc,), I32),
                        pltpu.SemaphoreType.DMA(()), pltpu.SemaphoreType.DMA(())],
        compiler_params=_cparams(("arbitrary", "arbitrary")),
    )(dest3, route, x, g2, lp["ln2_g"], lp["ln2_b"], yb)


def _moe_plan(routes, counts):
    cnt = counts[0, ROUTE_LANE0:ROUTE_LANE0 + MOE_EXPERTS].astype(I32)
    padded = (cnt + MOE_BLOCK - 1) // MOE_BLOCK * MOE_BLOCK
    pad_end = jnp.cumsum(padded)
    pad_start = pad_end - padded
    n_assign = sum(r.shape[0] * r.shape[1] for r in routes) * 2
    n_blocks = -(-n_assign // MOE_BLOCK) + MOE_EXPERTS
    first_row = jnp.arange(n_blocks, dtype=I32)[:, None] * MOE_BLOCK
    block_e = jnp.minimum(jnp.sum((pad_end[None, :] <= first_row).astype(I32), axis=1), MOE_EXPERTS - 1)
    n_used = (pad_end[-1:] // MOE_BLOCK).astype(I32)
    eidx = jnp.arange(MOE_EXPERTS, dtype=I32)
    dests = []
    for r in routes:
        e = r[..., 0:2].astype(I32)
        rank = r[..., 4:6].astype(I32)
        start = jnp.sum(jnp.where(e[..., None] == eidx, pad_start, 0), axis=-1)
        dests.append((start + rank).reshape(-1, 2))
    return dests, block_e, n_used, n_blocks * MOE_BLOCK


def _layer_params(i, p):
    w_in = p["w_in"][i]
    s0, s1, s2, s3 = D_INNER, D_INNER + D_XBC, D_INNER + D_XBC + 2 * HEADS, D_INNER + D_XBC + 2 * HEADS + D_MODEL
    pad = LANES - 2 * HEADS
    wdt = jnp.pad(w_in[:, s1:s2], ((0, 0), (0, pad)))
    w_r = jnp.concatenate([p["w_router_group"][i],
                           jnp.transpose(p["w_router_expert"][i], (1, 0, 2)).reshape(D_MODEL, MOE_EXPERTS)], axis=1)
    w_r = jnp.pad(w_r, ((0, 0), (0, LANES - w_r.shape[1])))
    w_r_hi = w_r.astype(BF16)
    b_r = jnp.concatenate([p["b_router_group"][i], p["b_router_expert"][i].reshape(-1)])
    return {
        "w_in": tuple(w.astype(BF16) for w in (w_in[:, :s0], w_in[:, s0:s1], w_in[:, s2:s3], w_in[:, s3:], wdt)),
        "conv_w": p["conv_w"][i], "conv_b": p["conv_b"][i][None],
        "dt_bias": jnp.pad(p["dt_bias"][i].reshape(1, -1), ((0, 0), (0, pad))),
        "a_log": jnp.pad(p["a_log"][i].reshape(1, -1), ((0, 0), (0, pad))),
        "d_skip_x": jnp.repeat(p["d_skip"][i], HEADDIM)[None],
        "ssd_norm_g": p["ssd_norm_g"][i][None],
        "pool_w": p["pool_w"][i].astype(BF16), "pool_scale": p["pool_scale"][i][None],
        "b_gate": p["b_gate"][i][None],
        "w_branch_a": p["w_branch_a"][i].astype(BF16), "w_branch_b": p["w_branch_b"][i].astype(BF16),
        "w_out": p["w_out"][i].astype(BF16),
        "ln1_g": p["ln1_g"][i][None], "ln1_b": p["ln1_b"][i][None],
        "ln2_g": p["ln2_g"][i][None], "ln2_b": p["ln2_b"][i][None],
        "w_r_hi": w_r_hi, "w_r_lo": (w_r - w_r_hi.astype(F32)).astype(BF16),
        "b_r": jnp.pad(b_r, (0, LANES - b_r.shape[0]))[None],
    }


def _head_expanders():
    col_head = jnp.arange(D_INNER, dtype=I32) // HEADDIM
    row = jnp.arange(LANES, dtype=I32)[:, None] % (2 * HEADS)
    return (row == col_head).astype(BF16), (row == col_head + HEADS).astype(BF16)


def kernel(x, c, ctx, c_ctx, w_ada, b_ada, w_in, b_gate, conv_w, conv_b, dt_bias, a_log, d_skip, ssd_norm_g, pool_w, pool_scale, w_branch_a, w_branch_b, w_out, ln1_g, ln1_b, ln2_g, ln2_b, w_router_group, b_router_group, w_router_expert, b_router_expert, w_expert_gate, w_expert_up, w_expert_down):
    p = dict(w_in=w_in, b_gate=b_gate, conv_w=conv_w, conv_b=conv_b, dt_bias=dt_bias, a_log=a_log, d_skip=d_skip,
             ssd_norm_g=ssd_norm_g, pool_w=pool_w, pool_scale=pool_scale, w_branch_a=w_branch_a,
             w_branch_b=w_branch_b, w_out=w_out, ln1_g=ln1_g, ln1_b=ln1_b, ln2_g=ln2_g, ln2_b=ln2_b,
             w_router_group=w_router_group, b_router_group=b_router_group, w_router_expert=w_router_expert,
             b_router_expert=b_router_expert, w_expert_gate=w_expert_gate, w_expert_up=w_expert_up,
             w_expert_down=w_expert_down)
    b, l, d = x.shape
    depth = w_ada.shape[0]
    alpha = (2.0 * depth) ** DEPTH_ALPHA_POW
    n_mod = -(-(b + 1) // 8) * 8
    cvec = jnp.zeros((n_mod, d), F32).at[:b].set(c).at[b].set(c_ctx)
    mods = _ada_mods(cvec, w_ada, b_ada)
    r_f, r_b = _head_expanders()
    zero_state = jnp.zeros((b, STATE, D_INNER), F32)
    zero_cnt = jnp.zeros((1, LANES), F32)
    xl, xc = x, ctx
    for i in range(depth):
        last = i == depth - 1
        lp = _layer_params(i, p)
        mod_l = [mods[i, :b, k * d:(k + 1) * d][:, None, :] for k in range(6)]
        mod_c = [jnp.broadcast_to(mods[i, b, k * d:(k + 1) * d], (b, 1, d)) for k in range(6)]

        zc, xbc_c, uc, gc, dtc = _in_proj(xc, mod_c[0], mod_c[1], lp["w_in"])
        if last:
            (s_f,) = _ssd_state_scan(xbc_c, dtc, zero_state, lp, r_f, rev=False, emit=False)
            (s_b,) = _ssd_state_scan(xbc_c, dtc, zero_state, lp, r_b, rev=True, emit=False)
        else:
            stb_c, act_c, s_b = _ssd_state_scan(xbc_c, dtc, zero_state, lp, r_b, rev=True, emit=True)
            ya_c, s_f = _ssd_main(act_c, dtc, zc, stb_c, zero_state, lp, r_f, r_b)
            yp_c = _pool_branch(uc, lp, False)

        zl, xbc_l, ul, gl, dtl = _in_proj(xl, mod_l[0], mod_l[1], lp["w_in"])
        stb_l, act_l, _ = _ssd_state_scan(xbc_l, dtl, s_b, lp, r_b, rev=True, emit=True)
        ya_l, _ = _ssd_main(act_l, dtl, zl, stb_l, s_f, lp, r_f, r_b)
        yp_l = _pool_branch(ul, lp, True)
        xl, h2_l, rt_l, cnt = _merge(ya_l, yp_l, gl, xl, mod_l[2], mod_l[3], mod_l[4], lp, zero_cnt, alpha)
        streams = [(h2_l, rt_l)]
        if not last:
            xc, h2_c, rt_c, cnt = _merge(ya_c, yp_c, gc, xc, mod_c[2], mod_c[3], mod_c[4], lp, cnt, alpha)
            streams.append((h2_c, rt_c))

        dests, block_e, n_used, n_rows = _moe_plan([rt for _, rt in streams], cnt)
        xin = jnp.zeros((n_rows, d // 2), U32)
        for (h2, _), dest in zip(streams, dests):
            xin = _dispatch(h2.reshape(-1, d // 2), dest, xin)
        yb = _experts(xin, block_e, n_used, w_expert_gate, w_expert_up, w_expert_down, i)
        xl = _combine(dests[0], rt_l, xl, mod_l[5], lp, yb, alpha)
        if not last:
            xc = _combine(dests[1], rt_c, xc, mod_c[5], lp, yb, alpha)
    return xl
```

```python
import functools
import math

import jax
import jax.numpy as jnp
from jax import lax
from jax.experimental import pallas as pl
from jax.experimental.pallas import tpu as pltpu

F32 = jnp.float32
BF16 = jnp.bfloat16
I32 = jnp.int32

D_MODEL = 1024
D_INNER = 2048
HEADS = 32
HEADDIM = 64
GROUPS = 4
STATE = 128
D_BC = GROUPS * STATE
D_XBC = D_INNER + 2 * D_BC
CONV_K = 5
CHUNK = 128
GRID_W = 64
POOL_WINDOWS = (2, 4, 8, 16)
POOL_GROUP_DIM = 256
MOE_GROUPS = 4
MOE_EPG = 8
MOE_EXPERTS = 32
D_EXPERT = 512
MOE_BLOCK = 512
DEPTH_ALPHA_POW = 0.25
NORM_EPS = 1e-5
LANES = 128
ROUTE_LANE0 = MOE_GROUPS
VMEM_LIMIT = 56 * 1024 * 1024


def _cparams(sem, vmem=VMEM_LIMIT):
    return pltpu.CompilerParams(dimension_semantics=sem, vmem_limit_bytes=vmem)


def _silu(v):
    return v * jax.nn.sigmoid(v)


def _layer_norm(v, g, b):
    mu = jnp.mean(v, axis=-1, keepdims=True)
    d = v - mu
    var = jnp.mean(d * d, axis=-1, keepdims=True)
    return d * lax.rsqrt(var + NORM_EPS) * g + b


def _split_bf16(v):
    hi = v.astype(BF16)
    lo = (v - hi.astype(F32)).astype(BF16)
    return hi, lo


def _dot(a, b):
    return jnp.dot(a, b, preferred_element_type=F32)


U32 = jnp.uint32
HI16 = 0xFFFF0000


def _pack_rows(v):
    half = v.shape[1] // 2
    lo = lax.bitcast_convert_type(v[:, :half].astype(BF16).astype(F32), U32)
    hi = lax.bitcast_convert_type(v[:, half:].astype(BF16).astype(F32), U32)
    return (lo >> 16) | (hi & U32(HI16))


def _unpack_rows(w):
    lo = lax.bitcast_convert_type(w << 16, F32)
    hi = lax.bitcast_convert_type(w & U32(HI16), F32)
    return jnp.concatenate([lo, hi], axis=1)


def _ada_kernel(c_ref, w_ref, b_ref, o_ref):
    s = _silu(c_ref[...]).astype(BF16)
    o_ref[0] = _dot(s, w_ref[0].astype(BF16)) + b_ref[0]


def _ada_mods(cvec, w_ada, b_ada):
    depth, d, n = w_ada.shape
    r = cvec.shape[0]
    tn = D_MODEL
    return pl.pallas_call(
        _ada_kernel,
        grid=(depth, n // tn),
        in_specs=[pl.BlockSpec((r, d), lambda l, j: (0, 0)),
                  pl.BlockSpec((1, d, tn), lambda l, j: (l, 0, j)),
                  pl.BlockSpec((1, 1, tn), lambda l, j: (l, 0, j))],
        out_specs=pl.BlockSpec((1, r, tn), lambda l, j: (l, 0, j)),
        out_shape=jax.ShapeDtypeStruct((depth, r, n), F32),
        compiler_params=_cparams(("arbitrary", "arbitrary")),
    )(cvec, w_ada, b_ada.reshape(depth, 1, n))


CONV_HALO = 8
CONV_SLAB = 512


def _inproj_kernel(xp_ref, x_ref, xn_ref, sh_ref, sc_ref, wz_ref, wx_ref, wu_ref, wg_ref, wdt_ref, cw_ref, cb_ref,
                   z_ref, act_ref, u_ref, g_ref, dt_ref):
    j = pl.program_id(1)
    nj = pl.num_programs(1)
    tm = x_ref.shape[1]
    n = tm + 2 * CONV_HALO
    mod = lambda v: (v * (1.0 + sc_ref[0]) + sh_ref[0]).astype(BF16)
    h = mod(x_ref[0])
    h_ext = mod(jnp.concatenate([xp_ref[0], x_ref[0], xn_ref[0]], axis=0))
    z_ref[0] = _dot(h, wz_ref[...]).astype(BF16)
    u_ref[0] = _dot(h, wu_ref[...]).astype(BF16)
    g_ref[0] = _dot(h, wg_ref[...]).astype(BF16)
    dt_ref[0] = _dot(h, wdt_ref[...])

    row = lax.broadcasted_iota(I32, (n, 1), 0)
    inside = ((row >= CONV_HALO) | (j > 0)) & ((row < CONV_HALO + tm) | (j < nj - 1))
    down = lambda v: pltpu.roll(v, 1, 0)
    up = lambda v: pltpu.roll(v, n - 1, 0)
    for s in range(D_XBC // CONV_SLAB):
        cols = slice(s * CONV_SLAB, (s + 1) * CONV_SLAB)
        e = jnp.where(inside, _dot(h_ext, wx_ref[:, cols]), 0.0)
        y = [e * cw_ref[k:k + 1, cols] for k in range(CONV_K)]
        acc = y[2] + down(y[1] + down(y[0])) + up(y[3] + up(y[4]))
        act_ref[0, :, cols] = _silu(acc[CONV_HALO:CONV_HALO + tm] + cb_ref[:, cols]).astype(BF16)


def _in_proj(x, shift, scale, wts, conv_w, conv_b):
    b, l, d = x.shape
    tm = min(512, l)
    hpt = tm // CONV_HALO
    nh = l // CONV_HALO
    wz, wx, wu, wg, wdt = wts
    tok = lambda n: pl.BlockSpec((1, tm, n), lambda i, j: (i, j, 0))
    mod = pl.BlockSpec((1, 1, d), lambda i, j: (i, 0, 0))
    wspec = lambda w: pl.BlockSpec(w.shape, lambda i, j: (0, 0), pipeline_mode=pl.Buffered(1))
    par = lambda a: pl.BlockSpec(a.shape, lambda i, j: (0, 0))
    halo_prev = pl.BlockSpec((1, CONV_HALO, d), lambda i, j: (i, jnp.maximum(j * hpt - 1, 0), 0))
    halo_next = pl.BlockSpec((1, CONV_HALO, d), lambda i, j: (i, jnp.minimum((j + 1) * hpt, nh - 1), 0))
    return pl.pallas_call(
        _inproj_kernel,
        grid=(b, l // tm),
        in_specs=[halo_prev, tok(d), halo_next, mod, mod, wspec(wz), wspec(wx), wspec(wu), wspec(wg), wspec(wdt),
                  par(conv_w), par(conv_b)],
        out_specs=[tok(D_INNER), tok(D_XBC), tok(D_MODEL), tok(2 * D_MODEL), tok(LANES)],
        out_shape=[jax.ShapeDtypeStruct((b, l, D_INNER), BF16),
                   jax.ShapeDtypeStruct((b, l, D_XBC), BF16),
                   jax.ShapeDtypeStruct((b, l, D_MODEL), BF16),
                   jax.ShapeDtypeStruct((b, l, 2 * D_MODEL), BF16),
                   jax.ShapeDtypeStruct((b, l, LANES), F32)],
        compiler_params=_cparams(("arbitrary", "arbitrary")),
    )(x, x, x, shift, scale, wz, wx, wu, wg, wdt, conv_w, conv_b)


def _dt_prep(dt_raw, dtb_ref, alog_ref):
    q = CHUNK
    lane = lax.broadcasted_iota(I32, (q, LANES), 1)
    raw = dt_raw + dtb_ref[...]
    dt = jnp.maximum(raw, 0.0) + jnp.log1p(jnp.exp(-jnp.abs(raw)))
    dt = jnp.where(lane < 2 * HEADS, dt, 0.0)
    a = dt * (-jnp.exp(alog_ref[...]))
    ii = lax.broadcasted_iota(I32, (q, q), 0)
    jj = lax.broadcasted_iota(I32, (q, q), 1)
    tri = (jj <= ii).astype(BF16)
    a_hi, a_lo = _split_bf16(a)
    acs = _dot(tri, a_hi) + _dot(tri, a_lo)
    tot = acs[q - 1:q, :]
    return dt, a, acs, tot


def _expand(v, lane0, r_ref):
    lane = lax.broadcasted_iota(I32, v.shape, 1)
    vm = jnp.where((lane >= lane0) & (lane < lane0 + HEADS), v, 0.0)
    hi = vm.astype(BF16).astype(F32)
    both = hi + pltpu.roll(vm - hi, 2 * HEADS, 1)
    return _dot(both.astype(BF16), r_ref[...])


def _state_update(st_ref, bm, xw_bf, decay_x):
    gw = D_INNER // GROUPS
    for g in range(GROUPS):
        bgt = bm[:, g * STATE:(g + 1) * STATE].T.astype(BF16)
        cols = slice(g * gw, (g + 1) * gw)
        st_ref[:, cols] = st_ref[:, cols] * decay_x[:, cols] + _dot(bgt, xw_bf[:, cols])


SCAN_CHUNKS = 4


def _ssd_state_kernel(*refs, rev, emit, nch):
    act_ref, dt_ref, h0_ref, dtb_ref, alog_ref, r_ref = refs[:6]
    if emit:
        stout_ref, sfin_ref, st_ref = refs[6:]
    else:
        sfin_ref, st_ref = refs[6:]
    s = pl.program_id(1)
    ns = pl.num_programs(1)

    @pl.when(s == 0)
    def _():
        st_ref[...] = h0_ref[0]

    for k in range(nch):
        cc = (nch - 1 - k) if rev else k
        rows = slice(cc * CHUNK, (cc + 1) * CHUNK)
        if emit:
            stout_ref[0, cc] = st_ref[...].astype(BF16)
        xs = act_ref[0, rows, :D_INNER].astype(F32)
        bm = act_ref[0, rows, D_INNER:D_INNER + D_BC].astype(F32)
        dt, a, acs, tot = _dt_prep(dt_ref[0, rows, :], dtb_ref, alog_ref)
        if rev:
            w = jnp.exp(acs - a) * dt
            lane0 = HEADS
        else:
            w = jnp.exp(tot - acs) * dt
            lane0 = 0
        w_x = _expand(w, lane0, r_ref)
        dec_x = _expand(jnp.broadcast_to(jnp.exp(tot), (16, LANES)), lane0, r_ref)[0:1]
        _state_update(st_ref, bm, (w_x * xs).astype(BF16), dec_x)

    @pl.when(s == ns - 1)
    def _():
        sfin_ref[0] = st_ref[...]


def _ssd_state_scan(act, dt, h0, lp, r_mat, *, rev, emit):
    b, l, _ = act.shape
    nc = l // CHUNK
    nch = min(SCAN_CHUNKS, nc)
    ns = nc // nch
    seq = (lambda s: ns - 1 - s) if rev else (lambda s: s)
    par = lambda shape: pl.BlockSpec(shape, lambda i, s: (0,) * len(shape))
    in_specs = [
        pl.BlockSpec((1, nch * CHUNK, D_XBC), lambda i, s: (i, seq(s), 0)),
        pl.BlockSpec((1, nch * CHUNK, LANES), lambda i, s: (i, seq(s), 0)),
        pl.BlockSpec((1, STATE, D_INNER), lambda i, s: (i, 0, 0)),
        par((1, LANES)), par((1, LANES)), par((LANES, D_INNER)),
    ]
    out_specs = [pl.BlockSpec((1, STATE, D_INNER), lambda i, s: (i, 0, 0))]
    out_shape = [jax.ShapeDtypeStruct((b, STATE, D_INNER), F32)]
    if emit:
        out_specs = [pl.BlockSpec((1, nch, STATE, D_INNER), lambda i, s: (i, seq(s), 0, 0))] + out_specs
        out_shape = [jax.ShapeDtypeStruct((b, nc, STATE, D_INNER), BF16)] + out_shape
    return pl.pallas_call(
        functools.partial(_ssd_state_kernel, rev=rev, emit=emit, nch=nch),
        grid=(b, ns),
        in_specs=in_specs, out_specs=out_specs, out_shape=out_shape,
        scratch_shapes=[pltpu.VMEM((STATE, D_INNER), F32)],
        compiler_params=_cparams(("arbitrary", "arbitrary")),
    )(act, dt, h0, lp["dt_bias"], lp["a_log"], r_mat)


def _ssd_main_kernel(act_ref, dt_ref, z_ref, stb_ref, h0_ref,
                     dtb_ref, alog_ref, dsk_ref, ng_ref, rf_ref, rb_ref,
                     y_ref, sfin_ref, st_ref, ys_ref):
    q = CHUNK
    c = pl.program_id(1)
    nc = pl.num_programs(1)

    @pl.when(c == 0)
    def _():
        st_ref[...] = h0_ref[0]

    xs_bf = act_ref[0, :, :D_INNER]
    xs = xs_bf.astype(F32)
    dt, a, acs, tot = _dt_prep(dt_ref[0], dtb_ref, alog_ref)
    eb = acs - a

    ef_x = _expand(jnp.exp(acs), 0, rf_ref)
    eb_x = _expand(jnp.exp(tot - eb), HEADS, rb_ref)
    tf_x = _expand(jnp.exp(tot - acs) * dt, 0, rf_ref)
    dec_x = _expand(jnp.broadcast_to(jnp.exp(tot), (16, LANES)), 0, rf_ref)[0:1]

    lane = lax.broadcasted_iota(I32, (q, LANES), 1)
    ldt = jnp.log(dt)
    qm = jnp.where(lane < HEADS, acs - ldt, jnp.where(lane < 2 * HEADS, eb + ldt, pltpu.roll(dt, 2 * HEADS, 1)))
    qt = qm.T
    ii = lax.broadcasted_iota(I32, (q, q), 0)
    jj = lax.broadcasted_iota(I32, (q, q), 1)
    lower = jj <= ii
    diag = jj == ii
    lane2 = lax.broadcasted_iota(I32, (q, LANES), 1)
    left = lane2 < HEADDIM

    stf_bf = st_ref[...].astype(BF16)
    gw = D_INNER // GROUPS
    hpg = HEADS // GROUPS
    for g in range(GROUPS):
        bg = act_ref[0, :, D_INNER + g * STATE:D_INNER + (g + 1) * STATE]
        cg = act_ref[0, :, D_INNER + D_BC + g * STATE:D_INNER + D_BC + (g + 1) * STATE]
        cb = lax.dot_general(cg, bg, (((1,), (1,)), ((), ())), preferred_element_type=F32)
        cols = slice(g * gw, (g + 1) * gw)
        y_off = ef_x[:, cols] * _dot(cg, stf_bf[:, cols]) + eb_x[:, cols] * _dot(cg, stb_ref[0, 0][:, cols])
        for pr in range(hpg // 2):
            ms = []
            for hh in range(2):
                h = g * hpg + pr * 2 + hh
                afc = acs[:, h:h + 1]
                afr = qt[h:h + 1, :]
                ebc = eb[:, HEADS + h:HEADS + h + 1]
                ebr = qt[HEADS + h:HEADS + h + 1, :]
                wb = qt[3 * HEADS + h:3 * HEADS + h + 1, :]
                arg = jnp.where(lower, afc - afr, ebr - ebc)
                ms.append((cb * (jnp.exp(arg) + jnp.where(diag, wb, 0.0))).astype(BF16))
            lhs = jnp.concatenate(ms, axis=1)
            c0 = g * gw + pr * 2 * HEADDIM
            xp = xs_bf[:, c0:c0 + 2 * HEADDIM]
            zero = jnp.zeros_like(xp)
            rhs = jnp.concatenate([jnp.where(left, xp, zero), jnp.where(left, zero, xp)], axis=0)
            y_pair = _dot(lhs, rhs) + y_off[:, pr * 2 * HEADDIM:(pr + 1) * 2 * HEADDIM]
            y_pair = y_pair + dsk_ref[:, c0:c0 + 2 * HEADDIM] * xs[:, c0:c0 + 2 * HEADDIM]
            ys_ref[:, c0:c0 + 2 * HEADDIM] = y_pair

    y = ys_ref[...]
    yz = y * _silu(z_ref[0].astype(F32))
    ms2 = jnp.mean(yz * yz, axis=-1, keepdims=True)
    y_ref[0] = (yz * lax.rsqrt(ms2 + NORM_EPS) * ng_ref[...]).astype(BF16)

    bm = act_ref[0, :, D_INNER:D_INNER + D_BC].astype(F32)
    _state_update(st_ref, bm, (tf_x * xs).astype(BF16), dec_x)

    @pl.when(c == nc - 1)
    def _():
        sfin_ref[0] = st_ref[...]


def _ssd_main(act, dt, z, stb, h0, lp, r_f, r_b):
    b, l, _ = act.shape
    nc = l // CHUNK
    par = lambda shape: pl.BlockSpec(shape, lambda i, s: (0,) * len(shape))
    in_specs = [
        pl.BlockSpec((1, CHUNK, D_XBC), lambda i, s: (i, s, 0)),
        pl.BlockSpec((1, CHUNK, LANES), lambda i, s: (i, s, 0)),
        pl.BlockSpec((1, CHUNK, D_INNER), lambda i, s: (i, s, 0)),
        pl.BlockSpec((1, 1, STATE, D_INNER), lambda i, s: (i, s, 0, 0)),
        pl.BlockSpec((1, STATE, D_INNER), lambda i, s: (i, 0, 0)),
        par((1, LANES)), par((1, LANES)),
        par((1, D_INNER)), par((1, D_INNER)), par((LANES, D_INNER)), par((LANES, D_INNER)),
    ]
    return pl.pallas_call(
        _ssd_main_kernel,
        grid=(b, nc),
        in_specs=in_specs,
        out_specs=[pl.BlockSpec((1, CHUNK, D_INNER), lambda i, s: (i, s, 0)),
                   pl.BlockSpec((1, STATE, D_INNER), lambda i, s: (i, 0, 0))],
        out_shape=[jax.ShapeDtypeStruct((b, l, D_INNER), BF16),
                   jax.ShapeDtypeStruct((b, STATE, D_INNER), F32)],
        scratch_shapes=[pltpu.VMEM((STATE, D_INNER), F32), pltpu.VMEM((CHUNK, D_INNER), F32)],
        compiler_params=_cparams(("arbitrary", "arbitrary")),
    )(act, dt, z, stb, h0, lp["dt_bias"], lp["a_log"], lp["d_skip_x"], lp["ssd_norm_g"], r_f, r_b)


def _pool_kernel(*refs, width, vertical, halo, hblock):
    if vertical:
        prev_ref, cur_ref, next_ref, pw_ref, ps_ref, o_ref = refs
    else:
        cur_ref, pw_ref, ps_ref, o_ref = refs
    t = pl.program_id(1)
    nt = pl.num_programs(1)
    tp = cur_ref.shape[1]
    n_tok = nt * tp
    shift = int(math.log2(width))
    idx = lax.broadcasted_iota(I32, (tp, 1), 0) + t * tp
    colpos = idx & (width - 1)
    rowpos = idx >> shift
    n_rows = n_tok // width
    bi = lax.broadcasted_iota(I32, (hblock, hblock), 0)
    bj = lax.broadcasted_iota(I32, (hblock, hblock), 1)
    same_row = (bi >> shift) == (bj >> shift)
    cur = cur_ref[0].astype(F32)
    if vertical:
        prev = jnp.where(t > 0, prev_ref[0].astype(F32), 0.0)
        nxt = jnp.where(t < nt - 1, next_ref[0].astype(F32), 0.0)
    for gi, k in enumerate(POOL_WINDOWS):
        cols = slice(gi * POOL_GROUP_DIM, (gi + 1) * POOL_GROUP_DIM)
        ug = cur[:, cols]
        lo = k // 2
        if vertical:
            e = jnp.concatenate([prev[:, cols], ug, nxt[:, cols]], axis=0)
            step = width
            m = 1
            while m < k:
                n = e.shape[0] - step
                e = e[0:n] + e[step:step + n]
                step *= 2
                m *= 2
            start = halo - lo * width
            s = e[start:start + tp]
            cnt_r = (jnp.minimum(rowpos - lo + k, n_rows) - jnp.maximum(rowpos - lo, 0)).astype(F32)
        else:
            s = ug
            cnt_r = jnp.ones((tp, 1), F32)
        band = (same_row & (bj - bi >= -lo) & (bj - bi < k - lo)).astype(BF16)
        s_bf = s.astype(BF16)
        hs = [_dot(band, s_bf[r * hblock:(r + 1) * hblock]) for r in range(tp // hblock)]
        hsum = hs[0] if len(hs) == 1 else jnp.concatenate(hs, axis=0)
        cnt_c = (jnp.minimum(colpos - lo + k, width) - jnp.maximum(colpos - lo, 0)).astype(F32)
        mean = hsum / (cnt_r * cnt_c)
        y = _dot((mean - ug).astype(BF16), pw_ref[gi])
        o_ref[0, :, cols] = (y * ps_ref[:, cols]).astype(BF16)


def _pool_branch(u, lp, on_grid):
    b, l, d = u.shape
    pw, ps = lp["pool_w"], lp["pool_scale"]
    if on_grid:
        width, halo = GRID_W, (max(POOL_WINDOWS) // 2) * GRID_W
        tp = min(1024, l)
        hb = tp // halo
        nhb = l // halo
        in_specs = [pl.BlockSpec((1, halo, d), lambda i, t: (i, jnp.maximum(t * hb - 1, 0), 0)),
                    pl.BlockSpec((1, tp, d), lambda i, t: (i, t, 0)),
                    pl.BlockSpec((1, halo, d), lambda i, t: (i, jnp.minimum((t + 1) * hb, nhb - 1), 0))]
        args = (u, u, u)
        kern = functools.partial(_pool_kernel, width=width, vertical=True, halo=halo, hblock=LANES)
    else:
        tp = l
        in_specs = [pl.BlockSpec((1, tp, d), lambda i, t: (i, t, 0))]
        args = (u,)
        kern = functools.partial(_pool_kernel, width=l, vertical=False, halo=0, hblock=l)
    in_specs += [pl.BlockSpec(pw.shape, lambda i, t: (0, 0, 0)), pl.BlockSpec(ps.shape, lambda i, t: (0, 0))]
    return pl.pallas_call(
        kern,
        grid=(b, l // tp),
        in_specs=in_specs,
        out_specs=pl.BlockSpec((1, tp, d), lambda i, t: (i, t, 0)),
        out_shape=jax.ShapeDtypeStruct((b, l, d), BF16),
        compiler_params=_cparams(("arbitrary", "arbitrary")),
    )(*args, pw, ps)


def _merge_kernel(ya_ref, yp_ref, gt_ref, x_ref, g1_ref, sh2_ref, sc2_ref, bg_ref,
                  wa_ref, wb_ref, wo_ref, lng_ref, lnb_ref, wrh_ref, wrl_ref, br_ref, cin_ref,
                  xo_ref, h2_ref, rt_ref, cout_ref, cnt_ref, *, alpha):
    i = pl.program_id(0)
    j = pl.program_id(1)
    tm = x_ref.shape[1]

    @pl.when((i == 0) & (j == 0))
    def _():
        cnt_ref[...] = cin_ref[...]

    gate = jax.nn.sigmoid(gt_ref[0].astype(F32) + bg_ref[...])
    merged = (gate[:, :D_MODEL] * _dot(ya_ref[0], wa_ref[...])
              + gate[:, D_MODEL:] * _dot(yp_ref[0], wb_ref[...]))
    out = _dot(merged.astype(BF16), wo_ref[...])
    xn = _layer_norm(alpha * x_ref[0] + g1_ref[0] * out, lng_ref[...], lnb_ref[...])
    xo_ref[0] = xn
    h2 = xn * (1.0 + sc2_ref[0]) + sh2_ref[0]
    h2_ref[0] = _pack_rows(h2)

    h_hi, h_lo = _split_bf16(h2)
    logits = _dot(h_hi, wrh_ref[...]) + _dot(h_lo, wrh_ref[...]) + _dot(h_hi, wrl_ref[...]) + br_ref[...]
    lane = lax.broadcasted_iota(I32, (tm, LANES), 1)
    neg = jnp.float32(-jnp.inf)
    lg = jnp.where(lane < MOE_GROUPS, logits, neg)
    mg = jnp.max(lg, axis=-1, keepdims=True)
    grp = jnp.min(jnp.where(lg == mg, lane, LANES), axis=-1, keepdims=True)
    p_grp = 1.0 / jnp.sum(jnp.exp(lg - mg), axis=-1, keepdims=True)
    lo_lane = ROUTE_LANE0 + grp * MOE_EPG
    le = jnp.where((lane >= lo_lane) & (lane < lo_lane + MOE_EPG), logits, neg)
    v1 = jnp.max(le, axis=-1, keepdims=True)
    i1 = jnp.min(jnp.where(le == v1, lane, LANES), axis=-1, keepdims=True)
    le2 = jnp.where(lane == i1, neg, le)
    v2 = jnp.max(le2, axis=-1, keepdims=True)
    i2 = jnp.min(jnp.where(le2 == v2, lane, LANES), axis=-1, keepdims=True)
    e2 = jnp.exp(v2 - v1)
    w1 = p_grp / (1.0 + e2)
    w2 = p_grp * e2 / (1.0 + e2)

    oh1 = (lane == i1).astype(F32)
    oh2 = (lane == i2).astype(F32)
    oh = oh1 + oh2
    ri = lax.broadcasted_iota(I32, (tm, tm), 0)
    rj = lax.broadcasted_iota(I32, (tm, tm), 1)
    before = _dot((rj < ri).astype(BF16), oh.astype(BF16)) + cnt_ref[...]
    rank1 = jnp.sum(oh1 * before, axis=-1, keepdims=True)
    rank2 = jnp.sum(oh2 * before, axis=-1, keepdims=True)
    cnt_ref[...] = cnt_ref[...] + jnp.sum(oh, axis=0, keepdims=True)
    cout_ref[...] = cnt_ref[...]

    slab = jnp.where(lane == 0, (i1 - ROUTE_LANE0).astype(F32),
           jnp.where(lane == 1, (i2 - ROUTE_LANE0).astype(F32),
           jnp.where(lane == 2, w1,
           jnp.where(lane == 3, w2,
           jnp.where(lane == 4, rank1,
           jnp.where(lane == 5, rank2, 0.0))))))
    rt_ref[0] = slab


def _merge(ya, yp, gates, x, g1, sh2, sc2, lp, cnt_in, alpha):
    b, l, d = x.shape
    tm = min(512, l)
    tok = lambda n: pl.BlockSpec((1, tm, n), lambda i, j: (i, j, 0))
    mod = pl.BlockSpec((1, 1, d), lambda i, j: (i, 0, 0))
    par = lambda a: pl.BlockSpec(a.shape, lambda i, j: (0,) * a.ndim)
    params = (lp["b_gate"], lp["w_branch_a"], lp["w_branch_b"], lp["w_out"], lp["ln1_g"], lp["ln1_b"],
              lp["w_r_hi"], lp["w_r_lo"], lp["b_r"], cnt_in)
    return pl.pallas_call(
        functools.partial(_merge_kernel, alpha=alpha),
        grid=(b, l // tm),
        in_specs=[tok(D_INNER), tok(d), tok(2 * d), tok(d), mod, mod, mod] + [par(a) for a in params],
        out_specs=[tok(d), tok(d // 2), tok(LANES), pl.BlockSpec((1, LANES), lambda i, j: (0, 0))],
        out_shape=[jax.ShapeDtypeStruct((b, l, d), F32), jax.ShapeDtypeStruct((b, l, d // 2), U32),
                   jax.ShapeDtypeStruct((b, l, LANES), F32), jax.ShapeDtypeStruct((1, LANES), F32)],
        scratch_shapes=[pltpu.VMEM((1, LANES), F32)],
        compiler_params=_cparams(("arbitrary", "arbitrary")),
    )(ya, yp, gates, x, g1, sh2, sc2, *params)


ROW_TILE = 512
DMA_UNROLL = 8


def _row_copy(src_ref, src_row, dst_ref, dst_row, sem):
    return pltpu.make_async_copy(src_ref.at[pl.ds(src_row, 1)], dst_ref.at[pl.ds(dst_row, 1)], sem)


def _dispatch_kernel(h_ref, dest_ref, xin_any, xout_any, idx_smem, sem_idx, sem_row):
    del xin_any
    td = h_ref.shape[0]
    cp = pltpu.make_async_copy(dest_ref.at[0, 0], idx_smem, sem_idx)
    cp.start()
    cp.wait()

    def issue(r, carry):
        _row_copy(h_ref, r, xout_any, idx_smem[2 * r], sem_row).start(priority=0)
        _row_copy(h_ref, r, xout_any, idx_smem[2 * r + 1], sem_row).start(priority=1)
        return carry

    lax.fori_loop(0, td, issue, 0, unroll=DMA_UNROLL)

    def drain(r, carry):
        _row_copy(h_ref, 0, xout_any, 0, sem_row).wait()
        _row_copy(h_ref, 0, xout_any, 0, sem_row).wait()
        return carry

    lax.fori_loop(0, td, drain, 0, unroll=DMA_UNROLL)


def _dispatch(h2, dest, xin):
    t, d = h2.shape
    td = min(ROW_TILE, t)
    dest3 = dest.reshape(t // td, 1, 2 * td)
    return pl.pallas_call(
        _dispatch_kernel,
        grid=(t // td,),
        in_specs=[pl.BlockSpec((td, d), lambda i: (i, 0)),
                  pl.BlockSpec((1, 1, 2 * td), lambda i: (i, 0, 0)),
                  pl.BlockSpec(memory_space=pl.ANY)],
        out_specs=pl.BlockSpec(memory_space=pl.ANY),
        out_shape=jax.ShapeDtypeStruct(xin.shape, xin.dtype),
        scratch_shapes=[pltpu.SMEM((2 * td,), I32), pltpu.SemaphoreType.DMA(()), pltpu.SemaphoreType.DMA(())],
        input_output_aliases={2: 0},
        compiler_params=_cparams(("arbitrary",)),
    )(h2, dest3, xin)


def _expert_kernel(be_ref, nu_ref, x_ref, wg_ref, wu_ref, wd_ref, o_ref, wgb_ref, wub_ref, wdb_ref):
    i = pl.program_id(0)

    @pl.when((i == 0) | (be_ref[i] != be_ref[jnp.maximum(i - 1, 0)]))
    def _():
        wgb_ref[...] = wg_ref[0, 0].astype(BF16)
        wub_ref[...] = wu_ref[0, 0].astype(BF16)
        wdb_ref[...] = wd_ref[0, 0].astype(BF16)

    @pl.when(i < nu_ref[0])
    def _():
        xb = _unpack_rows(x_ref[...]).astype(BF16)
        gte = _dot(xb, wgb_ref[...])
        up = _dot(xb, wub_ref[...])
        o_ref[...] = _pack_rows(_dot((_silu(gte) * up).astype(BF16), wdb_ref[...]))

    @pl.when(i >= nu_ref[0])
    def _():
        o_ref[...] = jnp.zeros_like(o_ref)


def _experts(xin, block_e, n_used, w_eg, w_eu, w_ed, layer):
    n_rows, dh = xin.shape
    d = 2 * dh
    nb = n_rows // MOE_BLOCK
    grid_spec = pltpu.PrefetchScalarGridSpec(
        num_scalar_prefetch=2,
        grid=(nb,),
        in_specs=[pl.BlockSpec((MOE_BLOCK, dh), lambda i, be, nu: (i, 0)),
                  pl.BlockSpec((1, 1, d, D_EXPERT), lambda i, be, nu: (layer, be[i], 0, 0)),
                  pl.BlockSpec((1, 1, d, D_EXPERT), lambda i, be, nu: (layer, be[i], 0, 0)),
                  pl.BlockSpec((1, 1, D_EXPERT, d), lambda i, be, nu: (layer, be[i], 0, 0))],
        out_specs=pl.BlockSpec((MOE_BLOCK, dh), lambda i, be, nu: (i, 0)),
        scratch_shapes=[pltpu.VMEM((d, D_EXPERT), BF16), pltpu.VMEM((d, D_EXPERT), BF16),
                        pltpu.VMEM((D_EXPERT, d), BF16)],
    )
    return pl.pallas_call(
        _expert_kernel,
        grid_spec=grid_spec,
        out_shape=jax.ShapeDtypeStruct((n_rows, dh), U32),
        compiler_params=_cparams(("arbitrary",)),
    )(block_e, n_used, xin, w_eg, w_eu, w_ed)


def _combine_kernel(dest_ref, rt_ref, x_ref, g2_ref, lng_ref, lnb_ref, yb_any,
                    o_ref, buf_ref, idx_smem, sem_idx, sem_row, *, alpha):
    tc = x_ref.shape[1]
    cp = pltpu.make_async_copy(dest_ref.at[0, 0], idx_smem, sem_idx)
    cp.start()
    cp.wait()

    def issue(r, carry):
        _row_copy(yb_any, idx_smem[2 * r], buf_ref.at[0], r, sem_row).start(priority=0)
        _row_copy(yb_any, idx_smem[2 * r + 1], buf_ref.at[1], r, sem_row).start(priority=1)
        return carry

    lax.fori_loop(0, tc, issue, 0, unroll=DMA_UNROLL)

    def drain(r, carry):
        _row_copy(yb_any, 0, buf_ref.at[0], 0, sem_row).wait()
        _row_copy(yb_any, 0, buf_ref.at[1], 0, sem_row).wait()
        return carry

    lax.fori_loop(0, tc, drain, 0, unroll=DMA_UNROLL)

    rt = rt_ref[0]
    y = rt[:, 2:3] * _unpack_rows(buf_ref[0]) + rt[:, 3:4] * _unpack_rows(buf_ref[1])
    o_ref[0] = _layer_norm(alpha * x_ref[0] + g2_ref[0] * y, lng_ref[...], lnb_ref[...])


def _combine(dest, route, x, g2, lp, yb, alpha):
    b, l, d = x.shape
    tc = min(ROW_TILE, l)
    npb = l // tc
    dest3 = dest.reshape(b * npb, 1, 2 * tc)
    par = lambda a: pl.BlockSpec(a.shape, lambda i, j: (0,) * a.ndim)
    return pl.pallas_call(
        functools.partial(_combine_kernel, alpha=alpha),
        grid=(b, npb),
        in_specs=[pl.BlockSpec((1, 1, 2 * tc), lambda i, j: (i * npb + j, 0, 0)),
                  pl.BlockSpec((1, tc, LANES), lambda i, j: (i, j, 0)),
                  pl.BlockSpec((1, tc, d), lambda i, j: (i, j, 0)),
                  pl.BlockSpec((1, 1, d), lambda i, j: (i, 0, 0)),
                  par(lp["ln2_g"]), par(lp["ln2_b"]),
                  pl.BlockSpec(memory_space=pl.ANY)],
        out_specs=pl.BlockSpec((1, tc, d), lambda i, j: (i, j, 0)),
        out_shape=jax.ShapeDtypeStruct((b, l, d), F32),
        scratch_shapes=[pltpu.VMEM((2, tc, d // 2), U32), pltpu.SMEM((2 * tc,), I32),
                        pltpu.SemaphoreType.DMA(()), pltpu.SemaphoreType.DMA(())],
        compiler_params=_cparams(("arbitrary", "arbitrary")),
    )(dest3, route, x, g2, lp["ln2_g"], lp["ln2_b"], yb)


def _moe_plan(routes, counts):
    cnt = counts[0, ROUTE_LANE0:ROUTE_LANE0 + MOE_EXPERTS].astype(I32)
    padded = (cnt + MOE_BLOCK - 1) // MOE_BLOCK * MOE_BLOCK
    pad_end = jnp.cumsum(padded)
    pad_start = pad_end - padded
    n_assign = sum(r.shape[0] * r.shape[1] for r in routes) * 2
    n_blocks = -(-n_assign // MOE_BLOCK) + MOE_EXPERTS
    first_row = jnp.arange(n_blocks, dtype=I32)[:, None] * MOE_BLOCK
    block_e = jnp.minimum(jnp.sum((pad_end[None, :] <= first_row).astype(I32), axis=1), MOE_EXPERTS - 1)
    n_used = (pad_end[-1:] // MOE_BLOCK).astype(I32)
    eidx = jnp.arange(MOE_EXPERTS, dtype=I32)
    dests = []
    for r in routes:
        e = r[..., 0:2].astype(I32)
        rank = r[..., 4:6].astype(I32)
        start = jnp.sum(jnp.where(e[..., None] == eidx, pad_start, 0), axis=-1)
        dests.append((start + rank).reshape(-1, 2))
    return dests, block_e, n_used, n_blocks * MOE_BLOCK


def _layer_params(i, p):
    w_in = p["w_in"][i]
    s0, s1, s2, s3 = D_INNER, D_INNER + D_XBC, D_INNER + D_XBC + 2 * HEADS, D_INNER + D_XBC + 2 * HEADS + D_MODEL
    pad = LANES - 2 * HEADS
    wdt = jnp.pad(w_in[:, s1:s2], ((0, 0), (0, pad)))
    w_r = jnp.concatenate([p["w_router_group"][i],
                           jnp.transpose(p["w_router_expert"][i], (1, 0, 2)).reshape(D_MODEL, MOE_EXPERTS)], axis=1)
    w_r = jnp.pad(w_r, ((0, 0), (0, LANES - w_r.shape[1])))
    w_r_hi = w_r.astype(BF16)
    b_r = jnp.concatenate([p["b_router_group"][i], p["b_router_expert"][i].reshape(-1)])
    return {
        "w_in": tuple(w.astype(BF16) for w in (w_in[:, :s0], w_in[:, s0:s1], w_in[:, s2:s3], w_in[:, s3:], wdt)),
        "conv_w": p["conv_w"][i], "conv_b": p["conv_b"][i][None],
        "dt_bias": jnp.pad(p["dt_bias"][i].reshape(1, -1), ((0, 0), (0, pad))),
        "a_log": jnp.pad(p["a_log"][i].reshape(1, -1), ((0, 0), (0, pad))),
        "d_skip_x": jnp.repeat(p["d_skip"][i], HEADDIM)[None],
        "ssd_norm_g": p["ssd_norm_g"][i][None],
        "pool_w": p["pool_w"][i].astype(BF16), "pool_scale": p["pool_scale"][i][None],
        "b_gate": p["b_gate"][i][None],
        "w_branch_a": p["w_branch_a"][i].astype(BF16), "w_branch_b": p["w_branch_b"][i].astype(BF16),
        "w_out": p["w_out"][i].astype(BF16),
        "ln1_g": p["ln1_g"][i][None], "ln1_b": p["ln1_b"][i][None],
        "ln2_g": p["ln2_g"][i][None], "ln2_b": p["ln2_b"][i][None],
        "w_r_hi": w_r_hi, "w_r_lo": (w_r - w_r_hi.astype(F32)).astype(BF16),
        "b_r": jnp.pad(b_r, (0, LANES - b_r.shape[0]))[None],
    }


def _head_expanders():
    col_head = jnp.arange(D_INNER, dtype=I32) // HEADDIM
    row = jnp.arange(LANES, dtype=I32)[:, None] % (2 * HEADS)
    return (row == col_head).astype(BF16), (row == col_head + HEADS).astype(BF16)


def kernel(x, c, ctx, c_ctx, w_ada, b_ada, w_in, b_gate, conv_w, conv_b, dt_bias, a_log, d_skip, ssd_norm_g, pool_w, pool_scale, w_branch_a, w_branch_b, w_out, ln1_g, ln1_b, ln2_g, ln2_b, w_router_group, b_router_group, w_router_expert, b_router_expert, w_expert_gate, w_expert_up, w_expert_down):
    p = dict(w_in=w_in, b_gate=b_gate, conv_w=conv_w, conv_b=conv_b, dt_bias=dt_bias, a_log=a_log, d_skip=d_skip,
             ssd_norm_g=ssd_norm_g, pool_w=pool_w, pool_scale=pool_scale, w_branch_a=w_branch_a,
             w_branch_b=w_branch_b, w_out=w_out, ln1_g=ln1_g, ln1_b=ln1_b, ln2_g=ln2_g, ln2_b=ln2_b,
             w_router_group=w_router_group, b_router_group=b_router_group, w_router_expert=w_router_expert,
             b_router_expert=b_router_expert, w_expert_gate=w_expert_gate, w_expert_up=w_expert_up,
             w_expert_down=w_expert_down)
    b, l, d = x.shape
    depth = w_ada.shape[0]
    alpha = (2.0 * depth) ** DEPTH_ALPHA_POW
    n_mod = -(-(b + 1) // 8) * 8
    cvec = jnp.zeros((n_mod, d), F32).at[:b].set(c).at[b].set(c_ctx)
    mods = _ada_mods(cvec, w_ada, b_ada)
    r_f, r_b = _head_expanders()
    zero_state = jnp.zeros((b, STATE, D_INNER), F32)
    zero_cnt = jnp.zeros((1, LANES), F32)
    xl, xc = x, ctx
    for i in range(depth):
        last = i == depth - 1
        lp = _layer_params(i, p)
        mod_l = [mods[i, :b, k * d:(k + 1) * d][:, None, :] for k in range(6)]
        mod_c = [jnp.broadcast_to(mods[i, b, k * d:(k + 1) * d], (b, 1, d)) for k in range(6)]

        zc, act_c, uc, gc, dtc = _in_proj(xc, mod_c[0], mod_c[1], lp["w_in"], lp["conv_w"], lp["conv_b"])
        if last:
            (s_f,) = _ssd_state_scan(act_c, dtc, zero_state, lp, r_f, rev=False, emit=False)
            (s_b,) = _ssd_state_scan(act_c, dtc, zero_state, lp, r_b, rev=True, emit=False)
        else:
            stb_c, s_b = _ssd_state_scan(act_c, dtc, zero_state, lp, r_b, rev=True, emit=True)
            ya_c, s_f = _ssd_main(act_c, dtc, zc, stb_c, zero_state, lp, r_f, r_b)
            yp_c = _pool_branch(uc, lp, False)

        zl, act_l, ul, gl, dtl = _in_proj(xl, mod_l[0], mod_l[1], lp["w_in"], lp["conv_w"], lp["conv_b"])
        stb_l, _ = _ssd_state_scan(act_l, dtl, s_b, lp, r_b, rev=True, emit=True)
        ya_l, _ = _ssd_main(act_l, dtl, zl, stb_l, s_f, lp, r_f, r_b)
        yp_l = _pool_branch(ul, lp, True)
        xl, h2_l, rt_l, cnt = _merge(ya_l, yp_l, gl, xl, mod_l[2], mod_l[3], mod_l[4], lp, zero_cnt, alpha)
        streams = [(h2_l, rt_l)]
        if not last:
            xc, h2_c, rt_c, cnt = _merge(ya_c, yp_c, gc, xc, mod_c[2], mod_c[3], mod_c[4], lp, cnt, alpha)
            streams.append((h2_c, rt_c))

        dests, block_e, n_used, n_rows = _moe_plan([rt for _, rt in streams], cnt)
        xin = jnp.zeros((n_rows, d // 2), U32)
        for (h2, _), dest in zip(streams, dests):
            xin = _dispatch(h2.reshape(-1, d // 2), dest, xin)
        yb = _experts(xin, block_e, n_used, w_expert_gate, w_expert_up, w_expert_down, i)
        xl = _combine(dests[0], rt_l, xl, mod_l[5], lp, yb, alpha)
        if not last:
            xc = _combine(dests[1], rt_c, xc, mod_c[5], lp, yb, alpha)
    return xl
```

```python
import functools
import math

import jax
import jax.numpy as jnp
from jax import lax
from jax.experimental import pallas as pl
from jax.experimental.pallas import tpu as pltpu

F32 = jnp.float32
BF16 = jnp.bfloat16
I32 = jnp.int32

D_MODEL = 1024
D_INNER = 2048
HEADS = 32
HEADDIM = 64
GROUPS = 4
STATE = 128
D_BC = GROUPS * STATE
D_XBC = D_INNER + 2 * D_BC
CONV_K = 5
CHUNK = 128
GRID_W = 64
POOL_WINDOWS = (2, 4, 8, 16)
POOL_GROUP_DIM = 256
MOE_GROUPS = 4
MOE_EPG = 8
MOE_EXPERTS = 32
D_EXPERT = 512
MOE_BLOCK = 512
DEPTH_ALPHA_POW = 0.25
NORM_EPS = 1e-5
LANES = 128
ROUTE_LANE0 = MOE_GROUPS
VMEM_LIMIT = 56 * 1024 * 1024


def _cparams(sem, vmem=VMEM_LIMIT):
    return pltpu.CompilerParams(dimension_semantics=sem, vmem_limit_bytes=vmem)


def _silu(v):
    return v * jax.nn.sigmoid(v)


def _layer_norm(v, g, b):
    mu = jnp.mean(v, axis=-1, keepdims=True)
    d = v - mu
    var = jnp.mean(d * d, axis=-1, keepdims=True)
    return d * lax.rsqrt(var + NORM_EPS) * g + b


def _split_bf16(v):
    hi = v.astype(BF16)
    lo = (v - hi.astype(F32)).astype(BF16)
    return hi, lo


def _dot(a, b):
    return jnp.dot(a, b, preferred_element_type=F32)


U32 = jnp.uint32
HI16 = 0xFFFF0000


def _pack_rows(v):
    half = v.shape[1] // 2
    lo = lax.bitcast_convert_type(v[:, :half].astype(BF16).astype(F32), U32)
    hi = lax.bitcast_convert_type(v[:, half:].astype(BF16).astype(F32), U32)
    return (lo >> 16) | (hi & U32(HI16))


def _unpack_rows(w):
    lo = lax.bitcast_convert_type(w << 16, F32)
    hi = lax.bitcast_convert_type(w & U32(HI16), F32)
    return jnp.concatenate([lo, hi], axis=1)


def _ada_kernel(c_ref, w_ref, b_ref, o_ref):
    s = _silu(c_ref[...]).astype(BF16)
    o_ref[0] = _dot(s, w_ref[0].astype(BF16)) + b_ref[0]


def _ada_mods(cvec, w_ada, b_ada):
    depth, d, n = w_ada.shape
    r = cvec.shape[0]
    tn = D_MODEL
    return pl.pallas_call(
        _ada_kernel,
        grid=(depth, n // tn),
        in_specs=[pl.BlockSpec((r, d), lambda l, j: (0, 0)),
                  pl.BlockSpec((1, d, tn), lambda l, j: (l, 0, j)),
                  pl.BlockSpec((1, 1, tn), lambda l, j: (l, 0, j))],
        out_specs=pl.BlockSpec((1, r, tn), lambda l, j: (l, 0, j)),
        out_shape=jax.ShapeDtypeStruct((depth, r, n), F32),
        compiler_params=_cparams(("arbitrary", "arbitrary")),
    )(cvec, w_ada, b_ada.reshape(depth, 1, n))


CONV_HALO = 8
CONV_SLAB = 512


def _inproj_kernel(xp_ref, x_ref, xn_ref, sh_ref, sc_ref, wz_ref, wx_ref, wu_ref, wg_ref, wdt_ref, cw_ref, cb_ref,
                   z_ref, act_ref, u_ref, g_ref, dt_ref):
    j = pl.program_id(1)
    nj = pl.num_programs(1)
    tm = x_ref.shape[1]
    n = tm + 2 * CONV_HALO
    mod = lambda v: (v * (1.0 + sc_ref[0]) + sh_ref[0]).astype(BF16)
    h = mod(x_ref[0])
    h_ext = mod(jnp.concatenate([xp_ref[0], x_ref[0], xn_ref[0]], axis=0))

    row = lax.broadcasted_iota(I32, (n, 1), 0)
    inside = ((row >= CONV_HALO) | (j > 0)) & ((row < CONV_HALO + tm) | (j < nj - 1))

    z_ref[0] = _dot(h, wz_ref[...]).astype(BF16)
    u_ref[0] = _dot(h, wu_ref[...]).astype(BF16)
    g_ref[0] = _dot(h, wg_ref[...]).astype(BF16)
    dt_ref[0] = _dot(h, wdt_ref[...])
    down = lambda v: pltpu.roll(v, 1, 0)
    up = lambda v: pltpu.roll(v, n - 1, 0)
    for s in range(D_XBC // CONV_SLAB):
        cols = slice(s * CONV_SLAB, (s + 1) * CONV_SLAB)
        e = jnp.where(inside, _dot(h_ext, wx_ref[:, cols]), 0.0)
        y = [e * cw_ref[k:k + 1, cols] for k in range(CONV_K)]
        acc = y[2] + down(y[1] + down(y[0])) + up(y[3] + up(y[4]))
        act_ref[0, :, cols] = _silu(acc[CONV_HALO:CONV_HALO + tm] + cb_ref[:, cols]).astype(BF16)


def _in_proj(x, shift, scale, wts, conv_w, conv_b):
    b, l, d = x.shape
    tm = min(512, l)
    hpt = tm // CONV_HALO
    nh = l // CONV_HALO
    wz, wx, wu, wg, wdt = wts
    tok = lambda n: pl.BlockSpec((1, tm, n), lambda i, j: (i, j, 0))
    mod = pl.BlockSpec((1, 1, d), lambda i, j: (i, 0, 0))
    wspec = lambda w: pl.BlockSpec(w.shape, lambda i, j: (0, 0), pipeline_mode=pl.Buffered(1))
    par = lambda a: pl.BlockSpec(a.shape, lambda i, j: (0, 0))
    halo_prev = pl.BlockSpec((1, CONV_HALO, d), lambda i, j: (i, jnp.maximum(j * hpt - 1, 0), 0))
    halo_next = pl.BlockSpec((1, CONV_HALO, d), lambda i, j: (i, jnp.minimum((j + 1) * hpt, nh - 1), 0))
    return pl.pallas_call(
        _inproj_kernel,
        grid=(b, l // tm),
        in_specs=[halo_prev, tok(d), halo_next, mod, mod, wspec(wz), wspec(wx), wspec(wu), wspec(wg), wspec(wdt),
                  par(conv_w), par(conv_b)],
        out_specs=[tok(D_INNER), tok(D_XBC), tok(D_MODEL), tok(2 * D_MODEL), tok(LANES)],
        out_shape=[jax.ShapeDtypeStruct((b, l, D_INNER), BF16),
                   jax.ShapeDtypeStruct((b, l, D_XBC), BF16),
                   jax.ShapeDtypeStruct((b, l, D_MODEL), BF16),
                   jax.ShapeDtypeStruct((b, l, 2 * D_MODEL), BF16),
                   jax.ShapeDtypeStruct((b, l, LANES), F32)],
        compiler_params=_cparams(("arbitrary", "arbitrary")),
    )(x, x, x, shift, scale, wz, wx, wu, wg, wdt, conv_w, conv_b)


def _dt_prep(dt_raw, dtb_ref, alog_ref):
    q = CHUNK
    lane = lax.broadcasted_iota(I32, (q, LANES), 1)
    raw = dt_raw + dtb_ref[...]
    dt = jnp.maximum(raw, 0.0) + jnp.log1p(jnp.exp(-jnp.abs(raw)))
    dt = jnp.where(lane < 2 * HEADS, dt, 0.0)
    a = dt * (-jnp.exp(alog_ref[...]))
    ii = lax.broadcasted_iota(I32, (q, q), 0)
    jj = lax.broadcasted_iota(I32, (q, q), 1)
    tri = (jj <= ii).astype(BF16)
    a_hi, a_lo = _split_bf16(a)
    acs = _dot(tri, a_hi) + _dot(tri, a_lo)
    tot = acs[q - 1:q, :]
    return dt, a, acs, tot


def _expand(v, lane0, r_ref):
    return _dot(_expand_src(v, lane0), r_ref[...])


def _expand_src(v, lane0):
    lane = lax.broadcasted_iota(I32, v.shape, 1)
    vm = jnp.where((lane >= lane0) & (lane < lane0 + HEADS), v, 0.0)
    hi = vm.astype(BF16).astype(F32)
    return (hi + pltpu.roll(vm - hi, 2 * HEADS, 1)).astype(BF16)


def _state_update(st_ref, bm, xw_bf, decay_x):
    gw = D_INNER // GROUPS
    for g in range(GROUPS):
        bgt = bm[:, g * STATE:(g + 1) * STATE].T.astype(BF16)
        cols = slice(g * gw, (g + 1) * gw)
        st_ref[:, cols] = st_ref[:, cols] * decay_x[:, cols] + _dot(bgt, xw_bf[:, cols])


SCAN_CHUNKS = 4


def _ssd_state_kernel(*refs, rev, emit, nch):
    act_ref, dt_ref, h0_ref, dtb_ref, alog_ref, r_ref = refs[:6]
    if emit:
        stout_ref, sfin_ref, st_ref = refs[6:]
    else:
        sfin_ref, st_ref = refs[6:]
    s = pl.program_id(1)
    ns = pl.num_programs(1)

    @pl.when(s == 0)
    def _():
        st_ref[...] = h0_ref[0]

    for k in range(nch):
        cc = (nch - 1 - k) if rev else k
        rows = slice(cc * CHUNK, (cc + 1) * CHUNK)
        if emit:
            stout_ref[0, cc] = st_ref[...].astype(BF16)
        xs = act_ref[0, rows, :D_INNER].astype(F32)
        bm = act_ref[0, rows, D_INNER:D_INNER + D_BC].astype(F32)
        dt, a, acs, tot = _dt_prep(dt_ref[0, rows, :], dtb_ref, alog_ref)
        if rev:
            w = jnp.exp(acs - a) * dt
            lane0 = HEADS
        else:
            w = jnp.exp(tot - acs) * dt
            lane0 = 0
        w_x = _expand(w, lane0, r_ref)
        dec_x = _expand(jnp.broadcast_to(jnp.exp(tot), (16, LANES)), lane0, r_ref)[0:1]
        _state_update(st_ref, bm, (w_x * xs).astype(BF16), dec_x)

    @pl.when(s == ns - 1)
    def _():
        sfin_ref[0] = st_ref[...]


def _ssd_state_scan(act, dt, h0, lp, r_mat, *, rev, emit):
    b, l, _ = act.shape
    nc = l // CHUNK
    nch = min(SCAN_CHUNKS, nc)
    ns = nc // nch
    seq = (lambda s: ns - 1 - s) if rev else (lambda s: s)
    par = lambda shape: pl.BlockSpec(shape, lambda i, s: (0,) * len(shape))
    in_specs = [
        pl.BlockSpec((1, nch * CHUNK, D_XBC), lambda i, s: (i, seq(s), 0)),
        pl.BlockSpec((1, nch * CHUNK, LANES), lambda i, s: (i, seq(s), 0)),
        pl.BlockSpec((1, STATE, D_INNER), lambda i, s: (i, 0, 0)),
        par((1, LANES)), par((1, LANES)), par((LANES, D_INNER)),
    ]
    out_specs = [pl.BlockSpec((1, STATE, D_INNER), lambda i, s: (i, 0, 0))]
    out_shape = [jax.ShapeDtypeStruct((b, STATE, D_INNER), F32)]
    if emit:
        out_specs = [pl.BlockSpec((1, nch, STATE, D_INNER), lambda i, s: (i, seq(s), 0, 0))] + out_specs
        out_shape = [jax.ShapeDtypeStruct((b, nc, STATE, D_INNER), BF16)] + out_shape
    return pl.pallas_call(
        functools.partial(_ssd_state_kernel, rev=rev, emit=emit, nch=nch),
        grid=(b, ns),
        in_specs=in_specs, out_specs=out_specs, out_shape=out_shape,
        scratch_shapes=[pltpu.VMEM((STATE, D_INNER), F32)],
        compiler_params=_cparams(("arbitrary", "arbitrary")),
    )(act, dt, h0, lp["dt_bias"], lp["a_log"], r_mat)


def _ssd_main_kernel(act_ref, dt_ref, z_ref, stb_ref, h0_ref,
                     dtb_ref, alog_ref, dsk_ref, ng_ref, rf_ref, rb_ref,
                     y_ref, sfin_ref, st_ref, ys_ref):
    q = CHUNK
    c = pl.program_id(1)
    nc = pl.num_programs(1)

    @pl.when(c == 0)
    def _():
        st_ref[...] = h0_ref[0]

    dt, a, acs, tot = _dt_prep(dt_ref[0], dtb_ref, alog_ref)
    eb = acs - a

    ef_s = _expand_src(jnp.exp(acs), 0)
    eb_s = _expand_src(jnp.exp(tot - eb), HEADS)
    tf_s = _expand_src(jnp.exp(tot - acs) * dt, 0)
    dec_s = _expand_src(jnp.broadcast_to(jnp.exp(tot), (16, LANES)), 0)

    lane = lax.broadcasted_iota(I32, (q, LANES), 1)
    ldt = jnp.log(dt)
    qm = jnp.where(lane < HEADS, acs - ldt, jnp.where(lane < 2 * HEADS, eb + ldt, pltpu.roll(dt, 2 * HEADS, 1)))
    qt = qm.T
    ii = lax.broadcasted_iota(I32, (q, q), 0)
    jj = lax.broadcasted_iota(I32, (q, q), 1)
    lower = jj <= ii
    diag = jj == ii
    lane2 = lax.broadcasted_iota(I32, (q, LANES), 1)
    left = lane2 < HEADDIM

    gw = D_INNER // GROUPS
    hpg = HEADS // GROUPS
    for g in range(GROUPS):
        bg = act_ref[0, :, D_INNER + g * STATE:D_INNER + (g + 1) * STATE]
        cg = act_ref[0, :, D_INNER + D_BC + g * STATE:D_INNER + D_BC + (g + 1) * STATE]
        cb = lax.dot_general(cg, bg, (((1,), (1,)), ((), ())), preferred_element_type=F32)
        cols = slice(g * gw, (g + 1) * gw)
        y_off = (_dot(ef_s, rf_ref[:, cols]) * _dot(cg, st_ref[:, cols].astype(BF16))
                 + _dot(eb_s, rb_ref[:, cols]) * _dot(cg, stb_ref[0, 0, :, cols]))
        for pr in range(hpg // 2):
            ms = []
            for hh in range(2):
                h = g * hpg + pr * 2 + hh
                afc = acs[:, h:h + 1]
                afr = qt[h:h + 1, :]
                ebc = eb[:, HEADS + h:HEADS + h + 1]
                ebr = qt[HEADS + h:HEADS + h + 1, :]
                wb = qt[3 * HEADS + h:3 * HEADS + h + 1, :]
                arg = jnp.where(lower, afc - afr, ebr - ebc)
                ms.append((cb * (jnp.exp(arg) + jnp.where(diag, wb, 0.0))).astype(BF16))
            lhs = jnp.concatenate(ms, axis=1)
            c0 = g * gw + pr * 2 * HEADDIM
            xp = act_ref[0, :, c0:c0 + 2 * HEADDIM]
            zero = jnp.zeros_like(xp)
            rhs = jnp.concatenate([jnp.where(left, xp, zero), jnp.where(left, zero, xp)], axis=0)
            y_pair = _dot(lhs, rhs) + y_off[:, pr * 2 * HEADDIM:(pr + 1) * 2 * HEADDIM]
            y_pair = y_pair + dsk_ref[:, c0:c0 + 2 * HEADDIM] * xp.astype(F32)
            ys_ref[:, c0:c0 + 2 * HEADDIM] = y_pair

        xw = (_dot(tf_s, rf_ref[:, cols]) * act_ref[0, :, cols].astype(F32)).astype(BF16)
        bgt = bg.astype(F32).T.astype(BF16)
        st_ref[:, cols] = st_ref[:, cols] * _dot(dec_s, rf_ref[:, cols])[0:1] + _dot(bgt, xw)

    y = ys_ref[...]
    yz = y * _silu(z_ref[0].astype(F32))
    ms2 = jnp.mean(yz * yz, axis=-1, keepdims=True)
    y_ref[0] = (yz * lax.rsqrt(ms2 + NORM_EPS) * ng_ref[...]).astype(BF16)

    @pl.when(c == nc - 1)
    def _():
        sfin_ref[0] = st_ref[...]


def _ssd_main(act, dt, z, stb, h0, lp, r_f, r_b):
    b, l, _ = act.shape
    nc = l // CHUNK
    par = lambda shape: pl.BlockSpec(shape, lambda i, s: (0,) * len(shape))
    in_specs = [
        pl.BlockSpec((1, CHUNK, D_XBC), lambda i, s: (i, s, 0)),
        pl.BlockSpec((1, CHUNK, LANES), lambda i, s: (i, s, 0)),
        pl.BlockSpec((1, CHUNK, D_INNER), lambda i, s: (i, s, 0)),
        pl.BlockSpec((1, 1, STATE, D_INNER), lambda i, s: (i, s, 0, 0)),
        pl.BlockSpec((1, STATE, D_INNER), lambda i, s: (i, 0, 0)),
        par((1, LANES)), par((1, LANES)),
        par((1, D_INNER)), par((1, D_INNER)), par((LANES, D_INNER)), par((LANES, D_INNER)),
    ]
    return pl.pallas_call(
        _ssd_main_kernel,
        grid=(b, nc),
        in_specs=in_specs,
        out_specs=[pl.BlockSpec((1, CHUNK, D_INNER), lambda i, s: (i, s, 0)),
                   pl.BlockSpec((1, STATE, D_INNER), lambda i, s: (i, 0, 0))],
        out_shape=[jax.ShapeDtypeStruct((b, l, D_INNER), BF16),
                   jax.ShapeDtypeStruct((b, STATE, D_INNER), F32)],
        scratch_shapes=[pltpu.VMEM((STATE, D_INNER), F32), pltpu.VMEM((CHUNK, D_INNER), F32)],
        compiler_params=_cparams(("arbitrary", "arbitrary")),
    )(act, dt, z, stb, h0, lp["dt_bias"], lp["a_log"], lp["d_skip_x"], lp["ssd_norm_g"], r_f, r_b)


def _pool_kernel(*refs, width, vertical, halo, hblock):
    if vertical:
        prev_ref, cur_ref, next_ref, pw_ref, ps_ref, o_ref = refs
    else:
        cur_ref, pw_ref, ps_ref, o_ref = refs
    t = pl.program_id(1)
    nt = pl.num_programs(1)
    tp = cur_ref.shape[1]
    n_tok = nt * tp
    shift = int(math.log2(width))
    idx = lax.broadcasted_iota(I32, (tp, 1), 0) + t * tp
    colpos = idx & (width - 1)
    rowpos = idx >> shift
    n_rows = n_tok // width
    bi = lax.broadcasted_iota(I32, (hblock, hblock), 0)
    bj = lax.broadcasted_iota(I32, (hblock, hblock), 1)
    same_row = (bi >> shift) == (bj >> shift)
    cur = cur_ref[0].astype(F32)
    if vertical:
        prev = jnp.where(t > 0, prev_ref[0].astype(F32), 0.0)
        nxt = jnp.where(t < nt - 1, next_ref[0].astype(F32), 0.0)
    for gi, k in enumerate(POOL_WINDOWS):
        cols = slice(gi * POOL_GROUP_DIM, (gi + 1) * POOL_GROUP_DIM)
        ug = cur[:, cols]
        lo = k // 2
        if vertical:
            e = jnp.concatenate([prev[:, cols], ug, nxt[:, cols]], axis=0)
            step = width
            m = 1
            while m < k:
                n = e.shape[0] - step
                e = e[0:n] + e[step:step + n]
                step *= 2
                m *= 2
            start = halo - lo * width
            s = e[start:start + tp]
            cnt_r = (jnp.minimum(rowpos - lo + k, n_rows) - jnp.maximum(rowpos - lo, 0)).astype(F32)
        else:
            s = ug
            cnt_r = jnp.ones((tp, 1), F32)
        band = (same_row & (bj - bi >= -lo) & (bj - bi < k - lo)).astype(BF16)
        s_bf = s.astype(BF16)
        hs = [_dot(band, s_bf[r * hblock:(r + 1) * hblock]) for r in range(tp // hblock)]
        hsum = hs[0] if len(hs) == 1 else jnp.concatenate(hs, axis=0)
        cnt_c = (jnp.minimum(colpos - lo + k, width) - jnp.maximum(colpos - lo, 0)).astype(F32)
        mean = hsum / (cnt_r * cnt_c)
        y = _dot((mean - ug).astype(BF16), pw_ref[gi])
        o_ref[0, :, cols] = (y * ps_ref[:, cols]).astype(BF16)


def _pool_branch(u, lp, on_grid):
    b, l, d = u.shape
    pw, ps = lp["pool_w"], lp["pool_scale"]
    if on_grid:
        width, halo = GRID_W, (max(POOL_WINDOWS) // 2) * GRID_W
        tp = min(1024, l)
        hb = tp // halo
        nhb = l // halo
        in_specs = [pl.BlockSpec((1, halo, d), lambda i, t: (i, jnp.maximum(t * hb - 1, 0), 0)),
                    pl.BlockSpec((1, tp, d), lambda i, t: (i, t, 0)),
                    pl.BlockSpec((1, halo, d), lambda i, t: (i, jnp.minimum((t + 1) * hb, nhb - 1), 0))]
        args = (u, u, u)
        kern = functools.partial(_pool_kernel, width=width, vertical=True, halo=halo, hblock=LANES)
    else:
        tp = l
        in_specs = [pl.BlockSpec((1, tp, d), lambda i, t: (i, t, 0))]
        args = (u,)
        kern = functools.partial(_pool_kernel, width=l, vertical=False, halo=0, hblock=l)
    in_specs += [pl.BlockSpec(pw.shape, lambda i, t: (0, 0, 0)), pl.BlockSpec(ps.shape, lambda i, t: (0, 0))]
    return pl.pallas_call(
        kern,
        grid=(b, l // tp),
        in_specs=in_specs,
        out_specs=pl.BlockSpec((1, tp, d), lambda i, t: (i, t, 0)),
        out_shape=jax.ShapeDtypeStruct((b, l, d), BF16),
        compiler_params=_cparams(("arbitrary", "arbitrary")),
    )(*args, pw, ps)


def _merge_kernel(ya_ref, yp_ref, gt_ref, x_ref, g1_ref, sh2_ref, sc2_ref, bg_ref,
                  wa_ref, wb_ref, wo_ref, lng_ref, lnb_ref, wrh_ref, wrl_ref, br_ref, cin_ref,
                  xo_ref, h2_ref, rt_ref, cout_ref, cnt_ref, *, alpha):
    i = pl.program_id(0)
    j = pl.program_id(1)
    tm = x_ref.shape[1]

    @pl.when((i == 0) & (j == 0))
    def _():
        cnt_ref[...] = cin_ref[...]

    gate = jax.nn.sigmoid(gt_ref[0].astype(F32) + bg_ref[...])
    merged = (gate[:, :D_MODEL] * _dot(ya_ref[0], wa_ref[...])
              + gate[:, D_MODEL:] * _dot(yp_ref[0], wb_ref[...]))
    out = _dot(merged.astype(BF16), wo_ref[...])
    xn = _layer_norm(alpha * x_ref[0] + g1_ref[0] * out, lng_ref[...], lnb_ref[...])
    xo_ref[0] = xn
    h2 = xn * (1.0 + sc2_ref[0]) + sh2_ref[0]
    h2_ref[0] = _pack_rows(h2)

    h_hi, h_lo = _split_bf16(h2)
    logits = _dot(h_hi, wrh_ref[...]) + _dot(h_lo, wrh_ref[...]) + _dot(h_hi, wrl_ref[...]) + br_ref[...]
    lane = lax.broadcasted_iota(I32, (tm, LANES), 1)
    neg = jnp.float32(-jnp.inf)
    lg = jnp.where(lane < MOE_GROUPS, logits, neg)
    mg = jnp.max(lg, axis=-1, keepdims=True)
    grp = jnp.min(jnp.where(lg == mg, lane, LANES), axis=-1, keepdims=True)
    p_grp = 1.0 / jnp.sum(jnp.exp(lg - mg), axis=-1, keepdims=True)
    lo_lane = ROUTE_LANE0 + grp * MOE_EPG
    le = jnp.where((lane >= lo_lane) & (lane < lo_lane + MOE_EPG), logits, neg)
    v1 = jnp.max(le, axis=-1, keepdims=True)
    i1 = jnp.min(jnp.where(le == v1, lane, LANES), axis=-1, keepdims=True)
    le2 = jnp.where(lane == i1, neg, le)
    v2 = jnp.max(le2, axis=-1, keepdims=True)
    i2 = jnp.min(jnp.where(le2 == v2, lane, LANES), axis=-1, keepdims=True)
    e2 = jnp.exp(v2 - v1)
    w1 = p_grp / (1.0 + e2)
    w2 = p_grp * e2 / (1.0 + e2)

    oh1 = (lane == i1).astype(F32)
    oh2 = (lane == i2).astype(F32)
    oh = oh1 + oh2
    ri = lax.broadcasted_iota(I32, (tm, tm), 0)
    rj = lax.broadcasted_iota(I32, (tm, tm), 1)
    before = _dot((rj < ri).astype(BF16), oh.astype(BF16)) + cnt_ref[...]
    rank1 = jnp.sum(oh1 * before, axis=-1, keepdims=True)
    rank2 = jnp.sum(oh2 * before, axis=-1, keepdims=True)
    cnt_ref[...] = cnt_ref[...] + jnp.sum(oh, axis=0, keepdims=True)
    cout_ref[...] = cnt_ref[...]

    slab = jnp.where(lane == 0, (i1 - ROUTE_LANE0).astype(F32),
           jnp.where(lane == 1, (i2 - ROUTE_LANE0).astype(F32),
           jnp.where(lane == 2, w1,
           jnp.where(lane == 3, w2,
           jnp.where(lane == 4, rank1,
           jnp.where(lane == 5, rank2, 0.0))))))
    rt_ref[0] = slab


def _merge(ya, yp, gates, x, g1, sh2, sc2, lp, cnt_in, alpha):
    b, l, d = x.shape
    tm = min(512, l)
    tok = lambda n: pl.BlockSpec((1, tm, n), lambda i, j: (i, j, 0))
    mod = pl.BlockSpec((1, 1, d), lambda i, j: (i, 0, 0))
    par = lambda a: pl.BlockSpec(a.shape, lambda i, j: (0,) * a.ndim)
    params = (lp["b_gate"], lp["w_branch_a"], lp["w_branch_b"], lp["w_out"], lp["ln1_g"], lp["ln1_b"],
              lp["w_r_hi"], lp["w_r_lo"], lp["b_r"], cnt_in)
    return pl.pallas_call(
        functools.partial(_merge_kernel, alpha=alpha),
        grid=(b, l // tm),
        in_specs=[tok(D_INNER), tok(d), tok(2 * d), tok(d), mod, mod, mod] + [par(a) for a in params],
        out_specs=[tok(d), tok(d // 2), tok(LANES), pl.BlockSpec((1, LANES), lambda i, j: (0, 0))],
        out_shape=[jax.ShapeDtypeStruct((b, l, d), F32), jax.ShapeDtypeStruct((b, l, d // 2), U32),
                   jax.ShapeDtypeStruct((b, l, LANES), F32), jax.ShapeDtypeStruct((1, LANES), F32)],
        scratch_shapes=[pltpu.VMEM((1, LANES), F32)],
        compiler_params=_cparams(("arbitrary", "arbitrary")),
    )(ya, yp, gates, x, g1, sh2, sc2, *params)


ROW_TILE = 512


SUBLANES = 8


def _row_copy(src_row_ref, dst_row_ref, sem):
    return pltpu.make_async_copy(src_row_ref, dst_row_ref, sem)


def _dispatch_kernel(h_ref, dest_ref, xin_any, xout_any, idx_smem, sem_idx, sem_row):
    del xin_any
    n_tiles = h_ref.shape[0]
    cp = pltpu.make_async_copy(dest_ref.at[0, 0], idx_smem, sem_idx)
    cp.start()
    cp.wait()

    def issue(g, carry):
        for k in range(SUBLANES):
            r = g * SUBLANES + k
            src = h_ref.at[g, pl.ds(k, 1)]
            _row_copy(src, xout_any.at[pl.ds(idx_smem[2 * r], 1)], sem_row).start(priority=0)
            _row_copy(src, xout_any.at[pl.ds(idx_smem[2 * r + 1], 1)], sem_row).start(priority=1)
        return carry

    lax.fori_loop(0, n_tiles, issue, 0)

    def drain(g, carry):
        for _ in range(2 * SUBLANES):
            _row_copy(h_ref.at[0, pl.ds(0, 1)], xout_any.at[pl.ds(0, 1)], sem_row).wait()
        return carry

    lax.fori_loop(0, n_tiles, drain, 0)


def _dispatch(h2, dest, xin):
    t, d = h2.shape
    td = min(ROW_TILE, t)
    dest3 = dest.reshape(t // td, 1, 2 * td)
    h2 = h2.reshape(t // SUBLANES, SUBLANES, d)
    return pl.pallas_call(
        _dispatch_kernel,
        grid=(t // td,),
        in_specs=[pl.BlockSpec((td // SUBLANES, SUBLANES, d), lambda i: (i, 0, 0)),
                  pl.BlockSpec((1, 1, 2 * td), lambda i: (i, 0, 0)),
                  pl.BlockSpec(memory_space=pl.ANY)],
        out_specs=pl.BlockSpec(memory_space=pl.ANY),
        out_shape=jax.ShapeDtypeStruct(xin.shape, xin.dtype),
        scratch_shapes=[pltpu.SMEM((2 * td,), I32), pltpu.SemaphoreType.DMA(()), pltpu.SemaphoreType.DMA(())],
        input_output_aliases={2: 0},
        compiler_params=_cparams(("arbitrary",)),
    )(h2, dest3, xin)


def _expert_kernel(be_ref, nu_ref, x_ref, wg_ref, wu_ref, wd_ref, o_ref, wgb_ref, wub_ref, wdb_ref):
    i = pl.program_id(0)

    @pl.when((i == 0) | (be_ref[i] != be_ref[jnp.maximum(i - 1, 0)]))
    def _():
        wgb_ref[...] = wg_ref[0, 0].astype(BF16)
        wub_ref[...] = wu_ref[0, 0].astype(BF16)
        wdb_ref[...] = wd_ref[0, 0].astype(BF16)

    @pl.when(i < nu_ref[0])
    def _():
        xb = _unpack_rows(x_ref[...]).astype(BF16)
        gte = _dot(xb, wgb_ref[...])
        up = _dot(xb, wub_ref[...])
        o_ref[...] = _pack_rows(_dot((_silu(gte) * up).astype(BF16), wdb_ref[...]))

    @pl.when(i >= nu_ref[0])
    def _():
        o_ref[...] = jnp.zeros_like(o_ref)


def _experts(xin, block_e, n_used, w_eg, w_eu, w_ed, layer):
    n_rows, dh = xin.shape
    d = 2 * dh
    nb = n_rows // MOE_BLOCK
    grid_spec = pltpu.PrefetchScalarGridSpec(
        num_scalar_prefetch=2,
        grid=(nb,),
        in_specs=[pl.BlockSpec((MOE_BLOCK, dh), lambda i, be, nu: (i, 0)),
                  pl.BlockSpec((1, 1, d, D_EXPERT), lambda i, be, nu: (layer, be[i], 0, 0)),
                  pl.BlockSpec((1, 1, d, D_EXPERT), lambda i, be, nu: (layer, be[i], 0, 0)),
                  pl.BlockSpec((1, 1, D_EXPERT, d), lambda i, be, nu: (layer, be[i], 0, 0))],
        out_specs=pl.BlockSpec((MOE_BLOCK, dh), lambda i, be, nu: (i, 0)),
        scratch_shapes=[pltpu.VMEM((d, D_EXPERT), BF16), pltpu.VMEM((d, D_EXPERT), BF16),
                        pltpu.VMEM((D_EXPERT, d), BF16)],
    )
    return pl.pallas_call(
        _expert_kernel,
        grid_spec=grid_spec,
        out_shape=jax.ShapeDtypeStruct((n_rows, dh), U32),
        compiler_params=_cparams(("arbitrary",)),
    )(block_e, n_used, xin, w_eg, w_eu, w_ed)


def _combine_kernel(dest_ref, rt_ref, x_ref, g2_ref, lng_ref, lnb_ref, yb_any,
                    o_ref, buf_ref, idx_smem, sem_idx, sem_row, *, alpha):
    tc = x_ref.shape[1]
    cp = pltpu.make_async_copy(dest_ref.at[0, 0], idx_smem, sem_idx)
    cp.start()
    cp.wait()

    n_tiles = tc // SUBLANES

    def issue(g, carry):
        for k in range(SUBLANES):
            r = g * SUBLANES + k
            _row_copy(yb_any.at[pl.ds(idx_smem[2 * r], 1)], buf_ref.at[0, g, pl.ds(k, 1)], sem_row).start(priority=0)
            _row_copy(yb_any.at[pl.ds(idx_smem[2 * r + 1], 1)], buf_ref.at[1, g, pl.ds(k, 1)], sem_row).start(priority=1)
        return carry

    lax.fori_loop(0, n_tiles, issue, 0)

    def drain(g, carry):
        for _ in range(2 * SUBLANES):
            _row_copy(yb_any.at[pl.ds(0, 1)], buf_ref.at[0, 0, pl.ds(0, 1)], sem_row).wait()
        return carry

    lax.fori_loop(0, n_tiles, drain, 0)

    rt = rt_ref[0]
    rows = lambda k: _unpack_rows(buf_ref[k].reshape(tc, buf_ref.shape[-1]))
    y = rt[:, 2:3] * rows(0) + rt[:, 3:4] * rows(1)
    o_ref[0] = _layer_norm(alpha * x_ref[0] + g2_ref[0] * y, lng_ref[...], lnb_ref[...])


def _combine(dest, route, x, g2, lp, yb, alpha):
    b, l, d = x.shape
    tc = min(ROW_TILE, l)
    npb = l // tc
    dest3 = dest.reshape(b * npb, 1, 2 * tc)
    par = lambda a: pl.BlockSpec(a.shape, lambda i, j: (0,) * a.ndim)
    return pl.pallas_call(
        functools.partial(_combine_kernel, alpha=alpha),
        grid=(b, npb),
        in_specs=[pl.BlockSpec((1, 1, 2 * tc), lambda i, j: (i * npb + j, 0, 0)),
                  pl.BlockSpec((1, tc, LANES), lambda i, j: (i, j, 0)),
                  pl.BlockSpec((1, tc, d), lambda i, j: (i, j, 0)),
                  pl.BlockSpec((1, 1, d), lambda i, j: (i, 0, 0)),
                  par(lp["ln2_g"]), par(lp["ln2_b"]),
                  pl.BlockSpec(memory_space=pl.ANY)],
        out_specs=pl.BlockSpec((1, tc, d), lambda i, j: (i, j, 0)),
        out_shape=jax.ShapeDtypeStruct((b, l, d), F32),
        scratch_shapes=[pltpu.VMEM((2, tc // SUBLANES, SUBLANES, d // 2), U32), pltpu.SMEM((2 * tc,), I32),
                        pltpu.SemaphoreType.DMA(()), pltpu.SemaphoreType.DMA(())],
        compiler_params=_cparams(("arbitrary", "arbitrary")),
    )(dest3, route, x, g2, lp["ln2_g"], lp["ln2_b"], yb)


def _moe_plan(routes, counts):
    cnt = counts[0, ROUTE_LANE0:ROUTE_LANE0 + MOE_EXPERTS].astype(I32)
    padded = (cnt + MOE_BLOCK - 1) // MOE_BLOCK * MOE_BLOCK
    pad_end = jnp.cumsum(padded)
    pad_start = pad_end - padded
    n_assign = sum(r.shape[0] * r.shape[1] for r in routes) * 2
    n_blocks = -(-n_assign // MOE_BLOCK) + MOE_EXPERTS
    first_row = jnp.arange(n_blocks, dtype=I32)[:, None] * MOE_BLOCK
    block_e = jnp.minimum(jnp.sum((pad_end[None, :] <= first_row).astype(I32), axis=1), MOE_EXPERTS - 1)
    n_used = (pad_end[-1:] // MOE_BLOCK).astype(I32)
    eidx = jnp.arange(MOE_EXPERTS, dtype=I32)
    dests = []
    for r in routes:
        e = r[..., 0:2].astype(I32)
        rank = r[..., 4:6].astype(I32)
        start = jnp.sum(jnp.where(e[..., None] == eidx, pad_start, 0), axis=-1)
        dests.append((start + rank).reshape(-1, 2))
    return dests, block_e, n_used, n_blocks * MOE_BLOCK


def _layer_params(i, p):
    w_in = p["w_in"][i]
    s0, s1, s2, s3 = D_INNER, D_INNER + D_XBC, D_INNER + D_XBC + 2 * HEADS, D_INNER + D_XBC + 2 * HEADS + D_MODEL
    pad = LANES - 2 * HEADS
    wdt = jnp.pad(w_in[:, s1:s2], ((0, 0), (0, pad)))
    w_r = jnp.concatenate([p["w_router_group"][i],
                           jnp.transpose(p["w_router_expert"][i], (1, 0, 2)).reshape(D_MODEL, MOE_EXPERTS)], axis=1)
    w_r = jnp.pad(w_r, ((0, 0), (0, LANES - w_r.shape[1])))
    w_r_hi = w_r.astype(BF16)
    b_r = jnp.concatenate([p["b_router_group"][i], p["b_router_expert"][i].reshape(-1)])
    return {
        "w_in": tuple(w.astype(BF16) for w in (w_in[:, :s0], w_in[:, s0:s1], w_in[:, s2:s3], w_in[:, s3:], wdt)),
        "conv_w": p["conv_w"][i], "conv_b": p["conv_b"][i][None],
        "dt_bias": jnp.pad(p["dt_bias"][i].reshape(1, -1), ((0, 0), (0, pad))),
        "a_log": jnp.pad(p["a_log"][i].reshape(1, -1), ((0, 0), (0, pad))),
        "d_skip_x": jnp.repeat(p["d_skip"][i], HEADDIM)[None],
        "ssd_norm_g": p["ssd_norm_g"][i][None],
        "pool_w": p["pool_w"][i].astype(BF16), "pool_scale": p["pool_scale"][i][None],
        "b_gate": p["b_gate"][i][None],
        "w_branch_a": p["w_branch_a"][i].astype(BF16), "w_branch_b": p["w_branch_b"][i].astype(BF16),
        "w_out": p["w_out"][i].astype(BF16),
        "ln1_g": p["ln1_g"][i][None], "ln1_b": p["ln1_b"][i][None],
        "ln2_g": p["ln2_g"][i][None], "ln2_b": p["ln2_b"][i][None],
        "w_r_hi": w_r_hi, "w_r_lo": (w_r - w_r_hi.astype(F32)).astype(BF16),
        "b_r": jnp.pad(b_r, (0, LANES - b_r.shape[0]))[None],
    }


def _head_expanders():
    col_head = jnp.arange(D_INNER, dtype=I32) // HEADDIM
    row = jnp.arange(LANES, dtype=I32)[:, None] % (2 * HEADS)
    return (row == col_head).astype(BF16), (row == col_head + HEADS).astype(BF16)


def kernel(x, c, ctx, c_ctx, w_ada, b_ada, w_in, b_gate, conv_w, conv_b, dt_bias, a_log, d_skip, ssd_norm_g, pool_w, pool_scale, w_branch_a, w_branch_b, w_out, ln1_g, ln1_b, ln2_g, ln2_b, w_router_group, b_router_group, w_router_expert, b_router_expert, w_expert_gate, w_expert_up, w_expert_down):
    p = dict(w_in=w_in, b_gate=b_gate, conv_w=conv_w, conv_b=conv_b, dt_bias=dt_bias, a_log=a_log, d_skip=d_skip,
             ssd_norm_g=ssd_norm_g, pool_w=pool_w, pool_scale=pool_scale, w_branch_a=w_branch_a,
             w_branch_b=w_branch_b, w_out=w_out, ln1_g=ln1_g, ln1_b=ln1_b, ln2_g=ln2_g, ln2_b=ln2_b,
             w_router_group=w_router_group, b_router_group=b_router_group, w_router_expert=w_router_expert,
             b_router_expert=b_router_expert, w_expert_gate=w_expert_gate, w_expert_up=w_expert_up,
             w_expert_down=w_expert_down)
    b, l, d = x.shape
    depth = w_ada.shape[0]
    alpha = (2.0 * depth) ** DEPTH_ALPHA_POW
    n_mod = -(-(b + 1) // 8) * 8
    cvec = jnp.zeros((n_mod, d), F32).at[:b].set(c).at[b].set(c_ctx)
    mods = _ada_mods(cvec, w_ada, b_ada)
    r_f, r_b = _head_expanders()
    zero_state = jnp.zeros((b, STATE, D_INNER), F32)
    zero_cnt = jnp.zeros((1, LANES), F32)
    xl, xc = x, ctx
    for i in range(depth):
        last = i == depth - 1
        lp = _layer_params(i, p)
        mod_l = [mods[i, :b, k * d:(k + 1) * d][:, None, :] for k in range(6)]
        mod_c = [jnp.broadcast_to(mods[i, b, k * d:(k + 1) * d], (b, 1, d)) for k in range(6)]

        zc, act_c, uc, gc, dtc = _in_proj(xc, mod_c[0], mod_c[1], lp["w_in"], lp["conv_w"], lp["conv_b"])
        if last:
            (s_f,) = _ssd_state_scan(act_c, dtc, zero_state, lp, r_f, rev=False, emit=False)
            (s_b,) = _ssd_state_scan(act_c, dtc, zero_state, lp, r_b, rev=True, emit=False)
        else:
            stb_c, s_b = _ssd_state_scan(act_c, dtc, zero_state, lp, r_b, rev=True, emit=True)
            ya_c, s_f = _ssd_main(act_c, dtc, zc, stb_c, zero_state, lp, r_f, r_b)
            yp_c = _pool_branch(uc, lp, False)

        zl, act_l, ul, gl, dtl = _in_proj(xl, mod_l[0], mod_l[1], lp["w_in"], lp["conv_w"], lp["conv_b"])
        stb_l, _ = _ssd_state_scan(act_l, dtl, s_b, lp, r_b, rev=True, emit=True)
        ya_l, _ = _ssd_main(act_l, dtl, zl, stb_l, s_f, lp, r_f, r_b)
        yp_l = _pool_branch(ul, lp, True)
        xl, h2_l, rt_l, cnt = _merge(ya_l, yp_l, gl, xl, mod_l[2], mod_l[3], mod_l[4], lp, zero_cnt, alpha)
        streams = [(h2_l, rt_l)]
        if not last:
            xc, h2_c, rt_c, cnt = _merge(ya_c, yp_c, gc, xc, mod_c[2], mod_c[3], mod_c[4], lp, cnt, alpha)
            streams.append((h2_c, rt_c))

        dests, block_e, n_used, n_rows = _moe_plan([rt for _, rt in streams], cnt)
        xin = jnp.zeros((n_rows, d // 2), U32)
        for (h2, _), dest in zip(streams, dests):
            xin = _dispatch(h2.reshape(-1, d // 2), dest, xin)
        yb = _experts(xin, block_e, n_used, w_expert_gate, w_expert_up, w_expert_down, i)
        xl = _combine(dests[0], rt_l, xl, mod_l[5], lp, yb, alpha)
        if not last:
            xc = _combine(dests[1], rt_c, xc, mod_c[5], lp, yb, alpha)
    return xl
```

```python
import functools
import math

import jax
import jax.numpy as jnp
from jax import lax
from jax.experimental import pallas as pl
from jax.experimental.pallas import tpu as pltpu

F32 = jnp.float32
BF16 = jnp.bfloat16
I32 = jnp.int32

D_MODEL = 1024
D_INNER = 2048
HEADS = 32
HEADDIM = 64
GROUPS = 4
STATE = 128
D_BC = GROUPS * STATE
D_XBC = D_INNER + 2 * D_BC
CONV_K = 5
CHUNK = 128
GRID_W = 64
POOL_WINDOWS = (2, 4, 8, 16)
POOL_GROUP_DIM = 256
MOE_GROUPS = 4
MOE_EPG = 8
MOE_EXPERTS = 32
D_EXPERT = 512
MOE_BLOCK = 512
DEPTH_ALPHA_POW = 0.25
NORM_EPS = 1e-5
LANES = 128
ROUTE_LANE0 = MOE_GROUPS
VMEM_LIMIT = 56 * 1024 * 1024


def _cparams(sem, vmem=VMEM_LIMIT):
    return pltpu.CompilerParams(dimension_semantics=sem, vmem_limit_bytes=vmem)


def _sigmoid(v):
    return 0.5 + 0.5 * jnp.tanh(0.5 * v)


def _silu(v):
    h = 0.5 * v
    return h + h * jnp.tanh(h)


def _layer_norm(v, g, b):
    mu = jnp.mean(v, axis=-1, keepdims=True)
    d = v - mu
    var = jnp.mean(d * d, axis=-1, keepdims=True)
    return d * lax.rsqrt(var + NORM_EPS) * g + b


def _split_bf16(v):
    hi = v.astype(BF16)
    lo = (v - hi.astype(F32)).astype(BF16)
    return hi, lo


def _dot(a, b):
    return jnp.dot(a, b, preferred_element_type=F32)


U32 = jnp.uint32
HI16 = 0xFFFF0000


def _pack_rows(v):
    half = v.shape[1] // 2
    lo = lax.bitcast_convert_type(v[:, :half].astype(BF16).astype(F32), U32)
    hi = lax.bitcast_convert_type(v[:, half:].astype(BF16).astype(F32), U32)
    return (lo >> 16) | (hi & U32(HI16))


def _unpack_rows(w):
    lo = lax.bitcast_convert_type(w << 16, F32)
    hi = lax.bitcast_convert_type(w & U32(HI16), F32)
    return jnp.concatenate([lo, hi], axis=1)


def _ada_kernel(c_ref, w_ref, b_ref, o_ref):
    s = _silu(c_ref[...]).astype(BF16)
    o_ref[0] = _dot(s, w_ref[0].astype(BF16)) + b_ref[0]


def _ada_mods(cvec, w_ada, b_ada):
    depth, d, n = w_ada.shape
    r = cvec.shape[0]
    tn = D_MODEL
    return pl.pallas_call(
        _ada_kernel,
        grid=(depth, n // tn),
        in_specs=[pl.BlockSpec((r, d), lambda l, j: (0, 0)),
                  pl.BlockSpec((1, d, tn), lambda l, j: (l, 0, j)),
                  pl.BlockSpec((1, 1, tn), lambda l, j: (l, 0, j))],
        out_specs=pl.BlockSpec((1, r, tn), lambda l, j: (l, 0, j)),
        out_shape=jax.ShapeDtypeStruct((depth, r, n), F32),
        compiler_params=_cparams(("arbitrary", "arbitrary")),
    )(cvec, w_ada, b_ada.reshape(depth, 1, n))


CONV_HALO = 8
CONV_SLAB = 512


def _inproj_kernel(xp_ref, x_ref, xn_ref, sh_ref, sc_ref, wz_ref, wx_ref, wu_ref, wg_ref, wdt_ref, cw_ref, cb_ref,
                   z_ref, act_ref, u_ref, g_ref, dt_ref):
    j = pl.program_id(1)
    nj = pl.num_programs(1)
    tm = x_ref.shape[1]
    n = tm + 2 * CONV_HALO
    mod = lambda v: (v * (1.0 + sc_ref[0]) + sh_ref[0]).astype(BF16)
    h = mod(x_ref[0])
    h_ext = mod(jnp.concatenate([xp_ref[0], x_ref[0], xn_ref[0]], axis=0))

    row = lax.broadcasted_iota(I32, (n, 1), 0)
    inside = ((row >= CONV_HALO) | (j > 0)) & ((row < CONV_HALO + tm) | (j < nj - 1))

    z_ref[0] = _dot(h, wz_ref[...]).astype(BF16)
    u_ref[0] = _dot(h, wu_ref[...]).astype(BF16)
    g_ref[0] = _dot(h, wg_ref[...]).astype(BF16)
    dt_ref[0] = _dot(h, wdt_ref[...])
    down = lambda v: pltpu.roll(v, 1, 0)
    up = lambda v: pltpu.roll(v, n - 1, 0)
    for s in range(D_XBC // CONV_SLAB):
        cols = slice(s * CONV_SLAB, (s + 1) * CONV_SLAB)
        e = jnp.where(inside, _dot(h_ext, wx_ref[:, cols]), 0.0)
        y = [e * cw_ref[k:k + 1, cols] for k in range(CONV_K)]
        acc = y[2] + down(y[1] + down(y[0])) + up(y[3] + up(y[4]))
        act_ref[0, :, cols] = _silu(acc[CONV_HALO:CONV_HALO + tm] + cb_ref[:, cols]).astype(BF16)


def _in_proj(x, shift, scale, wts, conv_w, conv_b):
    b, l, d = x.shape
    tm = min(512, l)
    hpt = tm // CONV_HALO
    nh = l // CONV_HALO
    wz, wx, wu, wg, wdt = wts
    tok = lambda n: pl.BlockSpec((1, tm, n), lambda i, j: (i, j, 0))
    mod = pl.BlockSpec((1, 1, d), lambda i, j: (i, 0, 0))
    wspec = lambda w: pl.BlockSpec(w.shape, lambda i, j: (0, 0), pipeline_mode=pl.Buffered(1))
    par = lambda a: pl.BlockSpec(a.shape, lambda i, j: (0, 0))
    halo_prev = pl.BlockSpec((1, CONV_HALO, d), lambda i, j: (i, jnp.maximum(j * hpt - 1, 0), 0))
    halo_next = pl.BlockSpec((1, CONV_HALO, d), lambda i, j: (i, jnp.minimum((j + 1) * hpt, nh - 1), 0))
    return pl.pallas_call(
        _inproj_kernel,
        grid=(b, l // tm),
        in_specs=[halo_prev, tok(d), halo_next, mod, mod, wspec(wz), wspec(wx), wspec(wu), wspec(wg), wspec(wdt),
                  par(conv_w), par(conv_b)],
        out_specs=[tok(D_INNER), tok(D_XBC), tok(D_MODEL), tok(2 * D_MODEL), tok(LANES)],
        out_shape=[jax.ShapeDtypeStruct((b, l, D_INNER), BF16),
                   jax.ShapeDtypeStruct((b, l, D_XBC), BF16),
                   jax.ShapeDtypeStruct((b, l, D_MODEL), BF16),
                   jax.ShapeDtypeStruct((b, l, 2 * D_MODEL), BF16),
                   jax.ShapeDtypeStruct((b, l, LANES), F32)],
        compiler_params=_cparams(("arbitrary", "arbitrary")),
    )(x, x, x, shift, scale, wz, wx, wu, wg, wdt, conv_w, conv_b)


def _dt_prep(dt_raw, dtb_ref, alog_ref):
    q = CHUNK
    lane = lax.broadcasted_iota(I32, (q, LANES), 1)
    raw = dt_raw + dtb_ref[...]
    dt = jnp.maximum(raw, 0.0) + jnp.log1p(jnp.exp(-jnp.abs(raw)))
    dt = jnp.where(lane < 2 * HEADS, dt, 0.0)
    a = dt * (-jnp.exp(alog_ref[...]))
    ii = lax.broadcasted_iota(I32, (q, q), 0)
    jj = lax.broadcasted_iota(I32, (q, q), 1)
    tri = (jj <= ii).astype(BF16)
    a_hi, a_lo = _split_bf16(a)
    acs = _dot(tri, a_hi) + _dot(tri, a_lo)
    tot = acs[q - 1:q, :]
    return dt, a, acs, tot


def _expand(v, lane0, r_ref):
    return _dot(_expand_src(v, lane0), r_ref[...])


def _expand_src(v, lane0):
    lane = lax.broadcasted_iota(I32, v.shape, 1)
    vm = jnp.where((lane >= lane0) & (lane < lane0 + HEADS), v, 0.0)
    hi = vm.astype(BF16).astype(F32)
    return (hi + pltpu.roll(vm - hi, 2 * HEADS, 1)).astype(BF16)


def _state_update(st_ref, bm, xw_bf, decay_x):
    gw = D_INNER // GROUPS
    for g in range(GROUPS):
        bgt = bm[:, g * STATE:(g + 1) * STATE].T.astype(BF16)
        cols = slice(g * gw, (g + 1) * gw)
        st_ref[:, cols] = st_ref[:, cols] * decay_x[:, cols] + _dot(bgt, xw_bf[:, cols])


SCAN_CHUNKS = 4


def _ssd_state_kernel(*refs, rev, emit, nch):
    act_ref, dt_ref, h0_ref, dtb_ref, alog_ref, r_ref = refs[:6]
    if emit:
        stout_ref, sfin_ref, st_ref = refs[6:]
    else:
        sfin_ref, st_ref = refs[6:]
    s = pl.program_id(1)
    ns = pl.num_programs(1)

    @pl.when(s == 0)
    def _():
        st_ref[...] = h0_ref[0]

    for k in range(nch):
        cc = (nch - 1 - k) if rev else k
        rows = slice(cc * CHUNK, (cc + 1) * CHUNK)
        if emit:
            stout_ref[0, cc] = st_ref[...].astype(BF16)
        xs = act_ref[0, rows, :D_INNER].astype(F32)
        bm = act_ref[0, rows, D_INNER:D_INNER + D_BC].astype(F32)
        dt, a, acs, tot = _dt_prep(dt_ref[0, rows, :], dtb_ref, alog_ref)
        if rev:
            w = jnp.exp(acs - a) * dt
            lane0 = HEADS
        else:
            w = jnp.exp(tot - acs) * dt
            lane0 = 0
        w_x = _expand(w, lane0, r_ref)
        dec_x = _expand(jnp.broadcast_to(jnp.exp(tot), (16, LANES)), lane0, r_ref)[0:1]
        _state_update(st_ref, bm, (w_x * xs).astype(BF16), dec_x)

    @pl.when(s == ns - 1)
    def _():
        sfin_ref[0] = st_ref[...]


def _ssd_state_scan(act, dt, h0, lp, r_mat, *, rev, emit):
    b, l, _ = act.shape
    nc = l // CHUNK
    nch = min(SCAN_CHUNKS, nc)
    ns = nc // nch
    seq = (lambda s: ns - 1 - s) if rev else (lambda s: s)
    par = lambda shape: pl.BlockSpec(shape, lambda i, s: (0,) * len(shape))
    in_specs = [
        pl.BlockSpec((1, nch * CHUNK, D_XBC), lambda i, s: (i, seq(s), 0)),
        pl.BlockSpec((1, nch * CHUNK, LANES), lambda i, s: (i, seq(s), 0)),
        pl.BlockSpec((1, STATE, D_INNER), lambda i, s: (i, 0, 0)),
        par((1, LANES)), par((1, LANES)), par((LANES, D_INNER)),
    ]
    out_specs = [pl.BlockSpec((1, STATE, D_INNER), lambda i, s: (i, 0, 0))]
    out_shape = [jax.ShapeDtypeStruct((b, STATE, D_INNER), F32)]
    if emit:
        out_specs = [pl.BlockSpec((1, nch, STATE, D_INNER), lambda i, s: (i, seq(s), 0, 0))] + out_specs
        out_shape = [jax.ShapeDtypeStruct((b, nc, STATE, D_INNER), BF16)] + out_shape
    return pl.pallas_call(
        functools.partial(_ssd_state_kernel, rev=rev, emit=emit, nch=nch),
        grid=(b, ns),
        in_specs=in_specs, out_specs=out_specs, out_shape=out_shape,
        scratch_shapes=[pltpu.VMEM((STATE, D_INNER), F32)],
        compiler_params=_cparams(("arbitrary", "arbitrary")),
    )(act, dt, h0, lp["dt_bias"], lp["a_log"], r_mat)


MAIN_CHUNKS = 2


def _ssd_main_kernel(act_ref, dt_ref, z_ref, stb_ref, h0_ref,
                     dtb_ref, alog_ref, dsk_ref, ng_ref, rf_ref, rb_ref,
                     y_ref, sfin_ref, st_ref, ys_ref, *, nch):
    c = pl.program_id(1)
    nc = pl.num_programs(1)

    @pl.when(c == 0)
    def _():
        st_ref[...] = h0_ref[0]

    for k in range(nch):
        rows = pl.ds(k * CHUNK, CHUNK)
        _ssd_main_chunk(act_ref.at[0, rows], dt_ref.at[0, rows], z_ref.at[0, rows], stb_ref.at[0, k],
                        dtb_ref, alog_ref, dsk_ref, ng_ref, rf_ref, rb_ref, y_ref.at[0, rows], st_ref, ys_ref)

    @pl.when(c == nc - 1)
    def _():
        sfin_ref[0] = st_ref[...]


def _ssd_main_chunk(act_ref, dt_ref, z_ref, stb_ref, dtb_ref, alog_ref, dsk_ref, ng_ref, rf_ref, rb_ref,
                    y_ref, st_ref, ys_ref):
    q = CHUNK
    dt, a, acs, tot = _dt_prep(dt_ref[...], dtb_ref, alog_ref)
    eb = acs - a

    ef_s = _expand_src(jnp.exp(acs), 0)
    eb_s = _expand_src(jnp.exp(tot - eb), HEADS)
    tf_s = _expand_src(jnp.exp(tot - acs) * dt, 0)
    dec_s = _expand_src(jnp.broadcast_to(jnp.exp(tot), (16, LANES)), 0)

    lane = lax.broadcasted_iota(I32, (q, LANES), 1)
    ldt = jnp.log(dt)
    qm = jnp.where(lane < HEADS, acs - ldt, jnp.where(lane < 2 * HEADS, eb + ldt, pltpu.roll(dt, 2 * HEADS, 1)))
    qt = qm.T
    ii = lax.broadcasted_iota(I32, (q, q), 0)
    jj = lax.broadcasted_iota(I32, (q, q), 1)
    lower = jj <= ii
    diag = jj == ii
    lane2 = lax.broadcasted_iota(I32, (q, LANES), 1)
    left = lane2 < HEADDIM

    gw = D_INNER // GROUPS
    hpg = HEADS // GROUPS
    for g in range(GROUPS):
        bg = act_ref[:, D_INNER + g * STATE:D_INNER + (g + 1) * STATE]
        cg = act_ref[:, D_INNER + D_BC + g * STATE:D_INNER + D_BC + (g + 1) * STATE]
        cb = lax.dot_general(cg, bg, (((1,), (1,)), ((), ())), preferred_element_type=F32)
        cols = slice(g * gw, (g + 1) * gw)
        y_off = (_dot(ef_s, rf_ref[:, cols]) * _dot(cg, st_ref[:, cols].astype(BF16))
                 + _dot(eb_s, rb_ref[:, cols]) * _dot(cg, stb_ref[:, cols]))
        for pr in range(hpg // 2):
            ms = []
            for hh in range(2):
                h = g * hpg + pr * 2 + hh
                afc = acs[:, h:h + 1]
                afr = qt[h:h + 1, :]
                ebc = eb[:, HEADS + h:HEADS + h + 1]
                ebr = qt[HEADS + h:HEADS + h + 1, :]
                wb = qt[3 * HEADS + h:3 * HEADS + h + 1, :]
                arg = jnp.where(lower, afc - afr, ebr - ebc)
                ms.append((cb * (jnp.exp(arg) + jnp.where(diag, wb, 0.0))).astype(BF16))
            lhs = jnp.concatenate(ms, axis=1)
            c0 = g * gw + pr * 2 * HEADDIM
            xp = act_ref[:, c0:c0 + 2 * HEADDIM]
            zero = jnp.zeros_like(xp)
            rhs = jnp.concatenate([jnp.where(left, xp, zero), jnp.where(left, zero, xp)], axis=0)
            y_pair = _dot(lhs, rhs) + y_off[:, pr * 2 * HEADDIM:(pr + 1) * 2 * HEADDIM]
            y_pair = y_pair + dsk_ref[:, c0:c0 + 2 * HEADDIM] * xp.astype(F32)
            ys_ref[:, c0:c0 + 2 * HEADDIM] = y_pair

        xw = (_dot(tf_s, rf_ref[:, cols]) * act_ref[:, cols].astype(F32)).astype(BF16)
        bgt = bg.astype(F32).T.astype(BF16)
        st_ref[:, cols] = st_ref[:, cols] * _dot(dec_s, rf_ref[:, cols])[0:1] + _dot(bgt, xw)

    y = ys_ref[...]
    yz = y * _silu(z_ref[...].astype(F32))
    ms2 = jnp.mean(yz * yz, axis=-1, keepdims=True)
    y_ref[...] = (yz * lax.rsqrt(ms2 + NORM_EPS) * ng_ref[...]).astype(BF16)


def _ssd_main(act, dt, z, stb, h0, lp, r_f, r_b):
    b, l, _ = act.shape
    nc = l // CHUNK
    nch = min(MAIN_CHUNKS, nc)
    rows = nch * CHUNK
    par = lambda shape: pl.BlockSpec(shape, lambda i, s: (0,) * len(shape))
    in_specs = [
        pl.BlockSpec((1, rows, D_XBC), lambda i, s: (i, s, 0)),
        pl.BlockSpec((1, rows, LANES), lambda i, s: (i, s, 0)),
        pl.BlockSpec((1, rows, D_INNER), lambda i, s: (i, s, 0)),
        pl.BlockSpec((1, nch, STATE, D_INNER), lambda i, s: (i, s, 0, 0)),
        pl.BlockSpec((1, STATE, D_INNER), lambda i, s: (i, 0, 0)),
        par((1, LANES)), par((1, LANES)),
        par((1, D_INNER)), par((1, D_INNER)), par((LANES, D_INNER)), par((LANES, D_INNER)),
    ]
    return pl.pallas_call(
        functools.partial(_ssd_main_kernel, nch=nch),
        grid=(b, nc // nch),
        in_specs=in_specs,
        out_specs=[pl.BlockSpec((1, rows, D_INNER), lambda i, s: (i, s, 0)),
                   pl.BlockSpec((1, STATE, D_INNER), lambda i, s: (i, 0, 0))],
        out_shape=[jax.ShapeDtypeStruct((b, l, D_INNER), BF16),
                   jax.ShapeDtypeStruct((b, STATE, D_INNER), F32)],
        scratch_shapes=[pltpu.VMEM((STATE, D_INNER), F32), pltpu.VMEM((CHUNK, D_INNER), F32)],
        compiler_params=_cparams(("arbitrary", "arbitrary")),
    )(act, dt, z, stb, h0, lp["dt_bias"], lp["a_log"], lp["d_skip_x"], lp["ssd_norm_g"], r_f, r_b)


def _pool_kernel(*refs, width, vertical, halo, hblock):
    if vertical:
        prev_ref, cur_ref, next_ref, pw_ref, ps_ref, o_ref = refs
    else:
        cur_ref, pw_ref, ps_ref, o_ref = refs
    t = pl.program_id(1)
    nt = pl.num_programs(1)
    tp = cur_ref.shape[1]
    n_tok = nt * tp
    shift = int(math.log2(width))
    idx = lax.broadcasted_iota(I32, (tp, 1), 0) + t * tp
    colpos = idx & (width - 1)
    rowpos = idx >> shift
    n_rows = n_tok // width
    bi = lax.broadcasted_iota(I32, (hblock, hblock), 0)
    bj = lax.broadcasted_iota(I32, (hblock, hblock), 1)
    same_row = (bi >> shift) == (bj >> shift)
    cur = cur_ref[0].astype(F32)
    if vertical:
        prev = jnp.where(t > 0, prev_ref[0].astype(F32), 0.0)
        nxt = jnp.where(t < nt - 1, next_ref[0].astype(F32), 0.0)
    for gi, k in enumerate(POOL_WINDOWS):
        cols = slice(gi * POOL_GROUP_DIM, (gi + 1) * POOL_GROUP_DIM)
        ug = cur[:, cols]
        lo = k // 2
        if vertical:
            e = jnp.concatenate([prev[:, cols], ug, nxt[:, cols]], axis=0)
            step = width
            m = 1
            while m < k:
                n = e.shape[0] - step
                e = e[0:n] + e[step:step + n]
                step *= 2
                m *= 2
            start = halo - lo * width
            s = e[start:start + tp]
            cnt_r = (jnp.minimum(rowpos - lo + k, n_rows) - jnp.maximum(rowpos - lo, 0)).astype(F32)
        else:
            s = ug
            cnt_r = jnp.ones((tp, 1), F32)
        band = (same_row & (bj - bi >= -lo) & (bj - bi < k - lo)).astype(BF16)
        s_bf = s.astype(BF16)
        hs = [_dot(band, s_bf[r * hblock:(r + 1) * hblock]) for r in range(tp // hblock)]
        hsum = hs[0] if len(hs) == 1 else jnp.concatenate(hs, axis=0)
        cnt_c = (jnp.minimum(colpos - lo + k, width) - jnp.maximum(colpos - lo, 0)).astype(F32)
        mean = hsum / (cnt_r * cnt_c)
        y = _dot((mean - ug).astype(BF16), pw_ref[gi])
        o_ref[0, :, cols] = (y * ps_ref[:, cols]).astype(BF16)


def _pool_branch(u, lp, on_grid):
    b, l, d = u.shape
    pw, ps = lp["pool_w"], lp["pool_scale"]
    if on_grid:
        width, halo = GRID_W, (max(POOL_WINDOWS) // 2) * GRID_W
        tp = min(1024, l)
        hb = tp // halo
        nhb = l // halo
        in_specs = [pl.BlockSpec((1, halo, d), lambda i, t: (i, jnp.maximum(t * hb - 1, 0), 0)),
                    pl.BlockSpec((1, tp, d), lambda i, t: (i, t, 0)),
                    pl.BlockSpec((1, halo, d), lambda i, t: (i, jnp.minimum((t + 1) * hb, nhb - 1), 0))]
        args = (u, u, u)
        kern = functools.partial(_pool_kernel, width=width, vertical=True, halo=halo, hblock=LANES)
    else:
        tp = l
        in_specs = [pl.BlockSpec((1, tp, d), lambda i, t: (i, t, 0))]
        args = (u,)
        kern = functools.partial(_pool_kernel, width=l, vertical=False, halo=0, hblock=l)
    in_specs += [pl.BlockSpec(pw.shape, lambda i, t: (0, 0, 0)), pl.BlockSpec(ps.shape, lambda i, t: (0, 0))]
    return pl.pallas_call(
        kern,
        grid=(b, l // tp),
        in_specs=in_specs,
        out_specs=pl.BlockSpec((1, tp, d), lambda i, t: (i, t, 0)),
        out_shape=jax.ShapeDtypeStruct((b, l, d), BF16),
        compiler_params=_cparams(("arbitrary", "arbitrary")),
    )(*args, pw, ps)


def _merge_kernel(ya_ref, yp_ref, gt_ref, x_ref, g1_ref, sh2_ref, sc2_ref, bg_ref,
                  wa_ref, wb_ref, wo_ref, lng_ref, lnb_ref, wrh_ref, wrl_ref, br_ref, cin_ref,
                  xo_ref, h2_ref, rt_ref, cout_ref, cnt_ref, *, alpha):
    i = pl.program_id(0)
    j = pl.program_id(1)
    tm = x_ref.shape[1]

    @pl.when((i == 0) & (j == 0))
    def _():
        cnt_ref[...] = cin_ref[...]

    gate = _sigmoid(gt_ref[0].astype(F32) + bg_ref[...])
    merged = (gate[:, :D_MODEL] * _dot(ya_ref[0], wa_ref[...])
              + gate[:, D_MODEL:] * _dot(yp_ref[0], wb_ref[...]))
    out = _dot(merged.astype(BF16), wo_ref[...])
    xn = _layer_norm(alpha * x_ref[0] + g1_ref[0] * out, lng_ref[...], lnb_ref[...])
    xo_ref[0] = xn
    h2 = xn * (1.0 + sc2_ref[0]) + sh2_ref[0]
    h2_ref[0] = _pack_rows(h2)

    h_hi, h_lo = _split_bf16(h2)
    logits = _dot(h_hi, wrh_ref[...]) + _dot(h_lo, wrh_ref[...]) + _dot(h_hi, wrl_ref[...]) + br_ref[...]
    lane = lax.broadcasted_iota(I32, (tm, LANES), 1)
    neg = jnp.float32(-jnp.inf)
    lg = jnp.where(lane < MOE_GROUPS, logits, neg)
    mg = jnp.max(lg, axis=-1, keepdims=True)
    grp = jnp.min(jnp.where(lg == mg, lane, LANES), axis=-1, keepdims=True)
    p_grp = 1.0 / jnp.sum(jnp.exp(lg - mg), axis=-1, keepdims=True)
    lo_lane = ROUTE_LANE0 + grp * MOE_EPG
    le = jnp.where((lane >= lo_lane) & (lane < lo_lane + MOE_EPG), logits, neg)
    v1 = jnp.max(le, axis=-1, keepdims=True)
    i1 = jnp.min(jnp.where(le == v1, lane, LANES), axis=-1, keepdims=True)
    le2 = jnp.where(lane == i1, neg, le)
    v2 = jnp.max(le2, axis=-1, keepdims=True)
    i2 = jnp.min(jnp.where(le2 == v2, lane, LANES), axis=-1, keepdims=True)
    e2 = jnp.exp(v2 - v1)
    w1 = p_grp / (1.0 + e2)
    w2 = p_grp * e2 / (1.0 + e2)

    oh1 = (lane == i1).astype(F32)
    oh2 = (lane == i2).astype(F32)
    oh = oh1 + oh2
    ri = lax.broadcasted_iota(I32, (tm, tm), 0)
    rj = lax.broadcasted_iota(I32, (tm, tm), 1)
    before = _dot((rj < ri).astype(BF16), oh.astype(BF16)) + cnt_ref[...]
    rank1 = jnp.sum(oh1 * before, axis=-1, keepdims=True)
    rank2 = jnp.sum(oh2 * before, axis=-1, keepdims=True)
    cnt_ref[...] = cnt_ref[...] + jnp.sum(oh, axis=0, keepdims=True)
    cout_ref[...] = cnt_ref[...]

    slab = jnp.where(lane == 0, (i1 - ROUTE_LANE0).astype(F32),
           jnp.where(lane == 1, (i2 - ROUTE_LANE0).astype(F32),
           jnp.where(lane == 2, w1,
           jnp.where(lane == 3, w2,
           jnp.where(lane == 4, rank1,
           jnp.where(lane == 5, rank2, 0.0))))))
    rt_ref[0] = slab


def _merge(ya, yp, gates, x, g1, sh2, sc2, lp, cnt_in, alpha):
    b, l, d = x.shape
    tm = min(512, l)
    tok = lambda n: pl.BlockSpec((1, tm, n), lambda i, j: (i, j, 0))
    mod = pl.BlockSpec((1, 1, d), lambda i, j: (i, 0, 0))
    par = lambda a: pl.BlockSpec(a.shape, lambda i, j: (0,) * a.ndim)
    params = (lp["b_gate"], lp["w_branch_a"], lp["w_branch_b"], lp["w_out"], lp["ln1_g"], lp["ln1_b"],
              lp["w_r_hi"], lp["w_r_lo"], lp["b_r"], cnt_in)
    return pl.pallas_call(
        functools.partial(_merge_kernel, alpha=alpha),
        grid=(b, l // tm),
        in_specs=[tok(D_INNER), tok(d), tok(2 * d), tok(d), mod, mod, mod] + [par(a) for a in params],
        out_specs=[tok(d), tok(d // 2), tok(LANES), pl.BlockSpec((1, LANES), lambda i, j: (0, 0))],
        out_shape=[jax.ShapeDtypeStruct((b, l, d), F32), jax.ShapeDtypeStruct((b, l, d // 2), U32),
                   jax.ShapeDtypeStruct((b, l, LANES), F32), jax.ShapeDtypeStruct((1, LANES), F32)],
        scratch_shapes=[pltpu.VMEM((1, LANES), F32)],
        compiler_params=_cparams(("arbitrary", "arbitrary")),
    )(ya, yp, gates, x, g1, sh2, sc2, *params)


ROW_TILE = 1024


SUBLANES = 8


def _row_copy(src_row_ref, dst_row_ref, sem):
    return pltpu.make_async_copy(src_row_ref, dst_row_ref, sem)


def _dispatch_kernel(h_ref, dest_ref, xin_any, xout_any, idx_smem, sem_idx, sem_row):
    del xin_any
    n_tiles = h_ref.shape[0]
    cp = pltpu.make_async_copy(dest_ref.at[0, 0], idx_smem, sem_idx)
    cp.start()
    cp.wait()

    def issue(g, carry):
        for k in range(SUBLANES):
            r = g * SUBLANES + k
            src = h_ref.at[g, pl.ds(k, 1)]
            _row_copy(src, xout_any.at[pl.ds(idx_smem[2 * r], 1)], sem_row).start(priority=0)
            _row_copy(src, xout_any.at[pl.ds(idx_smem[2 * r + 1], 1)], sem_row).start(priority=1)
        return carry

    lax.fori_loop(0, n_tiles, issue, 0)

    def drain(g, carry):
        for _ in range(2 * SUBLANES):
            _row_copy(h_ref.at[0, pl.ds(0, 1)], xout_any.at[pl.ds(0, 1)], sem_row).wait()
        return carry

    lax.fori_loop(0, n_tiles, drain, 0)


def _dispatch(h2, dest, xin):
    t, d = h2.shape
    td = min(ROW_TILE, t)
    dest3 = dest.reshape(t // td, 1, 2 * td)
    h2 = h2.reshape(t // SUBLANES, SUBLANES, d)
    return pl.pallas_call(
        _dispatch_kernel,
        grid=(t // td,),
        in_specs=[pl.BlockSpec((td // SUBLANES, SUBLANES, d), lambda i: (i, 0, 0)),
                  pl.BlockSpec((1, 1, 2 * td), lambda i: (i, 0, 0)),
                  pl.BlockSpec(memory_space=pl.ANY)],
        out_specs=pl.BlockSpec(memory_space=pl.ANY),
        out_shape=jax.ShapeDtypeStruct(xin.shape, xin.dtype),
        scratch_shapes=[pltpu.SMEM((2 * td,), I32), pltpu.SemaphoreType.DMA(()), pltpu.SemaphoreType.DMA(())],
        input_output_aliases={2: 0},
        compiler_params=_cparams(("arbitrary",)),
    )(h2, dest3, xin)


def _expert_kernel(be_ref, nu_ref, x_ref, wg_ref, wu_ref, wd_ref, o_ref, wgb_ref, wub_ref, wdb_ref):
    i = pl.program_id(0)

    @pl.when((i == 0) | (be_ref[i] != be_ref[jnp.maximum(i - 1, 0)]))
    def _():
        wgb_ref[...] = wg_ref[0, 0].astype(BF16)
        wub_ref[...] = wu_ref[0, 0].astype(BF16)
        wdb_ref[...] = wd_ref[0, 0].astype(BF16)

    @pl.when(i < nu_ref[0])
    def _():
        xb = _unpack_rows(x_ref[...]).astype(BF16)
        gte = _dot(xb, wgb_ref[...])
        up = _dot(xb, wub_ref[...])
        o_ref[...] = _pack_rows(_dot((_silu(gte) * up).astype(BF16), wdb_ref[...]))

    @pl.when(i >= nu_ref[0])
    def _():
        o_ref[...] = jnp.zeros_like(o_ref)


def _experts(xin, block_e, n_used, w_eg, w_eu, w_ed, layer):
    n_rows, dh = xin.shape
    d = 2 * dh
    nb = n_rows // MOE_BLOCK
    grid_spec = pltpu.PrefetchScalarGridSpec(
        num_scalar_prefetch=2,
        grid=(nb,),
        in_specs=[pl.BlockSpec((MOE_BLOCK, dh), lambda i, be, nu: (i, 0)),
                  pl.BlockSpec((1, 1, d, D_EXPERT), lambda i, be, nu: (layer, be[i], 0, 0)),
                  pl.BlockSpec((1, 1, d, D_EXPERT), lambda i, be, nu: (layer, be[i], 0, 0)),
                  pl.BlockSpec((1, 1, D_EXPERT, d), lambda i, be, nu: (layer, be[i], 0, 0))],
        out_specs=pl.BlockSpec((MOE_BLOCK, dh), lambda i, be, nu: (i, 0)),
        scratch_shapes=[pltpu.VMEM((d, D_EXPERT), BF16), pltpu.VMEM((d, D_EXPERT), BF16),
                        pltpu.VMEM((D_EXPERT, d), BF16)],
    )
    return pl.pallas_call(
        _expert_kernel,
        grid_spec=grid_spec,
        out_shape=jax.ShapeDtypeStruct((n_rows, dh), U32),
        compiler_params=_cparams(("arbitrary",)),
    )(block_e, n_used, xin, w_eg, w_eu, w_ed)


def _combine_kernel(dest_ref, rt_ref, x_ref, g2_ref, lng_ref, lnb_ref, yb_any,
                    o_ref, buf_ref, idx_smem, sem_idx, sem_row, *, alpha):
    tc = x_ref.shape[1]
    cp = pltpu.make_async_copy(dest_ref.at[0, 0], idx_smem, sem_idx)
    cp.start()
    cp.wait()

    n_tiles = tc // SUBLANES

    def issue(g, carry):
        for k in range(SUBLANES):
            r = g * SUBLANES + k
            _row_copy(yb_any.at[pl.ds(idx_smem[2 * r], 1)], buf_ref.at[0, g, pl.ds(k, 1)], sem_row).start(priority=0)
            _row_copy(yb_any.at[pl.ds(idx_smem[2 * r + 1], 1)], buf_ref.at[1, g, pl.ds(k, 1)], sem_row).start(priority=1)
        return carry

    lax.fori_loop(0, n_tiles, issue, 0)

    def drain(g, carry):
        for _ in range(2 * SUBLANES):
            _row_copy(yb_any.at[pl.ds(0, 1)], buf_ref.at[0, 0, pl.ds(0, 1)], sem_row).wait()
        return carry

    lax.fori_loop(0, n_tiles, drain, 0)

    rt = rt_ref[0]
    rows = lambda k: _unpack_rows(buf_ref[k].reshape(tc, buf_ref.shape[-1]))
    y = rt[:, 2:3] * rows(0) + rt[:, 3:4] * rows(1)
    o_ref[0] = _layer_norm(alpha * x_ref[0] + g2_ref[0] * y, lng_ref[...], lnb_ref[...])


def _combine(dest, route, x, g2, lp, yb, alpha):
    b, l, d = x.shape
    tc = min(ROW_TILE, l)
    npb = l // tc
    dest3 = dest.reshape(b * npb, 1, 2 * tc)
    par = lambda a: pl.BlockSpec(a.shape, lambda i, j: (0,) * a.ndim)
    return pl.pallas_call(
        functools.partial(_combine_kernel, alpha=alpha),
        grid=(b, npb),
        in_specs=[pl.BlockSpec((1, 1, 2 * tc), lambda i, j: (i * npb + j, 0, 0)),
                  pl.BlockSpec((1, tc, LANES), lambda i, j: (i, j, 0)),
                  pl.BlockSpec((1, tc, d), lambda i, j: (i, j, 0)),
                  pl.BlockSpec((1, 1, d), lambda i, j: (i, 0, 0)),
                  par(lp["ln2_g"]), par(lp["ln2_b"]),
                  pl.BlockSpec(memory_space=pl.ANY)],
        out_specs=pl.BlockSpec((1, tc, d), lambda i, j: (i, j, 0)),
        out_shape=jax.ShapeDtypeStruct((b, l, d), F32),
        scratch_shapes=[pltpu.VMEM((2, tc // SUBLANES, SUBLANES, d // 2), U32), pltpu.SMEM((2 * tc,), I32),
                        pltpu.SemaphoreType.DMA(()), pltpu.SemaphoreType.DMA(())],
        compiler_params=_cparams(("arbitrary", "arbitrary")),
    )(dest3, route, x, g2, lp["ln2_g"], lp["ln2_b"], yb)


def _moe_plan(routes, counts):
    cnt = counts[0, ROUTE_LANE0:ROUTE_LANE0 + MOE_EXPERTS].astype(I32)
    padded = (cnt + MOE_BLOCK - 1) // MOE_BLOCK * MOE_BLOCK
    pad_end = jnp.cumsum(padded)
    pad_start = pad_end - padded
    n_assign = sum(r.shape[0] * r.shape[1] for r in routes) * 2
    n_blocks = -(-n_assign // MOE_BLOCK) + MOE_EXPERTS
    first_row = jnp.arange(n_blocks, dtype=I32)[:, None] * MOE_BLOCK
    block_e = jnp.minimum(jnp.sum((pad_end[None, :] <= first_row).astype(I32), axis=1), MOE_EXPERTS - 1)
    n_used = (pad_end[-1:] // MOE_BLOCK).astype(I32)
    eidx = jnp.arange(MOE_EXPERTS, dtype=I32)
    dests = []
    for r in routes:
        e = r[..., 0:2].astype(I32)
        rank = r[..., 4:6].astype(I32)
        start = jnp.sum(jnp.where(e[..., None] == eidx, pad_start, 0), axis=-1)
        dests.append((start + rank).reshape(-1, 2))
    return dests, block_e, n_used, n_blocks * MOE_BLOCK


def _layer_params(i, p):
    w_in = p["w_in"][i]
    s0, s1, s2, s3 = D_INNER, D_INNER + D_XBC, D_INNER + D_XBC + 2 * HEADS, D_INNER + D_XBC + 2 * HEADS + D_MODEL
    pad = LANES - 2 * HEADS
    wdt = jnp.pad(w_in[:, s1:s2], ((0, 0), (0, pad)))
    w_r = jnp.concatenate([p["w_router_group"][i],
                           jnp.transpose(p["w_router_expert"][i], (1, 0, 2)).reshape(D_MODEL, MOE_EXPERTS)], axis=1)
    w_r = jnp.pad(w_r, ((0, 0), (0, LANES - w_r.shape[1])))
    w_r_hi = w_r.astype(BF16)
    b_r = jnp.concatenate([p["b_router_group"][i], p["b_router_expert"][i].reshape(-1)])
    return {
        "w_in": tuple(w.astype(BF16) for w in (w_in[:, :s0], w_in[:, s0:s1], w_in[:, s2:s3], w_in[:, s3:], wdt)),
        "conv_w": p["conv_w"][i], "conv_b": p["conv_b"][i][None],
        "dt_bias": jnp.pad(p["dt_bias"][i].reshape(1, -1), ((0, 0), (0, pad))),
        "a_log": jnp.pad(p["a_log"][i].reshape(1, -1), ((0, 0), (0, pad))),
        "d_skip_x": jnp.repeat(p["d_skip"][i], HEADDIM)[None],
        "ssd_norm_g": p["ssd_norm_g"][i][None],
        "pool_w": p["pool_w"][i].astype(BF16), "pool_scale": p["pool_scale"][i][None],
        "b_gate": p["b_gate"][i][None],
        "w_branch_a": p["w_branch_a"][i].astype(BF16), "w_branch_b": p["w_branch_b"][i].astype(BF16),
        "w_out": p["w_out"][i].astype(BF16),
        "ln1_g": p["ln1_g"][i][None], "ln1_b": p["ln1_b"][i][None],
        "ln2_g": p["ln2_g"][i][None], "ln2_b": p["ln2_b"][i][None],
        "w_r_hi": w_r_hi, "w_r_lo": (w_r - w_r_hi.astype(F32)).astype(BF16),
        "b_r": jnp.pad(b_r, (0, LANES - b_r.shape[0]))[None],
    }


def _head_expanders():
    col_head = jnp.arange(D_INNER, dtype=I32) // HEADDIM
    row = jnp.arange(LANES, dtype=I32)[:, None] % (2 * HEADS)
    return (row == col_head).astype(BF16), (row == col_head + HEADS).astype(BF16)


def kernel(x, c, ctx, c_ctx, w_ada, b_ada, w_in, b_gate, conv_w, conv_b, dt_bias, a_log, d_skip, ssd_norm_g, pool_w, pool_scale, w_branch_a, w_branch_b, w_out, ln1_g, ln1_b, ln2_g, ln2_b, w_router_group, b_router_group, w_router_expert, b_router_expert, w_expert_gate, w_expert_up, w_expert_down):
    p = dict(w_in=w_in, b_gate=b_gate, conv_w=conv_w, conv_b=conv_b, dt_bias=dt_bias, a_log=a_log, d_skip=d_skip,
             ssd_norm_g=ssd_norm_g, pool_w=pool_w, pool_scale=pool_scale, w_branch_a=w_branch_a,
             w_branch_b=w_branch_b, w_out=w_out, ln1_g=ln1_g, ln1_b=ln1_b, ln2_g=ln2_g, ln2_b=ln2_b,
             w_router_group=w_router_group, b_router_group=b_router_group, w_router_expert=w_router_expert,
             b_router_expert=b_router_expert, w_expert_gate=w_expert_gate, w_expert_up=w_expert_up,
             w_expert_down=w_expert_down)
    b, l, d = x.shape
    depth = w_ada.shape[0]
    alpha = (2.0 * depth) ** DEPTH_ALPHA_POW
    n_mod = -(-(b + 1) // 8) * 8
    cvec = jnp.zeros((n_mod, d), F32).at[:b].set(c).at[b].set(c_ctx)
    mods = _ada_mods(cvec, w_ada, b_ada)
    r_f, r_b = _head_expanders()
    zero_state = jnp.zeros((b, STATE, D_INNER), F32)
    zero_cnt = jnp.zeros((1, LANES), F32)
    xl, xc = x, ctx
    for i in range(depth):
        last = i == depth - 1
        lp = _layer_params(i, p)
        mod_l = [mods[i, :b, k * d:(k + 1) * d][:, None, :] for k in range(6)]
        mod_c = [jnp.broadcast_to(mods[i, b, k * d:(k + 1) * d], (b, 1, d)) for k in range(6)]

        zc, act_c, uc, gc, dtc = _in_proj(xc, mod_c[0], mod_c[1], lp["w_in"], lp["conv_w"], lp["conv_b"])
        if last:
            (s_f,) = _ssd_state_scan(act_c, dtc, zero_state, lp, r_f, rev=False, emit=False)
            (s_b,) = _ssd_state_scan(act_c, dtc, zero_state, lp, r_b, rev=True, emit=False)
        else:
            stb_c, s_b = _ssd_state_scan(act_c, dtc, zero_state, lp, r_b, rev=True, emit=True)
            ya_c, s_f = _ssd_main(act_c, dtc, zc, stb_c, zero_state, lp, r_f, r_b)
            yp_c = _pool_branch(uc, lp, False)

        zl, act_l, ul, gl, dtl = _in_proj(xl, mod_l[0], mod_l[1], lp["w_in"], lp["conv_w"], lp["conv_b"])
        stb_l, _ = _ssd_state_scan(act_l, dtl, s_b, lp, r_b, rev=True, emit=True)
        ya_l, _ = _ssd_main(act_l, dtl, zl, stb_l, s_f, lp, r_f, r_b)
        yp_l = _pool_branch(ul, lp, True)
        xl, h2_l, rt_l, cnt = _merge(ya_l, yp_l, gl, xl, mod_l[2], mod_l[3], mod_l[4], lp, zero_cnt, alpha)
        streams = [(h2_l, rt_l)]
        if not last:
            xc, h2_c, rt_c, cnt = _merge(ya_c, yp_c, gc, xc, mod_c[2], mod_c[3], mod_c[4], lp, cnt, alpha)
            streams.append((h2_c, rt_c))

        dests, block_e, n_used, n_rows = _moe_plan([rt for _, rt in streams], cnt)
        xin = jnp.zeros((n_rows, d // 2), U32)
        for (h2, _), dest in zip(streams, dests):
            xin = _dispatch(h2.reshape(-1, d // 2), dest, xin)
        yb = _experts(xin, block_e, n_used, w_expert_gate, w_expert_up, w_expert_down, i)
        xl = _combine(dests[0], rt_l, xl, mod_l[5], lp, yb, alpha)
        if not last:
            xc = _combine(dests[1], rt_c, xc, mod_c[5], lp, yb, alpha)
    return xl
```

```python
import functools
import math

import jax
import jax.numpy as jnp
from jax import lax
from jax.experimental import pallas as pl
from jax.experimental.pallas import tpu as pltpu

F32 = jnp.float32
BF16 = jnp.bfloat16
I32 = jnp.int32

D_MODEL = 1024
D_INNER = 2048
HEADS = 32
HEADDIM = 64
GROUPS = 4
STATE = 128
D_BC = GROUPS * STATE
D_XBC = D_INNER + 2 * D_BC
CONV_K = 5
CHUNK = 128
GRID_W = 64
POOL_WINDOWS = (2, 4, 8, 16)
POOL_GROUP_DIM = 256
MOE_GROUPS = 4
MOE_EPG = 8
MOE_EXPERTS = 32
D_EXPERT = 512
MOE_BLOCK = 512
DEPTH_ALPHA_POW = 0.25
NORM_EPS = 1e-5
LANES = 128
SUBLANES = 8
ROUTE_LANE0 = MOE_GROUPS
VMEM_LIMIT = 56 * 1024 * 1024


def _cparams(sem, vmem=VMEM_LIMIT):
    return pltpu.CompilerParams(dimension_semantics=sem, vmem_limit_bytes=vmem)


def _sigmoid(v):
    return 0.5 + 0.5 * jnp.tanh(0.5 * v)


def _silu(v):
    h = 0.5 * v
    return h + h * jnp.tanh(h)


def _layer_norm(v, g, b):
    mu = jnp.mean(v, axis=-1, keepdims=True)
    d = v - mu
    var = jnp.mean(d * d, axis=-1, keepdims=True)
    return d * lax.rsqrt(var + NORM_EPS) * g + b


def _split_bf16(v):
    hi = v.astype(BF16)
    lo = (v - hi.astype(F32)).astype(BF16)
    return hi, lo


def _dot(a, b):
    return jnp.dot(a, b, preferred_element_type=F32)


U32 = jnp.uint32
HI16 = 0xFFFF0000


def _pack_rows(v):
    half = v.shape[1] // 2
    lo = lax.bitcast_convert_type(v[:, :half].astype(BF16).astype(F32), U32)
    hi = lax.bitcast_convert_type(v[:, half:].astype(BF16).astype(F32), U32)
    return (lo >> 16) | (hi & U32(HI16))


def _unpack_rows(w):
    lo = lax.bitcast_convert_type(w << 16, F32)
    hi = lax.bitcast_convert_type(w & U32(HI16), F32)
    return jnp.concatenate([lo, hi], axis=1)


def _ada_kernel(c_ref, w_ref, b_ref, o_ref):
    s = _silu(c_ref[...]).astype(BF16)
    o_ref[0] = _dot(s, w_ref[0].astype(BF16)) + b_ref[0]


def _ada_mods(cvec, w_ada, b_ada):
    depth, d, n = w_ada.shape
    r = cvec.shape[0]
    tn = D_MODEL
    return pl.pallas_call(
        _ada_kernel,
        grid=(depth, n // tn),
        in_specs=[pl.BlockSpec((r, d), lambda l, j: (0, 0)),
                  pl.BlockSpec((1, d, tn), lambda l, j: (l, 0, j)),
                  pl.BlockSpec((1, 1, tn), lambda l, j: (l, 0, j))],
        out_specs=pl.BlockSpec((1, r, tn), lambda l, j: (l, 0, j)),
        out_shape=jax.ShapeDtypeStruct((depth, r, n), F32),
        compiler_params=_cparams(("arbitrary", "arbitrary")),
    )(cvec, w_ada, b_ada.reshape(depth, 1, n))


CONV_HALO = 8
CONV_SLAB = 512


def _inproj_kernel(xp_ref, x_ref, xn_ref, sh_ref, sc_ref, wz_ref, wx_ref, wu_ref, wg_ref, wdt_ref, cw_ref, cb_ref,
                   z_ref, act_ref, u_ref, g_ref, dt_ref):
    j = pl.program_id(1)
    nj = pl.num_programs(1)
    tm = x_ref.shape[1]
    n = tm + 2 * CONV_HALO
    mod = lambda v: (v * (1.0 + sc_ref[0]) + sh_ref[0]).astype(BF16)
    h = mod(x_ref[0])
    h_ext = mod(jnp.concatenate([xp_ref[0], x_ref[0], xn_ref[0]], axis=0))

    row = lax.broadcasted_iota(I32, (n, 1), 0)
    inside = ((row >= CONV_HALO) | (j > 0)) & ((row < CONV_HALO + tm) | (j < nj - 1))

    z_ref[0] = _dot(h, wz_ref[...]).astype(BF16)
    u_ref[0] = _dot(h, wu_ref[...]).astype(BF16)
    g_ref[0] = _dot(h, wg_ref[...]).astype(BF16)
    dt_ref[0] = _dot(h, wdt_ref[...])
    down = lambda v: pltpu.roll(v, 1, 0)
    up = lambda v: pltpu.roll(v, n - 1, 0)
    for s in range(D_XBC // CONV_SLAB):
        cols = slice(s * CONV_SLAB, (s + 1) * CONV_SLAB)
        e = jnp.where(inside, _dot(h_ext, wx_ref[:, cols]), 0.0)
        y = [e * cw_ref[k:k + 1, cols] for k in range(CONV_K)]
        acc = y[2] + down(y[1] + down(y[0])) + up(y[3] + up(y[4]))
        act_ref[0, :, cols] = _silu(acc[CONV_HALO:CONV_HALO + tm] + cb_ref[:, cols]).astype(BF16)


def _in_proj(x, shift, scale, wts, conv_w, conv_b):
    b, l, d = x.shape
    tm = min(512, l)
    hpt = tm // CONV_HALO
    nh = l // CONV_HALO
    wz, wx, wu, wg, wdt = wts
    tok = lambda n: pl.BlockSpec((1, tm, n), lambda i, j: (i, j, 0))
    mod = pl.BlockSpec((1, 1, d), lambda i, j: (i, 0, 0))
    wspec = lambda w: pl.BlockSpec(w.shape, lambda i, j: (0, 0), pipeline_mode=pl.Buffered(1))
    par = lambda a: pl.BlockSpec(a.shape, lambda i, j: (0, 0))
    halo_prev = pl.BlockSpec((1, CONV_HALO, d), lambda i, j: (i, jnp.maximum(j * hpt - 1, 0), 0))
    halo_next = pl.BlockSpec((1, CONV_HALO, d), lambda i, j: (i, jnp.minimum((j + 1) * hpt, nh - 1), 0))
    return pl.pallas_call(
        _inproj_kernel,
        grid=(b, l // tm),
        in_specs=[halo_prev, tok(d), halo_next, mod, mod, wspec(wz), wspec(wx), wspec(wu), wspec(wg), wspec(wdt),
                  par(conv_w), par(conv_b)],
        out_specs=[tok(D_INNER), tok(D_XBC), tok(D_MODEL), tok(2 * D_MODEL), tok(LANES)],
        out_shape=[jax.ShapeDtypeStruct((b, l, D_INNER), BF16),
                   jax.ShapeDtypeStruct((b, l, D_XBC), BF16),
                   jax.ShapeDtypeStruct((b, l, D_MODEL), BF16),
                   jax.ShapeDtypeStruct((b, l, 2 * D_MODEL), BF16),
                   jax.ShapeDtypeStruct((b, l, LANES), F32)],
        compiler_params=_cparams(("arbitrary", "arbitrary")),
    )(x, x, x, shift, scale, wz, wx, wu, wg, wdt, conv_w, conv_b)


def _dt_prep(dt_raw, dtb_ref, alog_ref):
    q = CHUNK
    lane = lax.broadcasted_iota(I32, (q, LANES), 1)
    raw = dt_raw + dtb_ref[...]
    dt = jnp.maximum(raw, 0.0) + jnp.log1p(jnp.exp(-jnp.abs(raw)))
    dt = jnp.where(lane < 2 * HEADS, dt, 0.0)
    a = dt * (-jnp.exp(alog_ref[...]))
    ii = lax.broadcasted_iota(I32, (q, q), 0)
    jj = lax.broadcasted_iota(I32, (q, q), 1)
    tri = (jj <= ii).astype(BF16)
    a_hi, a_lo = _split_bf16(a)
    acs = _dot(tri, a_hi) + _dot(tri, a_lo)
    tot = acs[q - 1:q, :]
    return dt, a, acs, tot


def _expand(v, lane0, r_ref):
    return _dot(_expand_src(v, lane0), r_ref[...])


def _expand_src(v, lane0):
    lane = lax.broadcasted_iota(I32, v.shape, 1)
    vm = jnp.where((lane >= lane0) & (lane < lane0 + HEADS), v, 0.0)
    hi = vm.astype(BF16).astype(F32)
    return (hi + pltpu.roll(vm - hi, 2 * HEADS, 1)).astype(BF16)


def _state_update(st_ref, bm, xw_bf, decay_x):
    gw = D_INNER // GROUPS
    for g in range(GROUPS):
        bgt = bm[:, g * STATE:(g + 1) * STATE].T.astype(BF16)
        cols = slice(g * gw, (g + 1) * gw)
        st_ref[:, cols] = st_ref[:, cols] * decay_x[:, cols] + _dot(bgt, xw_bf[:, cols])


SCAN_CHUNKS = 8


def _ssd_state_kernel(*refs, rev, emit, nch):
    act_ref, dt_ref, h0_ref, dtb_ref, alog_ref, r_ref = refs[:6]
    if emit:
        stout_ref, sfin_ref, st_ref = refs[6:]
    else:
        sfin_ref, st_ref = refs[6:]
    s = pl.program_id(1)
    ns = pl.num_programs(1)

    @pl.when(s == 0)
    def _():
        st_ref[...] = h0_ref[0]

    for k in range(nch):
        cc = (nch - 1 - k) if rev else k
        rows = slice(cc * CHUNK, (cc + 1) * CHUNK)
        if emit:
            stout_ref[0, cc] = st_ref[...].astype(BF16)
        xs = act_ref[0, rows, :D_INNER].astype(F32)
        bm = act_ref[0, rows, D_INNER:D_INNER + D_BC].astype(F32)
        dt, a, acs, tot = _dt_prep(dt_ref[0, rows, :], dtb_ref, alog_ref)
        if rev:
            w = jnp.exp(acs - a) * dt
            lane0 = HEADS
        else:
            w = jnp.exp(tot - acs) * dt
            lane0 = 0
        w_x = _expand(w, lane0, r_ref)
        dec_x = _expand(jnp.broadcast_to(jnp.exp(tot), (16, LANES)), lane0, r_ref)[0:1]
        _state_update(st_ref, bm, (w_x * xs).astype(BF16), dec_x)

    @pl.when(s == ns - 1)
    def _():
        sfin_ref[0] = st_ref[...]


def _ssd_state_scan(act, dt, h0, lp, r_mat, *, rev, emit):
    b, l, _ = act.shape
    nc = l // CHUNK
    nch = min(SCAN_CHUNKS, nc)
    ns = nc // nch
    seq = (lambda s: ns - 1 - s) if rev else (lambda s: s)
    par = lambda shape: pl.BlockSpec(shape, lambda i, s: (0,) * len(shape))
    in_specs = [
        pl.BlockSpec((1, nch * CHUNK, D_XBC), lambda i, s: (i, seq(s), 0)),
        pl.BlockSpec((1, nch * CHUNK, LANES), lambda i, s: (i, seq(s), 0)),
        pl.BlockSpec((1, STATE, D_INNER), lambda i, s: (i, 0, 0)),
        par((1, LANES)), par((1, LANES)), par((LANES, D_INNER)),
    ]
    out_specs = [pl.BlockSpec((1, STATE, D_INNER), lambda i, s: (i, 0, 0))]
    out_shape = [jax.ShapeDtypeStruct((b, STATE, D_INNER), F32)]
    if emit:
        out_specs = [pl.BlockSpec((1, nch, STATE, D_INNER), lambda i, s: (i, seq(s), 0, 0))] + out_specs
        out_shape = [jax.ShapeDtypeStruct((b, nc, STATE, D_INNER), BF16)] + out_shape
    return pl.pallas_call(
        functools.partial(_ssd_state_kernel, rev=rev, emit=emit, nch=nch),
        grid=(b, ns),
        in_specs=in_specs, out_specs=out_specs, out_shape=out_shape,
        scratch_shapes=[pltpu.VMEM((STATE, D_INNER), F32)],
        compiler_params=_cparams(("arbitrary", "arbitrary")),
    )(act, dt, h0, lp["dt_bias"], lp["a_log"], r_mat)


MAIN_CHUNKS = 4


def _ssd_main_kernel(act_ref, dt_ref, z_ref, stb_ref, h0_ref,
                     dtb_ref, alog_ref, dsk_ref, ng_ref, rf_ref, rb_ref,
                     y_ref, sfin_ref, st_ref, ys_ref, *, nch):
    c = pl.program_id(1)
    nc = pl.num_programs(1)

    @pl.when(c == 0)
    def _():
        st_ref[...] = h0_ref[0]

    for k in range(nch):
        rows = pl.ds(k * CHUNK, CHUNK)
        _ssd_main_chunk(act_ref.at[0, rows], dt_ref.at[0, rows], z_ref.at[0, rows], stb_ref.at[0, k],
                        dtb_ref, alog_ref, dsk_ref, ng_ref, rf_ref, rb_ref, y_ref.at[0, rows], st_ref, ys_ref)

    @pl.when(c == nc - 1)
    def _():
        sfin_ref[0] = st_ref[...]


def _ssd_main_chunk(act_ref, dt_ref, z_ref, stb_ref, dtb_ref, alog_ref, dsk_ref, ng_ref, rf_ref, rb_ref,
                    y_ref, st_ref, ys_ref):
    q = CHUNK
    dt, a, acs, tot = _dt_prep(dt_ref[...], dtb_ref, alog_ref)
    eb = acs - a

    ef_s = _expand_src(jnp.exp(acs), 0)
    eb_s = _expand_src(jnp.exp(tot - eb), HEADS)
    tf_s = _expand_src(jnp.exp(tot - acs) * dt, 0)
    dec_s = _expand_src(jnp.broadcast_to(jnp.exp(tot), (16, LANES)), 0)

    lane = lax.broadcasted_iota(I32, (q, LANES), 1)
    ldt = jnp.log(dt)
    qm = jnp.where(lane < HEADS, acs - ldt, jnp.where(lane < 2 * HEADS, eb + ldt, pltpu.roll(dt, 2 * HEADS, 1)))
    qt = qm.T
    ii = lax.broadcasted_iota(I32, (q, q), 0)
    jj = lax.broadcasted_iota(I32, (q, q), 1)
    lower = jj <= ii
    diag = jj == ii
    lane2 = lax.broadcasted_iota(I32, (q, LANES), 1)
    left = lane2 < HEADDIM

    gw = D_INNER // GROUPS
    hpg = HEADS // GROUPS
    for g in range(GROUPS):
        bg = act_ref[:, D_INNER + g * STATE:D_INNER + (g + 1) * STATE]
        cg = act_ref[:, D_INNER + D_BC + g * STATE:D_INNER + D_BC + (g + 1) * STATE]
        cb = lax.dot_general(cg, bg, (((1,), (1,)), ((), ())), preferred_element_type=F32)
        cols = slice(g * gw, (g + 1) * gw)
        y_off = (_dot(ef_s, rf_ref[:, cols]) * _dot(cg, st_ref[:, cols].astype(BF16))
                 + _dot(eb_s, rb_ref[:, cols]) * _dot(cg, stb_ref[:, cols]))
        for pr in range(hpg // 2):
            ms = []
            for hh in range(2):
                h = g * hpg + pr * 2 + hh
                afc = acs[:, h:h + 1]
                afr = qt[h:h + 1, :]
                ebc = eb[:, HEADS + h:HEADS + h + 1]
                ebr = qt[HEADS + h:HEADS + h + 1, :]
                wb = qt[3 * HEADS + h:3 * HEADS + h + 1, :]
                arg = jnp.where(lower, afc - afr, ebr - ebc)
                ms.append((cb * (jnp.exp(arg) + jnp.where(diag, wb, 0.0))).astype(BF16))
            lhs = jnp.concatenate(ms, axis=1)
            c0 = g * gw + pr * 2 * HEADDIM
            xp = act_ref[:, c0:c0 + 2 * HEADDIM]
            zero = jnp.zeros_like(xp)
            rhs = jnp.concatenate([jnp.where(left, xp, zero), jnp.where(left, zero, xp)], axis=0)
            y_pair = _dot(lhs, rhs) + y_off[:, pr * 2 * HEADDIM:(pr + 1) * 2 * HEADDIM]
            y_pair = y_pair + dsk_ref[:, c0:c0 + 2 * HEADDIM] * xp.astype(F32)
            ys_ref[:, c0:c0 + 2 * HEADDIM] = y_pair

        xw = (_dot(tf_s, rf_ref[:, cols]) * act_ref[:, cols].astype(F32)).astype(BF16)
        bgt = bg.astype(F32).T.astype(BF16)
        st_ref[:, cols] = st_ref[:, cols] * _dot(dec_s, rf_ref[:, cols])[0:1] + _dot(bgt, xw)

    y = ys_ref[...]
    yz = y * _silu(z_ref[...].astype(F32))
    ms2 = jnp.mean(yz * yz, axis=-1, keepdims=True)
    y_ref[...] = (yz * lax.rsqrt(ms2 + NORM_EPS) * ng_ref[...]).astype(BF16)


def _ssd_main(act, dt, z, stb, h0, lp, r_f, r_b):
    b, l, _ = act.shape
    nc = l // CHUNK
    nch = min(MAIN_CHUNKS, nc)
    rows = nch * CHUNK
    par = lambda shape: pl.BlockSpec(shape, lambda i, s: (0,) * len(shape))
    in_specs = [
        pl.BlockSpec((1, rows, D_XBC), lambda i, s: (i, s, 0)),
        pl.BlockSpec((1, rows, LANES), lambda i, s: (i, s, 0)),
        pl.BlockSpec((1, rows, D_INNER), lambda i, s: (i, s, 0)),
        pl.BlockSpec((1, nch, STATE, D_INNER), lambda i, s: (i, s, 0, 0)),
        pl.BlockSpec((1, STATE, D_INNER), lambda i, s: (i, 0, 0)),
        par((1, LANES)), par((1, LANES)),
        par((1, D_INNER)), par((1, D_INNER)), par((LANES, D_INNER)), par((LANES, D_INNER)),
    ]
    return pl.pallas_call(
        functools.partial(_ssd_main_kernel, nch=nch),
        grid=(b, nc // nch),
        in_specs=in_specs,
        out_specs=[pl.BlockSpec((1, rows, D_INNER), lambda i, s: (i, s, 0)),
                   pl.BlockSpec((1, STATE, D_INNER), lambda i, s: (i, 0, 0))],
        out_shape=[jax.ShapeDtypeStruct((b, l, D_INNER), BF16),
                   jax.ShapeDtypeStruct((b, STATE, D_INNER), F32)],
        scratch_shapes=[pltpu.VMEM((STATE, D_INNER), F32), pltpu.VMEM((CHUNK, D_INNER), F32)],
        compiler_params=_cparams(("arbitrary", "arbitrary")),
    )(act, dt, z, stb, h0, lp["dt_bias"], lp["a_log"], lp["d_skip_x"], lp["ssd_norm_g"], r_f, r_b)


def _pool_kernel(*refs, width, vertical, halo, hblock):
    if vertical:
        prev_ref, cur_ref, next_ref, pw_ref, ps_ref, o_ref = refs
    else:
        cur_ref, pw_ref, ps_ref, o_ref = refs
    t = pl.program_id(1)
    nt = pl.num_programs(1)
    tp = cur_ref.shape[1]
    n_tok = nt * tp
    shift = int(math.log2(width))
    idx = lax.broadcasted_iota(I32, (tp, 1), 0) + t * tp
    colpos = idx & (width - 1)
    rowpos = idx >> shift
    n_rows = n_tok // width
    bi = lax.broadcasted_iota(I32, (hblock, hblock), 0)
    bj = lax.broadcasted_iota(I32, (hblock, hblock), 1)
    same_row = (bi >> shift) == (bj >> shift)
    cur = cur_ref[0].astype(F32)
    if vertical:
        prev = jnp.where(t > 0, prev_ref[0].astype(F32), 0.0)
        nxt = jnp.where(t < nt - 1, next_ref[0].astype(F32), 0.0)
    for gi, k in enumerate(POOL_WINDOWS):
        cols = slice(gi * POOL_GROUP_DIM, (gi + 1) * POOL_GROUP_DIM)
        ug = cur[:, cols]
        lo = k // 2
        if vertical:
            above, below = lo * width, (k - 1 - lo) * width
            parts = [prev[halo - above:, cols], ug] + ([nxt[:below, cols]] if below else [])
            e = jnp.concatenate(parts, axis=0)
            step = width
            m = 1
            while m < k:
                n = e.shape[0] - step
                e = e[0:n] + e[step:step + n]
                step *= 2
                m *= 2
            s = e
            cnt_r = (jnp.minimum(rowpos - lo + k, n_rows) - jnp.maximum(rowpos - lo, 0)).astype(F32)
        else:
            s = ug
            cnt_r = jnp.ones((tp, 1), F32)
        band = (same_row & (bj - bi >= -lo) & (bj - bi < k - lo)).astype(BF16)
        s_bf = s.astype(BF16)
        hs = [_dot(band, s_bf[r * hblock:(r + 1) * hblock]) for r in range(tp // hblock)]
        hsum = hs[0] if len(hs) == 1 else jnp.concatenate(hs, axis=0)
        cnt_c = (jnp.minimum(colpos - lo + k, width) - jnp.maximum(colpos - lo, 0)).astype(F32)
        mean = hsum / (cnt_r * cnt_c)
        y = _dot((mean - ug).astype(BF16), pw_ref[gi])
        o_ref[0, :, cols] = (y * ps_ref[:, cols]).astype(BF16)


def _pool_branch(u, lp, on_grid):
    b, l, d = u.shape
    pw, ps = lp["pool_w"], lp["pool_scale"]
    if on_grid:
        width, halo = GRID_W, (max(POOL_WINDOWS) // 2) * GRID_W
        tp = min(1024, l)
        hb = tp // halo
        nhb = l // halo
        in_specs = [pl.BlockSpec((1, halo, d), lambda i, t: (i, jnp.maximum(t * hb - 1, 0), 0)),
                    pl.BlockSpec((1, tp, d), lambda i, t: (i, t, 0)),
                    pl.BlockSpec((1, halo, d), lambda i, t: (i, jnp.minimum((t + 1) * hb, nhb - 1), 0))]
        args = (u, u, u)
        kern = functools.partial(_pool_kernel, width=width, vertical=True, halo=halo, hblock=LANES)
    else:
        tp = l
        in_specs = [pl.BlockSpec((1, tp, d), lambda i, t: (i, t, 0))]
        args = (u,)
        kern = functools.partial(_pool_kernel, width=l, vertical=False, halo=0, hblock=l)
    in_specs += [pl.BlockSpec(pw.shape, lambda i, t: (0, 0, 0)), pl.BlockSpec(ps.shape, lambda i, t: (0, 0))]
    return pl.pallas_call(
        kern,
        grid=(b, l // tp),
        in_specs=in_specs,
        out_specs=pl.BlockSpec((1, tp, d), lambda i, t: (i, t, 0)),
        out_shape=jax.ShapeDtypeStruct((b, l, d), BF16),
        compiler_params=_cparams(("arbitrary", "arbitrary")),
    )(*args, pw, ps)


def _merge_kernel(ya_ref, yp_ref, gt_ref, x_ref, g1_ref, sh2_ref, sc2_ref, bg_ref,
                  wa_ref, wb_ref, wo_ref, lng_ref, lnb_ref, wrh_ref, wrl_ref, br_ref, cin_ref,
                  xo_ref, h2_ref, rt_ref, cout_ref, cnt_ref, *, alpha):
    i = pl.program_id(0)
    j = pl.program_id(1)
    tm = x_ref.shape[1]

    @pl.when((i == 0) & (j == 0))
    def _():
        cnt_ref[...] = cin_ref[...]

    gate = _sigmoid(gt_ref[0].astype(F32) + bg_ref[...])
    merged = (gate[:, :D_MODEL] * _dot(ya_ref[0], wa_ref[...])
              + gate[:, D_MODEL:] * _dot(yp_ref[0], wb_ref[...]))
    out = _dot(merged.astype(BF16), wo_ref[...])
    xn = _layer_norm(alpha * x_ref[0] + g1_ref[0] * out, lng_ref[...], lnb_ref[...])
    xo_ref[0] = xn
    h2 = xn * (1.0 + sc2_ref[0]) + sh2_ref[0]
    h2_ref[0] = _pack_rows(h2)

    h_hi, h_lo = _split_bf16(h2)
    logits = _dot(h_hi, wrh_ref[...]) + _dot(h_lo, wrh_ref[...]) + _dot(h_hi, wrl_ref[...]) + br_ref[...]
    lane = lax.broadcasted_iota(I32, (tm, LANES), 1)
    neg = jnp.float32(-jnp.inf)
    lg = jnp.where(lane < MOE_GROUPS, logits, neg)
    mg = jnp.max(lg, axis=-1, keepdims=True)
    grp = jnp.min(jnp.where(lg == mg, lane, LANES), axis=-1, keepdims=True)
    p_grp = 1.0 / jnp.sum(jnp.exp(lg - mg), axis=-1, keepdims=True)
    lo_lane = ROUTE_LANE0 + grp * MOE_EPG
    le = jnp.where((lane >= lo_lane) & (lane < lo_lane + MOE_EPG), logits, neg)
    v1 = jnp.max(le, axis=-1, keepdims=True)
    i1 = jnp.min(jnp.where(le == v1, lane, LANES), axis=-1, keepdims=True)
    le2 = jnp.where(lane == i1, neg, le)
    v2 = jnp.max(le2, axis=-1, keepdims=True)
    i2 = jnp.min(jnp.where(le2 == v2, lane, LANES), axis=-1, keepdims=True)
    e2 = jnp.exp(v2 - v1)
    w1 = p_grp / (1.0 + e2)
    w2 = p_grp * e2 / (1.0 + e2)

    oh1 = (lane == i1).astype(F32)
    oh2 = (lane == i2).astype(F32)
    oh = oh1 + oh2
    ri = lax.broadcasted_iota(I32, (tm, tm), 0)
    rj = lax.broadcasted_iota(I32, (tm, tm), 1)
    before = _dot((rj < ri).astype(BF16), oh.astype(BF16)) + cnt_ref[...]
    rank1 = jnp.sum(oh1 * before, axis=-1, keepdims=True)
    rank2 = jnp.sum(oh2 * before, axis=-1, keepdims=True)
    cnt_ref[...] = cnt_ref[...] + jnp.sum(oh, axis=0, keepdims=True)
    cout_ref[...] = cnt_ref[...]

    slab = jnp.where(lane == 0, (i1 - ROUTE_LANE0).astype(F32),
           jnp.where(lane == 1, (i2 - ROUTE_LANE0).astype(F32),
           jnp.where(lane == 2, w1,
           jnp.where(lane == 3, w2,
           jnp.where(lane == 4, rank1,
           jnp.where(lane == 5, rank2, 0.0))))))
    rt_ref[0] = slab


def _merge(ya, yp, gates, x, g1, sh2, sc2, lp, cnt_in, alpha):
    b, l, d = x.shape
    tm = min(512, l)
    tok = lambda n: pl.BlockSpec((1, tm, n), lambda i, j: (i, j, 0))
    mod = pl.BlockSpec((1, 1, d), lambda i, j: (i, 0, 0))
    par = lambda a: pl.BlockSpec(a.shape, lambda i, j: (0,) * a.ndim)
    params = (lp["b_gate"], lp["w_branch_a"], lp["w_branch_b"], lp["w_out"], lp["ln1_g"], lp["ln1_b"],
              lp["w_r_hi"], lp["w_r_lo"], lp["b_r"], cnt_in)
    return pl.pallas_call(
        functools.partial(_merge_kernel, alpha=alpha),
        grid=(b, l // tm),
        in_specs=[tok(D_INNER), tok(d), tok(2 * d), tok(d), mod, mod, mod] + [par(a) for a in params],
        out_specs=[tok(d), tok(d // 2), tok(LANES), pl.BlockSpec((1, LANES), lambda i, j: (0, 0))],
        out_shape=[jax.ShapeDtypeStruct((b, l, d), F32), jax.ShapeDtypeStruct((b, l, d // 2), U32),
                   jax.ShapeDtypeStruct((b, l, LANES), F32), jax.ShapeDtypeStruct((1, LANES), F32)],
        scratch_shapes=[pltpu.VMEM((1, LANES), F32)],
        compiler_params=_cparams(("arbitrary", "arbitrary")),
    )(ya, yp, gates, x, g1, sh2, sc2, *params)


ROW_TILE = 1024
DISPATCH_TILE = 2048


def _row_copy(src_row_ref, dst_row_ref, sem):
    return pltpu.make_async_copy(src_row_ref, dst_row_ref, sem)


def _dispatch_kernel(h_ref, dest_ref, xin_any, xout_any, idx_smem, sem_idx, sem_row):
    del xin_any
    n_tiles = h_ref.shape[0]
    cp = pltpu.make_async_copy(dest_ref.at[0, 0], idx_smem, sem_idx)
    cp.start()
    cp.wait()

    def issue(g, carry):
        for k in range(SUBLANES):
            r = g * SUBLANES + k
            src = h_ref.at[g, pl.ds(k, 1)]
            _row_copy(src, xout_any.at[pl.ds(idx_smem[2 * r], 1)], sem_row).start(priority=0)
            _row_copy(src, xout_any.at[pl.ds(idx_smem[2 * r + 1], 1)], sem_row).start(priority=1)
        return carry

    lax.fori_loop(0, n_tiles, issue, 0)

    def drain(g, carry):
        for _ in range(2 * SUBLANES):
            _row_copy(h_ref.at[0, pl.ds(0, 1)], xout_any.at[pl.ds(0, 1)], sem_row).wait()
        return carry

    lax.fori_loop(0, n_tiles, drain, 0)


def _dispatch(h2, dest, xin):
    t, d = h2.shape
    td = min(DISPATCH_TILE, t)
    dest3 = dest.reshape(t // td, 1, 2 * td)
    h2 = h2.reshape(t // SUBLANES, SUBLANES, d)
    return pl.pallas_call(
        _dispatch_kernel,
        grid=(t // td,),
        in_specs=[pl.BlockSpec((td // SUBLANES, SUBLANES, d), lambda i: (i, 0, 0)),
                  pl.BlockSpec((1, 1, 2 * td), lambda i: (i, 0, 0)),
                  pl.BlockSpec(memory_space=pl.ANY)],
        out_specs=pl.BlockSpec(memory_space=pl.ANY),
        out_shape=jax.ShapeDtypeStruct(xin.shape, xin.dtype),
        scratch_shapes=[pltpu.SMEM((2 * td,), I32), pltpu.SemaphoreType.DMA(()), pltpu.SemaphoreType.DMA(())],
        input_output_aliases={2: 0},
        compiler_params=_cparams(("arbitrary",)),
    )(h2, dest3, xin)


def _expert_kernel(be_ref, nu_ref, x_ref, wg_ref, wu_ref, wd_ref, o_ref, wgb_ref, wub_ref, wdb_ref):
    i = pl.program_id(0)

    @pl.when((i == 0) | (be_ref[i] != be_ref[jnp.maximum(i - 1, 0)]))
    def _():
        wgb_ref[...] = wg_ref[0, 0].astype(BF16)
        wub_ref[...] = wu_ref[0, 0].astype(BF16)
        wdb_ref[...] = wd_ref[0, 0].astype(BF16)

    @pl.when(i < nu_ref[0])
    def _():
        xb = _unpack_rows(x_ref[...]).astype(BF16)
        gte = _dot(xb, wgb_ref[...])
        up = _dot(xb, wub_ref[...])
        o_ref[...] = _pack_rows(_dot((_silu(gte) * up).astype(BF16), wdb_ref[...]))

    @pl.when(i >= nu_ref[0])
    def _():
        o_ref[...] = jnp.zeros_like(o_ref)


def _experts(xin, block_e, n_used, w_eg, w_eu, w_ed, layer):
    n_rows, dh = xin.shape
    d = 2 * dh
    nb = n_rows // MOE_BLOCK
    grid_spec = pltpu.PrefetchScalarGridSpec(
        num_scalar_prefetch=2,
        grid=(nb,),
        in_specs=[pl.BlockSpec((MOE_BLOCK, dh), lambda i, be, nu: (i, 0)),
                  pl.BlockSpec((1, 1, d, D_EXPERT), lambda i, be, nu: (layer, be[i], 0, 0)),
                  pl.BlockSpec((1, 1, d, D_EXPERT), lambda i, be, nu: (layer, be[i], 0, 0)),
                  pl.BlockSpec((1, 1, D_EXPERT, d), lambda i, be, nu: (layer, be[i], 0, 0))],
        out_specs=pl.BlockSpec((MOE_BLOCK, dh), lambda i, be, nu: (i, 0)),
        scratch_shapes=[pltpu.VMEM((d, D_EXPERT), BF16), pltpu.VMEM((d, D_EXPERT), BF16),
                        pltpu.VMEM((D_EXPERT, d), BF16)],
    )
    return pl.pallas_call(
        _expert_kernel,
        grid_spec=grid_spec,
        out_shape=jax.ShapeDtypeStruct((n_rows, dh), U32),
        compiler_params=_cparams(("arbitrary",)),
    )(block_e, n_used, xin, w_eg, w_eu, w_ed)


def _combine_kernel(dest_ref, rt_ref, x_ref, g2_ref, lng_ref, lnb_ref, yb_any,
                    o_ref, buf_ref, idx_smem, sem_idx, sem_row, *, alpha):
    tc = x_ref.shape[1]
    cp = pltpu.make_async_copy(dest_ref.at[0, 0], idx_smem, sem_idx)
    cp.start()
    cp.wait()

    n_tiles = tc // SUBLANES

    def issue(g, carry):
        for k in range(SUBLANES):
            r = g * SUBLANES + k
            _row_copy(yb_any.at[pl.ds(idx_smem[2 * r], 1)], buf_ref.at[0, g, pl.ds(k, 1)], sem_row).start(priority=0)
            _row_copy(yb_any.at[pl.ds(idx_smem[2 * r + 1], 1)], buf_ref.at[1, g, pl.ds(k, 1)], sem_row).start(priority=1)
        return carry

    lax.fori_loop(0, n_tiles, issue, 0)

    def drain(g, carry):
        for _ in range(2 * SUBLANES):
            _row_copy(yb_any.at[pl.ds(0, 1)], buf_ref.at[0, 0, pl.ds(0, 1)], sem_row).wait()
        return carry

    lax.fori_loop(0, n_tiles, drain, 0)

    rt = rt_ref[0]
    rows = lambda k: _unpack_rows(buf_ref[k].reshape(tc, buf_ref.shape[-1]))
    y = rt[:, 2:3] * rows(0) + rt[:, 3:4] * rows(1)
    o_ref[0] = _layer_norm(alpha * x_ref[0] + g2_ref[0] * y, lng_ref[...], lnb_ref[...])


def _combine(dest, route, x, g2, lp, yb, alpha):
    b, l, d = x.shape
    tc = min(ROW_TILE, l)
    npb = l // tc
    dest3 = dest.reshape(b * npb, 1, 2 * tc)
    par = lambda a: pl.BlockSpec(a.shape, lambda i, j: (0,) * a.ndim)
    return pl.pallas_call(
        functools.partial(_combine_kernel, alpha=alpha),
        grid=(b, npb),
        in_specs=[pl.BlockSpec((1, 1, 2 * tc), lambda i, j: (i * npb + j, 0, 0)),
                  pl.BlockSpec((1, tc, LANES), lambda i, j: (i, j, 0)),
                  pl.BlockSpec((1, tc, d), lambda i, j: (i, j, 0)),
                  pl.BlockSpec((1, 1, d), lambda i, j: (i, 0, 0)),
                  par(lp["ln2_g"]), par(lp["ln2_b"]),
                  pl.BlockSpec(memory_space=pl.ANY)],
        out_specs=pl.BlockSpec((1, tc, d), lambda i, j: (i, j, 0)),
        out_shape=jax.ShapeDtypeStruct((b, l, d), F32),
        scratch_shapes=[pltpu.VMEM((2, tc // SUBLANES, SUBLANES, d // 2), U32), pltpu.SMEM((2 * tc,), I32),
                        pltpu.SemaphoreType.DMA(()), pltpu.SemaphoreType.DMA(())],
        compiler_params=_cparams(("arbitrary", "arbitrary")),
    )(dest3, route, x, g2, lp["ln2_g"], lp["ln2_b"], yb)


def _moe_plan(routes, counts):
    cnt = counts[0, ROUTE_LANE0:ROUTE_LANE0 + MOE_EXPERTS].astype(I32)
    padded = (cnt + MOE_BLOCK - 1) // MOE_BLOCK * MOE_BLOCK
    pad_end = jnp.cumsum(padded)
    pad_start = pad_end - padded
    n_assign = sum(r.shape[0] * r.shape[1] for r in routes) * 2
    n_blocks = -(-n_assign // MOE_BLOCK) + MOE_EXPERTS
    first_row = jnp.arange(n_blocks, dtype=I32)[:, None] * MOE_BLOCK
    block_e = jnp.minimum(jnp.sum((pad_end[None, :] <= first_row).astype(I32), axis=1), MOE_EXPERTS - 1)
    n_used = (pad_end[-1:] // MOE_BLOCK).astype(I32)
    eidx = jnp.arange(MOE_EXPERTS, dtype=I32)
    dests = []
    for r in routes:
        e = r[..., 0:2].astype(I32)
        rank = r[..., 4:6].astype(I32)
        start = jnp.sum(jnp.where(e[..., None] == eidx, pad_start, 0), axis=-1)
        dests.append((start + rank).reshape(-1, 2))
    return dests, block_e, n_used, n_blocks * MOE_BLOCK


def _layer_params(i, p):
    w_in = p["w_in"][i]
    s0, s1, s2, s3 = D_INNER, D_INNER + D_XBC, D_INNER + D_XBC + 2 * HEADS, D_INNER + D_XBC + 2 * HEADS + D_MODEL
    pad = LANES - 2 * HEADS
    wdt = jnp.pad(w_in[:, s1:s2], ((0, 0), (0, pad)))
    w_r = jnp.concatenate([p["w_router_group"][i],
                           jnp.transpose(p["w_router_expert"][i], (1, 0, 2)).reshape(D_MODEL, MOE_EXPERTS)], axis=1)
    w_r = jnp.pad(w_r, ((0, 0), (0, LANES - w_r.shape[1])))
    w_r_hi = w_r.astype(BF16)
    b_r = jnp.concatenate([p["b_router_group"][i], p["b_router_expert"][i].reshape(-1)])
    return {
        "w_in": tuple(w.astype(BF16) for w in (w_in[:, :s0], w_in[:, s0:s1], w_in[:, s2:s3], w_in[:, s3:], wdt)),
        "conv_w": p["conv_w"][i], "conv_b": p["conv_b"][i][None],
        "dt_bias": jnp.pad(p["dt_bias"][i].reshape(1, -1), ((0, 0), (0, pad))),
        "a_log": jnp.pad(p["a_log"][i].reshape(1, -1), ((0, 0), (0, pad))),
        "d_skip_x": jnp.repeat(p["d_skip"][i], HEADDIM)[None],
        "ssd_norm_g": p["ssd_norm_g"][i][None],
        "pool_w": p["pool_w"][i].astype(BF16), "pool_scale": p["pool_scale"][i][None],
        "b_gate": p["b_gate"][i][None],
        "w_branch_a": p["w_branch_a"][i].astype(BF16), "w_branch_b": p["w_branch_b"][i].astype(BF16),
        "w_out": p["w_out"][i].astype(BF16),
        "ln1_g": p["ln1_g"][i][None], "ln1_b": p["ln1_b"][i][None],
        "ln2_g": p["ln2_g"][i][None], "ln2_b": p["ln2_b"][i][None],
        "w_r_hi": w_r_hi, "w_r_lo": (w_r - w_r_hi.astype(F32)).astype(BF16),
        "b_r": jnp.pad(b_r, (0, LANES - b_r.shape[0]))[None],
    }


def _head_expanders():
    col_head = jnp.arange(D_INNER, dtype=I32) // HEADDIM
    row = jnp.arange(LANES, dtype=I32)[:, None] % (2 * HEADS)
    return (row == col_head).astype(BF16), (row == col_head + HEADS).astype(BF16)


def kernel(x, c, ctx, c_ctx, w_ada, b_ada, w_in, b_gate, conv_w, conv_b, dt_bias, a_log, d_skip, ssd_norm_g, pool_w, pool_scale, w_branch_a, w_branch_b, w_out, ln1_g, ln1_b, ln2_g, ln2_b, w_router_group, b_router_group, w_router_expert, b_router_expert, w_expert_gate, w_expert_up, w_expert_down):
    p = dict(w_in=w_in, b_gate=b_gate, conv_w=conv_w, conv_b=conv_b, dt_bias=dt_bias, a_log=a_log, d_skip=d_skip,
             ssd_norm_g=ssd_norm_g, pool_w=pool_w, pool_scale=pool_scale, w_branch_a=w_branch_a,
             w_branch_b=w_branch_b, w_out=w_out, ln1_g=ln1_g, ln1_b=ln1_b, ln2_g=ln2_g, ln2_b=ln2_b,
             w_router_group=w_router_group, b_router_group=b_router_group, w_router_expert=w_router_expert,
             b_router_expert=b_router_expert, w_expert_gate=w_expert_gate, w_expert_up=w_expert_up,
             w_expert_down=w_expert_down)
    b, l, d = x.shape
    depth = w_ada.shape[0]
    alpha = (2.0 * depth) ** DEPTH_ALPHA_POW
    n_mod = -(-(b + 1) // 8) * 8
    cvec = jnp.zeros((n_mod, d), F32).at[:b].set(c).at[b].set(c_ctx)
    mods = _ada_mods(cvec, w_ada, b_ada)
    r_f, r_b = _head_expanders()
    zero_state = jnp.zeros((b, STATE, D_INNER), F32)
    zero_cnt = jnp.zeros((1, LANES), F32)
    xl, xc = x, ctx
    for i in range(depth):
        last = i == depth - 1
        lp = _layer_params(i, p)
        mod_l = [mods[i, :b, k * d:(k + 1) * d][:, None, :] for k in range(6)]
        mod_c = [jnp.broadcast_to(mods[i, b, k * d:(k + 1) * d], (b, 1, d)) for k in range(6)]

        zc, act_c, uc, gc, dtc = _in_proj(xc, mod_c[0], mod_c[1], lp["w_in"], lp["conv_w"], lp["conv_b"])
        if last:
            (s_f,) = _ssd_state_scan(act_c, dtc, zero_state, lp, r_f, rev=False, emit=False)
            (s_b,) = _ssd_state_scan(act_c, dtc, zero_state, lp, r_b, rev=True, emit=False)
        else:
            stb_c, s_b = _ssd_state_scan(act_c, dtc, zero_state, lp, r_b, rev=True, emit=True)
            ya_c, s_f = _ssd_main(act_c, dtc, zc, stb_c, zero_state, lp, r_f, r_b)
            yp_c = _pool_branch(uc, lp, False)

        zl, act_l, ul, gl, dtl = _in_proj(xl, mod_l[0], mod_l[1], lp["w_in"], lp["conv_w"], lp["conv_b"])
        stb_l, _ = _ssd_state_scan(act_l, dtl, s_b, lp, r_b, rev=True, emit=True)
        ya_l, _ = _ssd_main(act_l, dtl, zl, stb_l, s_f, lp, r_f, r_b)
        yp_l = _pool_branch(ul, lp, True)
        xl, h2_l, rt_l, cnt = _merge(ya_l, yp_l, gl, xl, mod_l[2], mod_l[3], mod_l[4], lp, zero_cnt, alpha)
        streams = [(h2_l, rt_l)]
        if not last:
            xc, h2_c, rt_c, cnt = _merge(ya_c, yp_c, gc, xc, mod_c[2], mod_c[3], mod_c[4], lp, cnt, alpha)
            streams.append((h2_c, rt_c))

        dests, block_e, n_used, n_rows = _moe_plan([rt for _, rt in streams], cnt)
        xin = jnp.zeros((n_rows, d // 2), U32)
        for (h2, _), dest in zip(streams, dests):
            xin = _dispatch(h2.reshape(-1, d // 2), dest, xin)
        yb = _experts(xin, block_e, n_used, w_expert_gate, w_expert_up, w_expert_down, i)
        xl = _combine(dests[0], rt_l, xl, mod_l[5], lp, yb, alpha)
        if not last:
            xc = _combine(dests[1], rt_c, xc, mod_c[5], lp, yb, alpha)
    return xl
```

```python
import functools
import math

import jax
import jax.numpy as jnp
from jax import lax
from jax.experimental import pallas as pl
from jax.experimental.pallas import tpu as pltpu
from jax.experimental.pallas import tpu_sc as plsc

F32 = jnp.float32
BF16 = jnp.bfloat16
I32 = jnp.int32

D_MODEL = 1024
D_INNER = 2048
HEADS = 32
HEADDIM = 64
GROUPS = 4
STATE = 128
D_BC = GROUPS * STATE
D_XBC = D_INNER + 2 * D_BC
CONV_K = 5
CHUNK = 128
GRID_W = 64
POOL_WINDOWS = (2, 4, 8, 16)
POOL_GROUP_DIM = 256
MOE_GROUPS = 4
MOE_EPG = 8
MOE_EXPERTS = 32
D_EXPERT = 512
MOE_BLOCK = 512
DEPTH_ALPHA_POW = 0.25
NORM_EPS = 1e-5
LANES = 128
SUBLANES = 8
ROUTE_LANE0 = MOE_GROUPS
VMEM_LIMIT = 56 * 1024 * 1024


def _cparams(sem, vmem=VMEM_LIMIT):
    return pltpu.CompilerParams(dimension_semantics=sem, vmem_limit_bytes=vmem)


def _sigmoid(v):
    return 0.5 + 0.5 * jnp.tanh(0.5 * v)


def _silu(v):
    h = 0.5 * v
    return h + h * jnp.tanh(h)


def _layer_norm(v, g, b):
    mu = jnp.mean(v, axis=-1, keepdims=True)
    d = v - mu
    var = jnp.mean(d * d, axis=-1, keepdims=True)
    return d * lax.rsqrt(var + NORM_EPS) * g + b


def _split_bf16(v):
    hi = v.astype(BF16)
    lo = (v - hi.astype(F32)).astype(BF16)
    return hi, lo


def _dot(a, b):
    return jnp.dot(a, b, preferred_element_type=F32)


U32 = jnp.uint32
HI16 = 0xFFFF0000


def _pack_rows(v):
    half = v.shape[1] // 2
    lo = lax.bitcast_convert_type(v[:, :half].astype(BF16).astype(F32), U32)
    hi = lax.bitcast_convert_type(v[:, half:].astype(BF16).astype(F32), U32)
    return lax.bitcast_convert_type((lo >> 16) | (hi & U32(HI16)), I32)


def _unpack_rows(w):
    w = lax.bitcast_convert_type(w, U32)
    lo = lax.bitcast_convert_type(w << 16, F32)
    hi = lax.bitcast_convert_type(w & U32(HI16), F32)
    return jnp.concatenate([lo, hi], axis=1)


def _ada_kernel(c_ref, w_ref, b_ref, o_ref):
    s = _silu(c_ref[...]).astype(BF16)
    o_ref[0] = _dot(s, w_ref[0].astype(BF16)) + b_ref[0]


def _ada_mods(cvec, w_ada, b_ada):
    depth, d, n = w_ada.shape
    r = cvec.shape[0]
    tn = D_MODEL
    return pl.pallas_call(
        _ada_kernel,
        grid=(depth, n // tn),
        in_specs=[pl.BlockSpec((r, d), lambda l, j: (0, 0)),
                  pl.BlockSpec((1, d, tn), lambda l, j: (l, 0, j)),
                  pl.BlockSpec((1, 1, tn), lambda l, j: (l, 0, j))],
        out_specs=pl.BlockSpec((1, r, tn), lambda l, j: (l, 0, j)),
        out_shape=jax.ShapeDtypeStruct((depth, r, n), F32),
        compiler_params=_cparams(("arbitrary", "arbitrary")),
    )(cvec, w_ada, b_ada.reshape(depth, 1, n))


CONV_HALO = 8
CONV_SLAB = 512


def _inproj_kernel(xp_ref, x_ref, xn_ref, sh_ref, sc_ref, wz_ref, wx_ref, wu_ref, wg_ref, wdt_ref, cw_ref, cb_ref,
                   z_ref, act_ref, u_ref, g_ref, dt_ref):
    j = pl.program_id(1)
    nj = pl.num_programs(1)
    tm = x_ref.shape[1]
    n = tm + 2 * CONV_HALO
    mod = lambda v: (v * (1.0 + sc_ref[0]) + sh_ref[0]).astype(BF16)
    h = mod(x_ref[0])
    h_ext = mod(jnp.concatenate([xp_ref[0], x_ref[0], xn_ref[0]], axis=0))

    row = lax.broadcasted_iota(I32, (n, 1), 0)
    inside = ((row >= CONV_HALO) | (j > 0)) & ((row < CONV_HALO + tm) | (j < nj - 1))

    z_ref[0] = _dot(h, wz_ref[...]).astype(BF16)
    u_ref[0] = _dot(h, wu_ref[...]).astype(BF16)
    g_ref[0] = _dot(h, wg_ref[...]).astype(BF16)
    dt_ref[0] = _dot(h, wdt_ref[...])
    down = lambda v: pltpu.roll(v, 1, 0)
    up = lambda v: pltpu.roll(v, n - 1, 0)
    for s in range(D_XBC // CONV_SLAB):
        cols = slice(s * CONV_SLAB, (s + 1) * CONV_SLAB)
        e = jnp.where(inside, _dot(h_ext, wx_ref[:, cols]), 0.0)
        y = [e * cw_ref[k:k + 1, cols] for k in range(CONV_K)]
        acc = y[2] + down(y[1] + down(y[0])) + up(y[3] + up(y[4]))
        act_ref[0, :, cols] = _silu(acc[CONV_HALO:CONV_HALO + tm] + cb_ref[:, cols]).astype(BF16)


def _in_proj(x, shift, scale, wts, conv_w, conv_b):
    b, l, d = x.shape
    tm = min(512, l)
    hpt = tm // CONV_HALO
    nh = l // CONV_HALO
    wz, wx, wu, wg, wdt = wts
    tok = lambda n: pl.BlockSpec((1, tm, n), lambda i, j: (i, j, 0))
    mod = pl.BlockSpec((1, 1, d), lambda i, j: (i, 0, 0))
    wspec = lambda w: pl.BlockSpec(w.shape, lambda i, j: (0, 0), pipeline_mode=pl.Buffered(1))
    par = lambda a: pl.BlockSpec(a.shape, lambda i, j: (0, 0))
    halo_prev = pl.BlockSpec((1, CONV_HALO, d), lambda i, j: (i, jnp.maximum(j * hpt - 1, 0), 0))
    halo_next = pl.BlockSpec((1, CONV_HALO, d), lambda i, j: (i, jnp.minimum((j + 1) * hpt, nh - 1), 0))
    return pl.pallas_call(
        _inproj_kernel,
        grid=(b, l // tm),
        in_specs=[halo_prev, tok(d), halo_next, mod, mod, wspec(wz), wspec(wx), wspec(wu), wspec(wg), wspec(wdt),
                  par(conv_w), par(conv_b)],
        out_specs=[tok(D_INNER), tok(D_XBC), tok(D_MODEL), tok(2 * D_MODEL), tok(LANES)],
        out_shape=[jax.ShapeDtypeStruct((b, l, D_INNER), BF16),
                   jax.ShapeDtypeStruct((b, l, D_XBC), BF16),
                   jax.ShapeDtypeStruct((b, l, D_MODEL), BF16),
                   jax.ShapeDtypeStruct((b, l, 2 * D_MODEL), BF16),
                   jax.ShapeDtypeStruct((b, l, LANES), F32)],
        compiler_params=_cparams(("arbitrary", "arbitrary")),
    )(x, x, x, shift, scale, wz, wx, wu, wg, wdt, conv_w, conv_b)


def _dt_prep(dt_raw, dtb_ref, alog_ref):
    q = CHUNK
    lane = lax.broadcasted_iota(I32, (q, LANES), 1)
    raw = dt_raw + dtb_ref[...]
    dt = jnp.maximum(raw, 0.0) + jnp.log1p(jnp.exp(-jnp.abs(raw)))
    dt = jnp.where(lane < 2 * HEADS, dt, 0.0)
    a = dt * (-jnp.exp(alog_ref[...]))
    ii = lax.broadcasted_iota(I32, (q, q), 0)
    jj = lax.broadcasted_iota(I32, (q, q), 1)
    tri = (jj <= ii).astype(BF16)
    a_hi, a_lo = _split_bf16(a)
    acs = _dot(tri, a_hi) + _dot(tri, a_lo)
    tot = acs[q - 1:q, :]
    return dt, a, acs, tot


def _expand(v, lane0, r_ref):
    return _dot(_expand_src(v, lane0), r_ref[...])


def _expand_src(v, lane0):
    lane = lax.broadcasted_iota(I32, v.shape, 1)
    vm = jnp.where((lane >= lane0) & (lane < lane0 + HEADS), v, 0.0)
    hi = vm.astype(BF16).astype(F32)
    return (hi + pltpu.roll(vm - hi, 2 * HEADS, 1)).astype(BF16)


def _state_update(st_ref, bm, xw_bf, decay_x):
    gw = D_INNER // GROUPS
    for g in range(GROUPS):
        bgt = bm[:, g * STATE:(g + 1) * STATE].T.astype(BF16)
        cols = slice(g * gw, (g + 1) * gw)
        st_ref[:, cols] = st_ref[:, cols] * decay_x[:, cols] + _dot(bgt, xw_bf[:, cols])


SCAN_CHUNKS = 8


def _ssd_state_kernel(*refs, rev, emit, nch):
    act_ref, dt_ref, h0_ref, dtb_ref, alog_ref, r_ref = refs[:6]
    if emit:
        stout_ref, sfin_ref, st_ref = refs[6:]
    else:
        sfin_ref, st_ref = refs[6:]
    s = pl.program_id(1)
    ns = pl.num_programs(1)

    @pl.when(s == 0)
    def _():
        st_ref[...] = h0_ref[0]

    for k in range(nch):
        cc = (nch - 1 - k) if rev else k
        rows = slice(cc * CHUNK, (cc + 1) * CHUNK)
        if emit:
            stout_ref[0, cc] = st_ref[...].astype(BF16)
        xs = act_ref[0, rows, :D_INNER].astype(F32)
        bm = act_ref[0, rows, D_INNER:D_INNER + D_BC].astype(F32)
        dt, a, acs, tot = _dt_prep(dt_ref[0, rows, :], dtb_ref, alog_ref)
        if rev:
            w = jnp.exp(acs - a) * dt
            lane0 = HEADS
        else:
            w = jnp.exp(tot - acs) * dt
            lane0 = 0
        w_x = _expand(w, lane0, r_ref)
        dec_x = _expand(jnp.broadcast_to(jnp.exp(tot), (16, LANES)), lane0, r_ref)[0:1]
        _state_update(st_ref, bm, (w_x * xs).astype(BF16), dec_x)

    @pl.when(s == ns - 1)
    def _():
        sfin_ref[0] = st_ref[...]


def _ssd_state_scan(act, dt, h0, lp, r_mat, *, rev, emit):
    b, l, _ = act.shape
    nc = l // CHUNK
    nch = min(SCAN_CHUNKS, nc)
    ns = nc // nch
    seq = (lambda s: ns - 1 - s) if rev else (lambda s: s)
    par = lambda shape: pl.BlockSpec(shape, lambda i, s: (0,) * len(shape))
    in_specs = [
        pl.BlockSpec((1, nch * CHUNK, D_XBC), lambda i, s: (i, seq(s), 0)),
        pl.BlockSpec((1, nch * CHUNK, LANES), lambda i, s: (i, seq(s), 0)),
        pl.BlockSpec((1, STATE, D_INNER), lambda i, s: (i, 0, 0)),
        par((1, LANES)), par((1, LANES)), par((LANES, D_INNER)),
    ]
    out_specs = [pl.BlockSpec((1, STATE, D_INNER), lambda i, s: (i, 0, 0))]
    out_shape = [jax.ShapeDtypeStruct((b, STATE, D_INNER), F32)]
    if emit:
        out_specs = [pl.BlockSpec((1, nch, STATE, D_INNER), lambda i, s: (i, seq(s), 0, 0))] + out_specs
        out_shape = [jax.ShapeDtypeStruct((b, nc, STATE, D_INNER), BF16)] + out_shape
    return pl.pallas_call(
        functools.partial(_ssd_state_kernel, rev=rev, emit=emit, nch=nch),
        grid=(b, ns),
        in_specs=in_specs, out_specs=out_specs, out_shape=out_shape,
        scratch_shapes=[pltpu.VMEM((STATE, D_INNER), F32)],
        compiler_params=_cparams(("arbitrary", "arbitrary")),
    )(act, dt, h0, lp["dt_bias"], lp["a_log"], r_mat)


MAIN_CHUNKS = 4


def _ssd_main_kernel(act_ref, dt_ref, z_ref, stb_ref, h0_ref,
                     dtb_ref, alog_ref, dsk_ref, ng_ref, rf_ref, rb_ref,
                     y_ref, sfin_ref, st_ref, ys_ref, *, nch):
    c = pl.program_id(1)
    nc = pl.num_programs(1)

    @pl.when(c == 0)
    def _():
        st_ref[...] = h0_ref[0]

    for k in range(nch):
        rows = pl.ds(k * CHUNK, CHUNK)
        _ssd_main_chunk(act_ref.at[0, rows], dt_ref.at[0, rows], z_ref.at[0, rows], stb_ref.at[0, k],
                        dtb_ref, alog_ref, dsk_ref, ng_ref, rf_ref, rb_ref, y_ref.at[0, rows], st_ref, ys_ref)

    @pl.when(c == nc - 1)
    def _():
        sfin_ref[0] = st_ref[...]


def _ssd_main_chunk(act_ref, dt_ref, z_ref, stb_ref, dtb_ref, alog_ref, dsk_ref, ng_ref, rf_ref, rb_ref,
                    y_ref, st_ref, ys_ref):
    q = CHUNK
    dt, a, acs, tot = _dt_prep(dt_ref[...], dtb_ref, alog_ref)
    eb = acs - a

    ef_s = _expand_src(jnp.exp(acs), 0)
    eb_s = _expand_src(jnp.exp(tot - eb), HEADS)
    tf_s = _expand_src(jnp.exp(tot - acs) * dt, 0)
    dec_s = _expand_src(jnp.broadcast_to(jnp.exp(tot), (16, LANES)), 0)

    lane = lax.broadcasted_iota(I32, (q, LANES), 1)
    ldt = jnp.log(dt)
    qm = jnp.where(lane < HEADS, acs - ldt, jnp.where(lane < 2 * HEADS, eb + ldt, pltpu.roll(dt, 2 * HEADS, 1)))
    qt = qm.T
    ii = lax.broadcasted_iota(I32, (q, q), 0)
    jj = lax.broadcasted_iota(I32, (q, q), 1)
    lower = jj <= ii
    diag = jj == ii
    lane2 = lax.broadcasted_iota(I32, (q, LANES), 1)
    left = lane2 < HEADDIM

    gw = D_INNER // GROUPS
    hpg = HEADS // GROUPS
    for g in range(GROUPS):
        bg = act_ref[:, D_INNER + g * STATE:D_INNER + (g + 1) * STATE]
        cg = act_ref[:, D_INNER + D_BC + g * STATE:D_INNER + D_BC + (g + 1) * STATE]
        cb = lax.dot_general(cg, bg, (((1,), (1,)), ((), ())), preferred_element_type=F32)
        cols = slice(g * gw, (g + 1) * gw)
        y_off = (_dot(ef_s, rf_ref[:, cols]) * _dot(cg, st_ref[:, cols].astype(BF16))
                 + _dot(eb_s, rb_ref[:, cols]) * _dot(cg, stb_ref[:, cols]))
        for pr in range(hpg // 2):
            ms = []
            for hh in range(2):
                h = g * hpg + pr * 2 + hh
                afc = acs[:, h:h + 1]
                afr = qt[h:h + 1, :]
                ebc = eb[:, HEADS + h:HEADS + h + 1]
                ebr = qt[HEADS + h:HEADS + h + 1, :]
                wb = qt[3 * HEADS + h:3 * HEADS + h + 1, :]
                arg = jnp.where(lower, afc - afr, ebr - ebc)
                ms.append((cb * (jnp.exp(arg) + jnp.where(diag, wb, 0.0))).astype(BF16))
            lhs = jnp.concatenate(ms, axis=1)
            c0 = g * gw + pr * 2 * HEADDIM
            xp = act_ref[:, c0:c0 + 2 * HEADDIM]
            zero = jnp.zeros_like(xp)
            rhs = jnp.concatenate([jnp.where(left, xp, zero), jnp.where(left, zero, xp)], axis=0)
            y_pair = _dot(lhs, rhs) + y_off[:, pr * 2 * HEADDIM:(pr + 1) * 2 * HEADDIM]
            y_pair = y_pair + dsk_ref[:, c0:c0 + 2 * HEADDIM] * xp.astype(F32)
            ys_ref[:, c0:c0 + 2 * HEADDIM] = y_pair

        xw = (_dot(tf_s, rf_ref[:, cols]) * act_ref[:, cols].astype(F32)).astype(BF16)
        bgt = bg.astype(F32).T.astype(BF16)
        st_ref[:, cols] = st_ref[:, cols] * _dot(dec_s, rf_ref[:, cols])[0:1] + _dot(bgt, xw)

    y = ys_ref[...]
    yz = y * _silu(z_ref[...].astype(F32))
    ms2 = jnp.mean(yz * yz, axis=-1, keepdims=True)
    y_ref[...] = (yz * lax.rsqrt(ms2 + NORM_EPS) * ng_ref[...]).astype(BF16)


def _ssd_main(act, dt, z, stb, h0, lp, r_f, r_b):
    b, l, _ = act.shape
    nc = l // CHUNK
    nch = min(MAIN_CHUNKS, nc)
    rows = nch * CHUNK
    par = lambda shape: pl.BlockSpec(shape, lambda i, s: (0,) * len(shape))
    in_specs = [
        pl.BlockSpec((1, rows, D_XBC), lambda i, s: (i, s, 0)),
        pl.BlockSpec((1, rows, LANES), lambda i, s: (i, s, 0)),
        pl.BlockSpec((1, rows, D_INNER), lambda i, s: (i, s, 0)),
        pl.BlockSpec((1, nch, STATE, D_INNER), lambda i, s: (i, s, 0, 0)),
        pl.BlockSpec((1, STATE, D_INNER), lambda i, s: (i, 0, 0)),
        par((1, LANES)), par((1, LANES)),
        par((1, D_INNER)), par((1, D_INNER)), par((LANES, D_INNER)), par((LANES, D_INNER)),
    ]
    return pl.pallas_call(
        functools.partial(_ssd_main_kernel, nch=nch),
        grid=(b, nc // nch),
        in_specs=in_specs,
        out_specs=[pl.BlockSpec((1, rows, D_INNER), lambda i, s: (i, s, 0)),
                   pl.BlockSpec((1, STATE, D_INNER), lambda i, s: (i, 0, 0))],
        out_shape=[jax.ShapeDtypeStruct((b, l, D_INNER), BF16),
                   jax.ShapeDtypeStruct((b, STATE, D_INNER), F32)],
        scratch_shapes=[pltpu.VMEM((STATE, D_INNER), F32), pltpu.VMEM((CHUNK, D_INNER), F32)],
        compiler_params=_cparams(("arbitrary", "arbitrary")),
    )(act, dt, z, stb, h0, lp["dt_bias"], lp["a_log"], lp["d_skip_x"], lp["ssd_norm_g"], r_f, r_b)


def _pool_kernel(*refs, width, vertical, halo, hblock):
    if vertical:
        prev_ref, cur_ref, next_ref, pw_ref, ps_ref, o_ref = refs
    else:
        cur_ref, pw_ref, ps_ref, o_ref = refs
    t = pl.program_id(1)
    nt = pl.num_programs(1)
    tp = cur_ref.shape[1]
    n_tok = nt * tp
    shift = int(math.log2(width))
    idx = lax.broadcasted_iota(I32, (tp, 1), 0) + t * tp
    colpos = idx & (width - 1)
    rowpos = idx >> shift
    n_rows = n_tok // width
    bi = lax.broadcasted_iota(I32, (hblock, hblock), 0)
    bj = lax.broadcasted_iota(I32, (hblock, hblock), 1)
    same_row = (bi >> shift) == (bj >> shift)
    cur = cur_ref[0].astype(F32)
    if vertical:
        prev = jnp.where(t > 0, prev_ref[0].astype(F32), 0.0)
        nxt = jnp.where(t < nt - 1, next_ref[0].astype(F32), 0.0)
    for gi, k in enumerate(POOL_WINDOWS):
        cols = slice(gi * POOL_GROUP_DIM, (gi + 1) * POOL_GROUP_DIM)
        ug = cur[:, cols]
        lo = k // 2
        if vertical:
            above, below = lo * width, (k - 1 - lo) * width
            parts = [prev[halo - above:, cols], ug] + ([nxt[:below, cols]] if below else [])
            e = jnp.concatenate(parts, axis=0)
            step = width
            m = 1
            while m < k:
                n = e.shape[0] - step
                e = e[0:n] + e[step:step + n]
                step *= 2
                m *= 2
            s = e
            cnt_r = (jnp.minimum(rowpos - lo + k, n_rows) - jnp.maximum(rowpos - lo, 0)).astype(F32)
        else:
            s = ug
            cnt_r = jnp.ones((tp, 1), F32)
        band = (same_row & (bj - bi >= -lo) & (bj - bi < k - lo)).astype(BF16)
        s_bf = s.astype(BF16)
        hs = [_dot(band, s_bf[r * hblock:(r + 1) * hblock]) for r in range(tp // hblock)]
        hsum = hs[0] if len(hs) == 1 else jnp.concatenate(hs, axis=0)
        cnt_c = (jnp.minimum(colpos - lo + k, width) - jnp.maximum(colpos - lo, 0)).astype(F32)
        mean = hsum / (cnt_r * cnt_c)
        y = _dot((mean - ug).astype(BF16), pw_ref[gi])
        o_ref[0, :, cols] = (y * ps_ref[:, cols]).astype(BF16)


def _pool_branch(u, lp, on_grid):
    b, l, d = u.shape
    pw, ps = lp["pool_w"], lp["pool_scale"]
    if on_grid:
        width, halo = GRID_W, (max(POOL_WINDOWS) // 2) * GRID_W
        tp = min(1024, l)
        hb = tp // halo
        nhb = l // halo
        in_specs = [pl.BlockSpec((1, halo, d), lambda i, t: (i, jnp.maximum(t * hb - 1, 0), 0)),
                    pl.BlockSpec((1, tp, d), lambda i, t: (i, t, 0)),
                    pl.BlockSpec((1, halo, d), lambda i, t: (i, jnp.minimum((t + 1) * hb, nhb - 1), 0))]
        args = (u, u, u)
        kern = functools.partial(_pool_kernel, width=width, vertical=True, halo=halo, hblock=LANES)
    else:
        tp = l
        in_specs = [pl.BlockSpec((1, tp, d), lambda i, t: (i, t, 0))]
        args = (u,)
        kern = functools.partial(_pool_kernel, width=l, vertical=False, halo=0, hblock=l)
    in_specs += [pl.BlockSpec(pw.shape, lambda i, t: (0, 0, 0)), pl.BlockSpec(ps.shape, lambda i, t: (0, 0))]
    return pl.pallas_call(
        kern,
        grid=(b, l // tp),
        in_specs=in_specs,
        out_specs=pl.BlockSpec((1, tp, d), lambda i, t: (i, t, 0)),
        out_shape=jax.ShapeDtypeStruct((b, l, d), BF16),
        compiler_params=_cparams(("arbitrary", "arbitrary")),
    )(*args, pw, ps)


def _merge_kernel(ya_ref, yp_ref, gt_ref, x_ref, g1_ref, sh2_ref, sc2_ref, bg_ref,
                  wa_ref, wb_ref, wo_ref, lng_ref, lnb_ref, wrh_ref, wrl_ref, br_ref, cin_ref,
                  xo_ref, h2_ref, rt_ref, cout_ref, cnt_ref, *, alpha):
    i = pl.program_id(0)
    j = pl.program_id(1)
    tm = x_ref.shape[1]

    @pl.when((i == 0) & (j == 0))
    def _():
        cnt_ref[...] = cin_ref[...]

    gate = _sigmoid(gt_ref[0].astype(F32) + bg_ref[...])
    merged = (gate[:, :D_MODEL] * _dot(ya_ref[0], wa_ref[...])
              + gate[:, D_MODEL:] * _dot(yp_ref[0], wb_ref[...]))
    out = _dot(merged.astype(BF16), wo_ref[...])
    xn = _layer_norm(alpha * x_ref[0] + g1_ref[0] * out, lng_ref[...], lnb_ref[...])
    xo_ref[0] = xn
    h2 = xn * (1.0 + sc2_ref[0]) + sh2_ref[0]
    h2_ref[0] = _pack_rows(h2)

    h_hi, h_lo = _split_bf16(h2)
    logits = _dot(h_hi, wrh_ref[...]) + _dot(h_lo, wrh_ref[...]) + _dot(h_hi, wrl_ref[...]) + br_ref[...]
    lane = lax.broadcasted_iota(I32, (tm, LANES), 1)
    neg = jnp.float32(-jnp.inf)
    lg = jnp.where(lane < MOE_GROUPS, logits, neg)
    mg = jnp.max(lg, axis=-1, keepdims=True)
    grp = jnp.min(jnp.where(lg == mg, lane, LANES), axis=-1, keepdims=True)
    p_grp = 1.0 / jnp.sum(jnp.exp(lg - mg), axis=-1, keepdims=True)
    lo_lane = ROUTE_LANE0 + grp * MOE_EPG
    le = jnp.where((lane >= lo_lane) & (lane < lo_lane + MOE_EPG), logits, neg)
    v1 = jnp.max(le, axis=-1, keepdims=True)
    i1 = jnp.min(jnp.where(le == v1, lane, LANES), axis=-1, keepdims=True)
    le2 = jnp.where(lane == i1, neg, le)
    v2 = jnp.max(le2, axis=-1, keepdims=True)
    i2 = jnp.min(jnp.where(le2 == v2, lane, LANES), axis=-1, keepdims=True)
    e2 = jnp.exp(v2 - v1)
    w1 = p_grp / (1.0 + e2)
    w2 = p_grp * e2 / (1.0 + e2)

    oh1 = (lane == i1).astype(F32)
    oh2 = (lane == i2).astype(F32)
    oh = oh1 + oh2
    ri = lax.broadcasted_iota(I32, (tm, tm), 0)
    rj = lax.broadcasted_iota(I32, (tm, tm), 1)
    before = _dot((rj < ri).astype(BF16), oh.astype(BF16)) + cnt_ref[...]
    rank1 = jnp.sum(oh1 * before, axis=-1, keepdims=True)
    rank2 = jnp.sum(oh2 * before, axis=-1, keepdims=True)
    cnt_ref[...] = cnt_ref[...] + jnp.sum(oh, axis=0, keepdims=True)
    cout_ref[...] = cnt_ref[...]

    slab = jnp.where(lane == 0, (i1 - ROUTE_LANE0).astype(F32),
           jnp.where(lane == 1, (i2 - ROUTE_LANE0).astype(F32),
           jnp.where(lane == 2, w1,
           jnp.where(lane == 3, w2,
           jnp.where(lane == 4, rank1,
           jnp.where(lane == 5, rank2, 0.0))))))
    rt_ref[0] = slab


def _merge(ya, yp, gates, x, g1, sh2, sc2, lp, cnt_in, alpha):
    b, l, d = x.shape
    tm = min(512, l)
    tok = lambda n: pl.BlockSpec((1, tm, n), lambda i, j: (i, j, 0))
    mod = pl.BlockSpec((1, 1, d), lambda i, j: (i, 0, 0))
    par = lambda a: pl.BlockSpec(a.shape, lambda i, j: (0,) * a.ndim)
    params = (lp["b_gate"], lp["w_branch_a"], lp["w_branch_b"], lp["w_out"], lp["ln1_g"], lp["ln1_b"],
              lp["w_r_hi"], lp["w_r_lo"], lp["b_r"], cnt_in)
    return pl.pallas_call(
        functools.partial(_merge_kernel, alpha=alpha),
        grid=(b, l // tm),
        in_specs=[tok(D_INNER), tok(d), tok(2 * d), tok(d), mod, mod, mod] + [par(a) for a in params],
        out_specs=[tok(d), tok(d // 2), tok(LANES), pl.BlockSpec((1, LANES), lambda i, j: (0, 0))],
        out_shape=[jax.ShapeDtypeStruct((b, l, d), F32), jax.ShapeDtypeStruct((b, l, d // 2), I32),
                   jax.ShapeDtypeStruct((b, l, LANES), F32), jax.ShapeDtypeStruct((1, LANES), F32)],
        scratch_shapes=[pltpu.VMEM((1, LANES), F32)],
        compiler_params=_cparams(("arbitrary", "arbitrary")),
    )(ya, yp, gates, x, g1, sh2, sc2, *params)


SC_CORES = 2
SC_SUBCORES = 16
SC_CHUNK = 128
ROW_TILE = 1024


def _sc_gather_rows(table, idx):
    m = idx.shape[0]
    d = table.shape[1]
    workers = SC_CORES * SC_SUBCORES
    n_chunks = m // (workers * SC_CHUNK)
    assert n_chunks * workers * SC_CHUNK == m, (m, workers, SC_CHUNK)
    mesh = plsc.VectorSubcoreMesh(core_axis_name="c", subcore_axis_name="s",
                                  num_cores=SC_CORES, num_subcores=SC_SUBCORES)

    @functools.partial(
        pl.kernel, mesh=mesh,
        out_type=jax.ShapeDtypeStruct((m, d), table.dtype),
        scratch_types=[pltpu.VMEM((SC_CHUNK,), I32), pltpu.VMEM((SC_CHUNK, d), table.dtype),
                       pltpu.SemaphoreType.DMA],
    )
    def gather(table_hbm, idx_hbm, out_hbm, idx_v, rows_v, sem):
        wid = lax.axis_index("s") * SC_CORES + lax.axis_index("c")

        @pl.loop(0, n_chunks)
        def _(j):
            base = (wid * n_chunks + j) * SC_CHUNK
            pltpu.sync_copy(idx_hbm.at[pl.ds(base, SC_CHUNK)], idx_v)
            pltpu.async_copy(table_hbm.at[idx_v], rows_v, sem).wait()
            pltpu.sync_copy(rows_v, out_hbm.at[pl.ds(base, SC_CHUNK)])

    return gather(table, idx)


def _expert_kernel(be_ref, nu_ref, x_ref, wg_ref, wu_ref, wd_ref, o_ref, wgb_ref, wub_ref, wdb_ref):
    i = pl.program_id(0)

    @pl.when((i == 0) | (be_ref[i] != be_ref[jnp.maximum(i - 1, 0)]))
    def _():
        wgb_ref[...] = wg_ref[0, 0].astype(BF16)
        wub_ref[...] = wu_ref[0, 0].astype(BF16)
        wdb_ref[...] = wd_ref[0, 0].astype(BF16)

    @pl.when(i < nu_ref[0])
    def _():
        xb = _unpack_rows(x_ref[...]).astype(BF16)
        gte = _dot(xb, wgb_ref[...])
        up = _dot(xb, wub_ref[...])
        o_ref[...] = _pack_rows(_dot((_silu(gte) * up).astype(BF16), wdb_ref[...]))

    @pl.when(i >= nu_ref[0])
    def _():
        o_ref[...] = jnp.zeros_like(o_ref)


def _experts(xin, block_e, n_used, w_eg, w_eu, w_ed, layer):
    n_rows, dh = xin.shape
    d = 2 * dh
    nb = n_rows // MOE_BLOCK
    grid_spec = pltpu.PrefetchScalarGridSpec(
        num_scalar_prefetch=2,
        grid=(nb,),
        in_specs=[pl.BlockSpec((MOE_BLOCK, dh), lambda i, be, nu: (i, 0)),
                  pl.BlockSpec((1, 1, d, D_EXPERT), lambda i, be, nu: (layer, be[i], 0, 0)),
                  pl.BlockSpec((1, 1, d, D_EXPERT), lambda i, be, nu: (layer, be[i], 0, 0)),
                  pl.BlockSpec((1, 1, D_EXPERT, d), lambda i, be, nu: (layer, be[i], 0, 0))],
        out_specs=pl.BlockSpec((MOE_BLOCK, dh), lambda i, be, nu: (i, 0)),
        scratch_shapes=[pltpu.VMEM((d, D_EXPERT), BF16), pltpu.VMEM((d, D_EXPERT), BF16),
                        pltpu.VMEM((D_EXPERT, d), BF16)],
    )
    return pl.pallas_call(
        _expert_kernel,
        grid_spec=grid_spec,
        out_shape=jax.ShapeDtypeStruct((n_rows, dh), I32),
        compiler_params=_cparams(("arbitrary",)),
    )(block_e, n_used, xin, w_eg, w_eu, w_ed)


def _combine_kernel(y0_ref, y1_ref, rt_ref, x_ref, g2_ref, lng_ref, lnb_ref, o_ref, *, alpha):
    rt = rt_ref[0]
    y = rt[:, 2:3] * _unpack_rows(y0_ref[...]) + rt[:, 3:4] * _unpack_rows(y1_ref[...])
    o_ref[0] = _layer_norm(alpha * x_ref[0] + g2_ref[0] * y, lng_ref[...], lnb_ref[...])


def _combine(planes, tok0, n_tok, route, x, g2, lp, alpha):
    b, l, d = x.shape
    tc = min(ROW_TILE, l)
    npb = l // tc
    blk0 = tok0 // tc
    blk1 = (n_tok + tok0) // tc
    assert blk0 * tc == tok0 and blk1 * tc == n_tok + tok0
    par = lambda a: pl.BlockSpec(a.shape, lambda i, j: (0,) * a.ndim)
    return pl.pallas_call(
        functools.partial(_combine_kernel, alpha=alpha),
        grid=(b, npb),
        in_specs=[pl.BlockSpec((tc, d // 2), lambda i, j: (blk0 + i * npb + j, 0)),
                  pl.BlockSpec((tc, d // 2), lambda i, j: (blk1 + i * npb + j, 0)),
                  pl.BlockSpec((1, tc, LANES), lambda i, j: (i, j, 0)),
                  pl.BlockSpec((1, tc, d), lambda i, j: (i, j, 0)),
                  pl.BlockSpec((1, 1, d), lambda i, j: (i, 0, 0)),
                  par(lp["ln2_g"]), par(lp["ln2_b"])],
        out_specs=pl.BlockSpec((1, tc, d), lambda i, j: (i, j, 0)),
        out_shape=jax.ShapeDtypeStruct((b, l, d), F32),
        compiler_params=_cparams(("arbitrary", "arbitrary")),
    )(planes, planes, route, x, g2, lp["ln2_g"], lp["ln2_b"])


def _moe_plan(routes, counts):
    cnt = counts[0, ROUTE_LANE0:ROUTE_LANE0 + MOE_EXPERTS].astype(I32)
    padded = (cnt + MOE_BLOCK - 1) // MOE_BLOCK * MOE_BLOCK
    pad_end = jnp.cumsum(padded)
    pad_start = pad_end - padded
    n_assign = sum(r.shape[0] * r.shape[1] for r in routes) * 2
    n_blocks = -(-n_assign // MOE_BLOCK) + MOE_EXPERTS
    first_row = jnp.arange(n_blocks, dtype=I32)[:, None] * MOE_BLOCK
    block_e = jnp.minimum(jnp.sum((pad_end[None, :] <= first_row).astype(I32), axis=1), MOE_EXPERTS - 1)
    n_used = (pad_end[-1:] // MOE_BLOCK).astype(I32)
    eidx = jnp.arange(MOE_EXPERTS, dtype=I32)
    dests = []
    for r in routes:
        e = r[..., 0:2].astype(I32)
        rank = r[..., 4:6].astype(I32)
        start = jnp.sum(jnp.where(e[..., None] == eidx, pad_start, 0), axis=-1)
        dests.append((start + rank).reshape(-1, 2))
    return dests, block_e, n_used, n_blocks * MOE_BLOCK


def _layer_params(i, p):
    w_in = p["w_in"][i]
    s0, s1, s2, s3 = D_INNER, D_INNER + D_XBC, D_INNER + D_XBC + 2 * HEADS, D_INNER + D_XBC + 2 * HEADS + D_MODEL
    pad = LANES - 2 * HEADS
    wdt = jnp.pad(w_in[:, s1:s2], ((0, 0), (0, pad)))
    w_r = jnp.concatenate([p["w_router_group"][i],
                           jnp.transpose(p["w_router_expert"][i], (1, 0, 2)).reshape(D_MODEL, MOE_EXPERTS)], axis=1)
    w_r = jnp.pad(w_r, ((0, 0), (0, LANES - w_r.shape[1])))
    w_r_hi = w_r.astype(BF16)
    b_r = jnp.concatenate([p["b_router_group"][i], p["b_router_expert"][i].reshape(-1)])
    return {
        "w_in": tuple(w.astype(BF16) for w in (w_in[:, :s0], w_in[:, s0:s1], w_in[:, s2:s3], w_in[:, s3:], wdt)),
        "conv_w": p["conv_w"][i], "conv_b": p["conv_b"][i][None],
        "dt_bias": jnp.pad(p["dt_bias"][i].reshape(1, -1), ((0, 0), (0, pad))),
        "a_log": jnp.pad(p["a_log"][i].reshape(1, -1), ((0, 0), (0, pad))),
        "d_skip_x": jnp.repeat(p["d_skip"][i], HEADDIM)[None],
        "ssd_norm_g": p["ssd_norm_g"][i][None],
        "pool_w": p["pool_w"][i].astype(BF16), "pool_scale": p["pool_scale"][i][None],
        "b_gate": p["b_gate"][i][None],
        "w_branch_a": p["w_branch_a"][i].astype(BF16), "w_branch_b": p["w_branch_b"][i].astype(BF16),
        "w_out": p["w_out"][i].astype(BF16),
        "ln1_g": p["ln1_g"][i][None], "ln1_b": p["ln1_b"][i][None],
        "ln2_g": p["ln2_g"][i][None], "ln2_b": p["ln2_b"][i][None],
        "w_r_hi": w_r_hi, "w_r_lo": (w_r - w_r_hi.astype(F32)).astype(BF16),
        "b_r": jnp.pad(b_r, (0, LANES - b_r.shape[0]))[None],
    }


def _head_expanders():
    col_head = jnp.arange(D_INNER, dtype=I32) // HEADDIM
    row = jnp.arange(LANES, dtype=I32)[:, None] % (2 * HEADS)
    return (row == col_head).astype(BF16), (row == col_head + HEADS).astype(BF16)


def kernel(x, c, ctx, c_ctx, w_ada, b_ada, w_in, b_gate, conv_w, conv_b, dt_bias, a_log, d_skip, ssd_norm_g, pool_w, pool_scale, w_branch_a, w_branch_b, w_out, ln1_g, ln1_b, ln2_g, ln2_b, w_router_group, b_router_group, w_router_expert, b_router_expert, w_expert_gate, w_expert_up, w_expert_down):
    p = dict(w_in=w_in, b_gate=b_gate, conv_w=conv_w, conv_b=conv_b, dt_bias=dt_bias, a_log=a_log, d_skip=d_skip,
             ssd_norm_g=ssd_norm_g, pool_w=pool_w, pool_scale=pool_scale, w_branch_a=w_branch_a,
             w_branch_b=w_branch_b, w_out=w_out, ln1_g=ln1_g, ln1_b=ln1_b, ln2_g=ln2_g, ln2_b=ln2_b,
             w_router_group=w_router_group, b_router_group=b_router_group, w_router_expert=w_router_expert,
             b_router_expert=b_router_expert, w_expert_gate=w_expert_gate, w_expert_up=w_expert_up,
             w_expert_down=w_expert_down)
    b, l, d = x.shape
    depth = w_ada.shape[0]
    alpha = (2.0 * depth) ** DEPTH_ALPHA_POW
    n_mod = -(-(b + 1) // 8) * 8
    cvec = jnp.zeros((n_mod, d), F32).at[:b].set(c).at[b].set(c_ctx)
    mods = _ada_mods(cvec, w_ada, b_ada)
    r_f, r_b = _head_expanders()
    zero_state = jnp.zeros((b, STATE, D_INNER), F32)
    zero_cnt = jnp.zeros((1, LANES), F32)
    xl, xc = x, ctx
    for i in range(depth):
        last = i == depth - 1
        lp = _layer_params(i, p)
        mod_l = [mods[i, :b, k * d:(k + 1) * d][:, None, :] for k in range(6)]
        mod_c = [jnp.broadcast_to(mods[i, b, k * d:(k + 1) * d], (b, 1, d)) for k in range(6)]

        zc, act_c, uc, gc, dtc = _in_proj(xc, mod_c[0], mod_c[1], lp["w_in"], lp["conv_w"], lp["conv_b"])
        if last:
            (s_f,) = _ssd_state_scan(act_c, dtc, zero_state, lp, r_f, rev=False, emit=False)
            (s_b,) = _ssd_state_scan(act_c, dtc, zero_state, lp, r_b, rev=True, emit=False)
        else:
            stb_c, s_b = _ssd_state_scan(act_c, dtc, zero_state, lp, r_b, rev=True, emit=True)
            ya_c, s_f = _ssd_main(act_c, dtc, zc, stb_c, zero_state, lp, r_f, r_b)
            yp_c = _pool_branch(uc, lp, False)

        zl, act_l, ul, gl, dtl = _in_proj(xl, mod_l[0], mod_l[1], lp["w_in"], lp["conv_w"], lp["conv_b"])
        stb_l, _ = _ssd_state_scan(act_l, dtl, s_b, lp, r_b, rev=True, emit=True)
        ya_l, _ = _ssd_main(act_l, dtl, zl, stb_l, s_f, lp, r_f, r_b)
        yp_l = _pool_branch(ul, lp, True)
        xl, h2_l, rt_l, cnt = _merge(ya_l, yp_l, gl, xl, mod_l[2], mod_l[3], mod_l[4], lp, zero_cnt, alpha)
        streams = [(h2_l, rt_l)]
        if not last:
            xc, h2_c, rt_c, cnt = _merge(ya_c, yp_c, gc, xc, mod_c[2], mod_c[3], mod_c[4], lp, cnt, alpha)
            streams.append((h2_c, rt_c))

        dests, block_e, n_used, n_rows = _moe_plan([rt for _, rt in streams], cnt)
        dest_all = jnp.concatenate(dests, axis=0)
        h2_all = jnp.concatenate([h2.reshape(-1, d // 2) for h2, _ in streams], axis=0)
        n_tok = dest_all.shape[0]
        row_tok = jnp.zeros((n_rows,), I32).at[dest_all.reshape(-1)].set(
            jnp.repeat(jnp.arange(n_tok, dtype=I32), 2), unique_indices=True)
        xin = _sc_gather_rows(h2_all, row_tok)
        yb = _experts(xin, block_e, n_used, w_expert_gate, w_expert_up, w_expert_down, i)
        planes = _sc_gather_rows(yb, dest_all.T.reshape(-1))
        xl = _combine(planes, 0, n_tok, rt_l, xl, mod_l[5], lp, alpha)
        if not last:
            xc = _combine(planes, b * l, n_tok, rt_c, xc, mod_c[5], lp, alpha)
    return xl
```

```python
import functools
import math

import jax
import jax.numpy as jnp
from jax import lax
from jax.experimental import pallas as pl
from jax.experimental.pallas import tpu as pltpu
from jax.experimental.pallas import tpu_sc as plsc

F32 = jnp.float32
BF16 = jnp.bfloat16
I32 = jnp.int32

D_MODEL = 1024
D_INNER = 2048
HEADS = 32
HEADDIM = 64
GROUPS = 4
STATE = 128
D_BC = GROUPS * STATE
D_XBC = D_INNER + 2 * D_BC
CONV_K = 5
CHUNK = 128
GRID_W = 64
POOL_WINDOWS = (2, 4, 8, 16)
POOL_GROUP_DIM = 256
MOE_GROUPS = 4
MOE_EPG = 8
MOE_EXPERTS = 32
D_EXPERT = 512
MOE_BLOCK = 512
DEPTH_ALPHA_POW = 0.25
NORM_EPS = 1e-5
LANES = 128
SUBLANES = 8
ROUTE_LANE0 = MOE_GROUPS
VMEM_LIMIT = 56 * 1024 * 1024


def _cparams(sem, vmem=VMEM_LIMIT):
    return pltpu.CompilerParams(dimension_semantics=sem, vmem_limit_bytes=vmem)


def _sigmoid(v):
    return 0.5 + 0.5 * jnp.tanh(0.5 * v)


def _silu(v):
    h = 0.5 * v
    return h + h * jnp.tanh(h)


def _layer_norm(v, g, b):
    mu = jnp.mean(v, axis=-1, keepdims=True)
    d = v - mu
    var = jnp.mean(d * d, axis=-1, keepdims=True)
    return d * lax.rsqrt(var + NORM_EPS) * g + b


def _split_bf16(v):
    hi = v.astype(BF16)
    lo = (v - hi.astype(F32)).astype(BF16)
    return hi, lo


def _dot(a, b):
    return jnp.dot(a, b, preferred_element_type=F32)


U32 = jnp.uint32
HI16 = 0xFFFF0000


def _pack_rows(v):
    half = v.shape[1] // 2
    lo = lax.bitcast_convert_type(v[:, :half].astype(BF16).astype(F32), U32)
    hi = lax.bitcast_convert_type(v[:, half:].astype(BF16).astype(F32), U32)
    return lax.bitcast_convert_type((lo >> 16) | (hi & U32(HI16)), I32)


def _unpack_rows(w):
    w = lax.bitcast_convert_type(w, U32)
    lo = lax.bitcast_convert_type(w << 16, F32)
    hi = lax.bitcast_convert_type(w & U32(HI16), F32)
    return jnp.concatenate([lo, hi], axis=1)


def _ada_kernel(c_ref, w_ref, b_ref, o_ref):
    s = _silu(c_ref[...]).astype(BF16)
    o_ref[0] = _dot(s, w_ref[0].astype(BF16)) + b_ref[0]


def _ada_mods(cvec, w_ada, b_ada):
    depth, d, n = w_ada.shape
    r = cvec.shape[0]
    tn = D_MODEL
    return pl.pallas_call(
        _ada_kernel,
        grid=(depth, n // tn),
        in_specs=[pl.BlockSpec((r, d), lambda l, j: (0, 0)),
                  pl.BlockSpec((1, d, tn), lambda l, j: (l, 0, j)),
                  pl.BlockSpec((1, 1, tn), lambda l, j: (l, 0, j))],
        out_specs=pl.BlockSpec((1, r, tn), lambda l, j: (l, 0, j)),
        out_shape=jax.ShapeDtypeStruct((depth, r, n), F32),
        compiler_params=_cparams(("arbitrary", "arbitrary")),
    )(cvec, w_ada, b_ada.reshape(depth, 1, n))


CONV_HALO = 8
CONV_SLAB = 512


def _inproj_kernel(xp_ref, x_ref, xn_ref, sh_ref, sc_ref, wz_ref, wx_ref, wu_ref, wg_ref, wdt_ref, cw_ref, cb_ref,
                   z_ref, act_ref, u_ref, g_ref, dt_ref):
    j = pl.program_id(1)
    nj = pl.num_programs(1)
    tm = x_ref.shape[1]
    n = tm + 2 * CONV_HALO
    mod = lambda v: (v * (1.0 + sc_ref[0]) + sh_ref[0]).astype(BF16)
    h = mod(x_ref[0])
    h_ext = mod(jnp.concatenate([xp_ref[0], x_ref[0], xn_ref[0]], axis=0))

    row = lax.broadcasted_iota(I32, (n, 1), 0)
    inside = ((row >= CONV_HALO) | (j > 0)) & ((row < CONV_HALO + tm) | (j < nj - 1))

    z_ref[0] = _dot(h, wz_ref[...]).astype(BF16)
    u_ref[0] = _dot(h, wu_ref[...]).astype(BF16)
    g_ref[0] = _dot(h, wg_ref[...]).astype(BF16)
    dt_ref[0] = _dot(h, wdt_ref[...])
    down = lambda v: pltpu.roll(v, 1, 0)
    up = lambda v: pltpu.roll(v, n - 1, 0)
    for s in range(D_XBC // CONV_SLAB):
        cols = slice(s * CONV_SLAB, (s + 1) * CONV_SLAB)
        e = jnp.where(inside, _dot(h_ext, wx_ref[:, cols]), 0.0)
        y = [e * cw_ref[k:k + 1, cols] for k in range(CONV_K)]
        acc = y[2] + down(y[1] + down(y[0])) + up(y[3] + up(y[4]))
        act_ref[0, :, cols] = _silu(acc[CONV_HALO:CONV_HALO + tm] + cb_ref[:, cols]).astype(BF16)


def _in_proj(x, shift, scale, wts, conv_w, conv_b):
    b, l, d = x.shape
    tm = min(512, l)
    hpt = tm // CONV_HALO
    nh = l // CONV_HALO
    wz, wx, wu, wg, wdt = wts
    tok = lambda n: pl.BlockSpec((1, tm, n), lambda i, j: (i, j, 0))
    mod = pl.BlockSpec((1, 1, d), lambda i, j: (i, 0, 0))
    wspec = lambda w: pl.BlockSpec(w.shape, lambda i, j: (0, 0), pipeline_mode=pl.Buffered(1))
    par = lambda a: pl.BlockSpec(a.shape, lambda i, j: (0, 0))
    halo_prev = pl.BlockSpec((1, CONV_HALO, d), lambda i, j: (i, jnp.maximum(j * hpt - 1, 0), 0))
    halo_next = pl.BlockSpec((1, CONV_HALO, d), lambda i, j: (i, jnp.minimum((j + 1) * hpt, nh - 1), 0))
    return pl.pallas_call(
        _inproj_kernel,
        grid=(b, l // tm),
        in_specs=[halo_prev, tok(d), halo_next, mod, mod, wspec(wz), wspec(wx), wspec(wu), wspec(wg), wspec(wdt),
                  par(conv_w), par(conv_b)],
        out_specs=[tok(D_INNER), tok(D_XBC), tok(D_MODEL), tok(2 * D_MODEL), tok(LANES)],
        out_shape=[jax.ShapeDtypeStruct((b, l, D_INNER), BF16),
                   jax.ShapeDtypeStruct((b, l, D_XBC), BF16),
                   jax.ShapeDtypeStruct((b, l, D_MODEL), BF16),
                   jax.ShapeDtypeStruct((b, l, 2 * D_MODEL), BF16),
                   jax.ShapeDtypeStruct((b, l, LANES), F32)],
        compiler_params=_cparams(("arbitrary", "arbitrary")),
    )(x, x, x, shift, scale, wz, wx, wu, wg, wdt, conv_w, conv_b)


def _dt_prep(dt_raw, dtb_ref, alog_ref):
    q = CHUNK
    lane = lax.broadcasted_iota(I32, (q, LANES), 1)
    raw = dt_raw + dtb_ref[...]
    dt = jnp.maximum(raw, 0.0) + jnp.log1p(jnp.exp(-jnp.abs(raw)))
    dt = jnp.where(lane < 2 * HEADS, dt, 0.0)
    a = dt * (-jnp.exp(alog_ref[...]))
    ii = lax.broadcasted_iota(I32, (q, q), 0)
    jj = lax.broadcasted_iota(I32, (q, q), 1)
    tri = (jj <= ii).astype(BF16)
    a_hi, a_lo = _split_bf16(a)
    acs = _dot(tri, a_hi) + _dot(tri, a_lo)
    tot = acs[q - 1:q, :]
    return dt, a, acs, tot


def _expand(v, lane0, r_ref):
    return _dot(_expand_src(v, lane0), r_ref[...])


def _expand_src(v, lane0):
    lane = lax.broadcasted_iota(I32, v.shape, 1)
    vm = jnp.where((lane >= lane0) & (lane < lane0 + HEADS), v, 0.0)
    hi = vm.astype(BF16).astype(F32)
    return (hi + pltpu.roll(vm - hi, 2 * HEADS, 1)).astype(BF16)


def _state_update(st_ref, bm, xw_bf, decay_x):
    gw = D_INNER // GROUPS
    for g in range(GROUPS):
        bgt = bm[:, g * STATE:(g + 1) * STATE].T.astype(BF16)
        cols = slice(g * gw, (g + 1) * gw)
        st_ref[:, cols] = st_ref[:, cols] * decay_x[:, cols] + _dot(bgt, xw_bf[:, cols])


SCAN_CHUNKS = 8


def _ssd_state_kernel(*refs, rev, emit, nch):
    act_ref, dt_ref, h0_ref, dtb_ref, alog_ref, r_ref = refs[:6]
    if emit:
        stout_ref, sfin_ref, st_ref = refs[6:]
    else:
        sfin_ref, st_ref = refs[6:]
    s = pl.program_id(1)
    ns = pl.num_programs(1)

    @pl.when(s == 0)
    def _():
        st_ref[...] = h0_ref[0]

    for k in range(nch):
        cc = (nch - 1 - k) if rev else k
        rows = slice(cc * CHUNK, (cc + 1) * CHUNK)
        if emit:
            stout_ref[0, cc] = st_ref[...].astype(BF16)
        xs = act_ref[0, rows, :D_INNER].astype(F32)
        bm = act_ref[0, rows, D_INNER:D_INNER + D_BC].astype(F32)
        dt, a, acs, tot = _dt_prep(dt_ref[0, rows, :], dtb_ref, alog_ref)
        if rev:
            w = jnp.exp(acs - a) * dt
            lane0 = HEADS
        else:
            w = jnp.exp(tot - acs) * dt
            lane0 = 0
        w_x = _expand(w, lane0, r_ref)
        dec_x = _expand(jnp.broadcast_to(jnp.exp(tot), (16, LANES)), lane0, r_ref)[0:1]
        _state_update(st_ref, bm, (w_x * xs).astype(BF16), dec_x)

    @pl.when(s == ns - 1)
    def _():
        sfin_ref[0] = st_ref[...]


def _ssd_state_scan(act, dt, h0, lp, r_mat, *, rev, emit):
    b, l, _ = act.shape
    nc = l // CHUNK
    nch = min(SCAN_CHUNKS, nc)
    ns = nc // nch
    seq = (lambda s: ns - 1 - s) if rev else (lambda s: s)
    par = lambda shape: pl.BlockSpec(shape, lambda i, s: (0,) * len(shape))
    in_specs = [
        pl.BlockSpec((1, nch * CHUNK, D_XBC), lambda i, s: (i, seq(s), 0)),
        pl.BlockSpec((1, nch * CHUNK, LANES), lambda i, s: (i, seq(s), 0)),
        pl.BlockSpec((1, STATE, D_INNER), lambda i, s: (i, 0, 0)),
        par((1, LANES)), par((1, LANES)), par((LANES, D_INNER)),
    ]
    out_specs = [pl.BlockSpec((1, STATE, D_INNER), lambda i, s: (i, 0, 0))]
    out_shape = [jax.ShapeDtypeStruct((b, STATE, D_INNER), F32)]
    if emit:
        out_specs = [pl.BlockSpec((1, nch, STATE, D_INNER), lambda i, s: (i, seq(s), 0, 0))] + out_specs
        out_shape = [jax.ShapeDtypeStruct((b, nc, STATE, D_INNER), BF16)] + out_shape
    return pl.pallas_call(
        functools.partial(_ssd_state_kernel, rev=rev, emit=emit, nch=nch),
        grid=(b, ns),
        in_specs=in_specs, out_specs=out_specs, out_shape=out_shape,
        scratch_shapes=[pltpu.VMEM((STATE, D_INNER), F32)],
        compiler_params=_cparams(("arbitrary", "arbitrary")),
    )(act, dt, h0, lp["dt_bias"], lp["a_log"], r_mat)


MAIN_CHUNKS = 4


def _ssd_main_kernel(act_ref, dt_ref, z_ref, stb_ref, h0_ref,
                     dtb_ref, alog_ref, dsk_ref, ng_ref, rf_ref, rb_ref,
                     y_ref, sfin_ref, st_ref, ys_ref, *, nch):
    c = pl.program_id(1)
    nc = pl.num_programs(1)

    @pl.when(c == 0)
    def _():
        st_ref[...] = h0_ref[0]

    for k in range(nch):
        rows = pl.ds(k * CHUNK, CHUNK)
        _ssd_main_chunk(act_ref.at[0, rows], dt_ref.at[0, rows], z_ref.at[0, rows], stb_ref.at[0, k],
                        dtb_ref, alog_ref, dsk_ref, ng_ref, rf_ref, rb_ref, y_ref.at[0, rows], st_ref, ys_ref)

    @pl.when(c == nc - 1)
    def _():
        sfin_ref[0] = st_ref[...]


def _ssd_main_chunk(act_ref, dt_ref, z_ref, stb_ref, dtb_ref, alog_ref, dsk_ref, ng_ref, rf_ref, rb_ref,
                    y_ref, st_ref, ys_ref):
    q = CHUNK
    dt, a, acs, tot = _dt_prep(dt_ref[...], dtb_ref, alog_ref)
    eb = acs - a

    ef_s = _expand_src(jnp.exp(acs), 0)
    eb_s = _expand_src(jnp.exp(tot - eb), HEADS)
    tf_s = _expand_src(jnp.exp(tot - acs) * dt, 0)
    dec_s = _expand_src(jnp.broadcast_to(jnp.exp(tot), (16, LANES)), 0)

    lane = lax.broadcasted_iota(I32, (q, LANES), 1)
    ldt = jnp.log(dt)
    qm = jnp.where(lane < HEADS, acs - ldt, jnp.where(lane < 2 * HEADS, eb + ldt, pltpu.roll(dt, 2 * HEADS, 1)))
    qt = qm.T
    ii = lax.broadcasted_iota(I32, (q, q), 0)
    jj = lax.broadcasted_iota(I32, (q, q), 1)
    lower = jj <= ii
    diag = jj == ii
    lane2 = lax.broadcasted_iota(I32, (q, LANES), 1)
    left = lane2 < HEADDIM

    gw = D_INNER // GROUPS
    hpg = HEADS // GROUPS
    for g in range(GROUPS):
        bg = act_ref[:, D_INNER + g * STATE:D_INNER + (g + 1) * STATE]
        cg = act_ref[:, D_INNER + D_BC + g * STATE:D_INNER + D_BC + (g + 1) * STATE]
        cb = lax.dot_general(cg, bg, (((1,), (1,)), ((), ())), preferred_element_type=F32)
        cols = slice(g * gw, (g + 1) * gw)
        y_off = (_dot(ef_s, rf_ref[:, cols]) * _dot(cg, st_ref[:, cols].astype(BF16))
                 + _dot(eb_s, rb_ref[:, cols]) * _dot(cg, stb_ref[:, cols]))
        for pr in range(hpg // 2):
            ms = []
            for hh in range(2):
                h = g * hpg + pr * 2 + hh
                afc = acs[:, h:h + 1]
                afr = qt[h:h + 1, :]
                ebc = eb[:, HEADS + h:HEADS + h + 1]
                ebr = qt[HEADS + h:HEADS + h + 1, :]
                wb = qt[3 * HEADS + h:3 * HEADS + h + 1, :]
                arg = jnp.where(lower, afc - afr, ebr - ebc)
                ms.append((cb * (jnp.exp(arg) + jnp.where(diag, wb, 0.0))).astype(BF16))
            lhs = jnp.concatenate(ms, axis=1)
            c0 = g * gw + pr * 2 * HEADDIM
            xp = act_ref[:, c0:c0 + 2 * HEADDIM]
            zero = jnp.zeros_like(xp)
            rhs = jnp.concatenate([jnp.where(left, xp, zero), jnp.where(left, zero, xp)], axis=0)
            y_pair = _dot(lhs, rhs) + y_off[:, pr * 2 * HEADDIM:(pr + 1) * 2 * HEADDIM]
            y_pair = y_pair + dsk_ref[:, c0:c0 + 2 * HEADDIM] * xp.astype(F32)
            ys_ref[:, c0:c0 + 2 * HEADDIM] = y_pair

        xw = (_dot(tf_s, rf_ref[:, cols]) * act_ref[:, cols].astype(F32)).astype(BF16)
        bgt = bg.astype(F32).T.astype(BF16)
        st_ref[:, cols] = st_ref[:, cols] * _dot(dec_s, rf_ref[:, cols])[0:1] + _dot(bgt, xw)

    y = ys_ref[...]
    yz = y * _silu(z_ref[...].astype(F32))
    ms2 = jnp.mean(yz * yz, axis=-1, keepdims=True)
    y_ref[...] = (yz * lax.rsqrt(ms2 + NORM_EPS) * ng_ref[...]).astype(BF16)


def _ssd_main(act, dt, z, stb, h0, lp, r_f, r_b):
    b, l, _ = act.shape
    nc = l // CHUNK
    nch = min(MAIN_CHUNKS, nc)
    rows = nch * CHUNK
    par = lambda shape: pl.BlockSpec(shape, lambda i, s: (0,) * len(shape))
    in_specs = [
        pl.BlockSpec((1, rows, D_XBC), lambda i, s: (i, s, 0)),
        pl.BlockSpec((1, rows, LANES), lambda i, s: (i, s, 0)),
        pl.BlockSpec((1, rows, D_INNER), lambda i, s: (i, s, 0)),
        pl.BlockSpec((1, nch, STATE, D_INNER), lambda i, s: (i, s, 0, 0)),
        pl.BlockSpec((1, STATE, D_INNER), lambda i, s: (i, 0, 0)),
        par((1, LANES)), par((1, LANES)),
        par((1, D_INNER)), par((1, D_INNER)), par((LANES, D_INNER)), par((LANES, D_INNER)),
    ]
    return pl.pallas_call(
        functools.partial(_ssd_main_kernel, nch=nch),
        grid=(b, nc // nch),
        in_specs=in_specs,
        out_specs=[pl.BlockSpec((1, rows, D_INNER), lambda i, s: (i, s, 0)),
                   pl.BlockSpec((1, STATE, D_INNER), lambda i, s: (i, 0, 0))],
        out_shape=[jax.ShapeDtypeStruct((b, l, D_INNER), BF16),
                   jax.ShapeDtypeStruct((b, STATE, D_INNER), F32)],
        scratch_shapes=[pltpu.VMEM((STATE, D_INNER), F32), pltpu.VMEM((CHUNK, D_INNER), F32)],
        compiler_params=_cparams(("arbitrary", "arbitrary")),
    )(act, dt, z, stb, h0, lp["dt_bias"], lp["a_log"], lp["d_skip_x"], lp["ssd_norm_g"], r_f, r_b)


def _pool_kernel(*refs, width, vertical, halo, hblock):
    if vertical:
        prev_ref, cur_ref, next_ref, pw_ref, ps_ref, o_ref = refs
    else:
        cur_ref, pw_ref, ps_ref, o_ref = refs
    t = pl.program_id(1)
    nt = pl.num_programs(1)
    tp = cur_ref.shape[1]
    n_tok = nt * tp
    shift = int(math.log2(width))
    idx = lax.broadcasted_iota(I32, (tp, 1), 0) + t * tp
    colpos = idx & (width - 1)
    rowpos = idx >> shift
    n_rows = n_tok // width
    bi = lax.broadcasted_iota(I32, (hblock, hblock), 0)
    bj = lax.broadcasted_iota(I32, (hblock, hblock), 1)
    same_row = (bi >> shift) == (bj >> shift)
    cur = cur_ref[0].astype(F32)
    if vertical:
        prev = jnp.where(t > 0, prev_ref[0].astype(F32), 0.0)
        nxt = jnp.where(t < nt - 1, next_ref[0].astype(F32), 0.0)
    for gi, k in enumerate(POOL_WINDOWS):
        cols = slice(gi * POOL_GROUP_DIM, (gi + 1) * POOL_GROUP_DIM)
        ug = cur[:, cols]
        lo = k // 2
        if vertical:
            above, below = lo * width, (k - 1 - lo) * width
            parts = [prev[halo - above:, cols], ug] + ([nxt[:below, cols]] if below else [])
            e = jnp.concatenate(parts, axis=0)
            step = width
            m = 1
            while m < k:
                n = e.shape[0] - step
                e = e[0:n] + e[step:step + n]
                step *= 2
                m *= 2
            s = e
            cnt_r = (jnp.minimum(rowpos - lo + k, n_rows) - jnp.maximum(rowpos - lo, 0)).astype(F32)
        else:
            s = ug
            cnt_r = jnp.ones((tp, 1), F32)
        band = (same_row & (bj - bi >= -lo) & (bj - bi < k - lo)).astype(BF16)
        s_bf = s.astype(BF16)
        hs = [_dot(band, s_bf[r * hblock:(r + 1) * hblock]) for r in range(tp // hblock)]
        hsum = hs[0] if len(hs) == 1 else jnp.concatenate(hs, axis=0)
        cnt_c = (jnp.minimum(colpos - lo + k, width) - jnp.maximum(colpos - lo, 0)).astype(F32)
        mean = hsum / (cnt_r * cnt_c)
        y = _dot((mean - ug).astype(BF16), pw_ref[gi])
        o_ref[0, :, cols] = (y * ps_ref[:, cols]).astype(BF16)


def _pool_branch(u, lp, on_grid):
    b, l, d = u.shape
    pw, ps = lp["pool_w"], lp["pool_scale"]
    if on_grid:
        width, halo = GRID_W, (max(POOL_WINDOWS) // 2) * GRID_W
        tp = min(1024, l)
        hb = tp // halo
        nhb = l // halo
        in_specs = [pl.BlockSpec((1, halo, d), lambda i, t: (i, jnp.maximum(t * hb - 1, 0), 0)),
                    pl.BlockSpec((1, tp, d), lambda i, t: (i, t, 0)),
                    pl.BlockSpec((1, halo, d), lambda i, t: (i, jnp.minimum((t + 1) * hb, nhb - 1), 0))]
        args = (u, u, u)
        kern = functools.partial(_pool_kernel, width=width, vertical=True, halo=halo, hblock=LANES)
    else:
        tp = l
        in_specs = [pl.BlockSpec((1, tp, d), lambda i, t: (i, t, 0))]
        args = (u,)
        kern = functools.partial(_pool_kernel, width=l, vertical=False, halo=0, hblock=l)
    in_specs += [pl.BlockSpec(pw.shape, lambda i, t: (0, 0, 0)), pl.BlockSpec(ps.shape, lambda i, t: (0, 0))]
    return pl.pallas_call(
        kern,
        grid=(b, l // tp),
        in_specs=in_specs,
        out_specs=pl.BlockSpec((1, tp, d), lambda i, t: (i, t, 0)),
        out_shape=jax.ShapeDtypeStruct((b, l, d), BF16),
        compiler_params=_cparams(("arbitrary", "arbitrary")),
    )(*args, pw, ps)


def _merge_kernel(ya_ref, yp_ref, gt_ref, x_ref, g1_ref, sh2_ref, sc2_ref, bg_ref,
                  wa_ref, wb_ref, wo_ref, lng_ref, lnb_ref, wrh_ref, wrl_ref, br_ref, cin_ref,
                  xo_ref, h2_ref, rt_ref, cout_ref, cnt_ref, *, alpha):
    i = pl.program_id(0)
    j = pl.program_id(1)
    tm = x_ref.shape[1]

    @pl.when((i == 0) & (j == 0))
    def _():
        cnt_ref[...] = cin_ref[...]

    gate = _sigmoid(gt_ref[0].astype(F32) + bg_ref[...])
    merged = (gate[:, :D_MODEL] * _dot(ya_ref[0], wa_ref[...])
              + gate[:, D_MODEL:] * _dot(yp_ref[0], wb_ref[...]))
    out = _dot(merged.astype(BF16), wo_ref[...])
    xn = _layer_norm(alpha * x_ref[0] + g1_ref[0] * out, lng_ref[...], lnb_ref[...])
    xo_ref[0] = xn
    h2 = xn * (1.0 + sc2_ref[0]) + sh2_ref[0]
    h2_ref[0] = _pack_rows(h2)

    h_hi, h_lo = _split_bf16(h2)
    logits = _dot(h_hi, wrh_ref[...]) + _dot(h_lo, wrh_ref[...]) + _dot(h_hi, wrl_ref[...]) + br_ref[...]
    lane = lax.broadcasted_iota(I32, (tm, LANES), 1)
    neg = jnp.float32(-jnp.inf)
    lg = jnp.where(lane < MOE_GROUPS, logits, neg)
    mg = jnp.max(lg, axis=-1, keepdims=True)
    grp = jnp.min(jnp.where(lg == mg, lane, LANES), axis=-1, keepdims=True)
    p_grp = 1.0 / jnp.sum(jnp.exp(lg - mg), axis=-1, keepdims=True)
    lo_lane = ROUTE_LANE0 + grp * MOE_EPG
    le = jnp.where((lane >= lo_lane) & (lane < lo_lane + MOE_EPG), logits, neg)
    v1 = jnp.max(le, axis=-1, keepdims=True)
    i1 = jnp.min(jnp.where(le == v1, lane, LANES), axis=-1, keepdims=True)
    le2 = jnp.where(lane == i1, neg, le)
    v2 = jnp.max(le2, axis=-1, keepdims=True)
    i2 = jnp.min(jnp.where(le2 == v2, lane, LANES), axis=-1, keepdims=True)
    e2 = jnp.exp(v2 - v1)
    w1 = p_grp / (1.0 + e2)
    w2 = p_grp * e2 / (1.0 + e2)

    oh1 = (lane == i1).astype(F32)
    oh2 = (lane == i2).astype(F32)
    oh = oh1 + oh2
    ri = lax.broadcasted_iota(I32, (tm, tm), 0)
    rj = lax.broadcasted_iota(I32, (tm, tm), 1)
    before = _dot((rj < ri).astype(BF16), oh.astype(BF16)) + cnt_ref[...]
    rank1 = jnp.sum(oh1 * before, axis=-1, keepdims=True)
    rank2 = jnp.sum(oh2 * before, axis=-1, keepdims=True)
    cnt_ref[...] = cnt_ref[...] + jnp.sum(oh, axis=0, keepdims=True)
    cout_ref[...] = cnt_ref[...]

    slab = jnp.where(lane == 0, (i1 - ROUTE_LANE0).astype(F32),
           jnp.where(lane == 1, (i2 - ROUTE_LANE0).astype(F32),
           jnp.where(lane == 2, w1,
           jnp.where(lane == 3, w2,
           jnp.where(lane == 4, rank1,
           jnp.where(lane == 5, rank2, 0.0))))))
    rt_ref[0] = slab


def _merge(ya, yp, gates, x, g1, sh2, sc2, lp, cnt_in, alpha):
    b, l, d = x.shape
    tm = min(512, l)
    tok = lambda n: pl.BlockSpec((1, tm, n), lambda i, j: (i, j, 0))
    mod = pl.BlockSpec((1, 1, d), lambda i, j: (i, 0, 0))
    par = lambda a: pl.BlockSpec(a.shape, lambda i, j: (0,) * a.ndim)
    params = (lp["b_gate"], lp["w_branch_a"], lp["w_branch_b"], lp["w_out"], lp["ln1_g"], lp["ln1_b"],
              lp["w_r_hi"], lp["w_r_lo"], lp["b_r"], cnt_in)
    return pl.pallas_call(
        functools.partial(_merge_kernel, alpha=alpha),
        grid=(b, l // tm),
        in_specs=[tok(D_INNER), tok(d), tok(2 * d), tok(d), mod, mod, mod] + [par(a) for a in params],
        out_specs=[tok(d), tok(d // 2), tok(LANES), pl.BlockSpec((1, LANES), lambda i, j: (0, 0))],
        out_shape=[jax.ShapeDtypeStruct((b, l, d), F32), jax.ShapeDtypeStruct((b, l, d // 2), I32),
                   jax.ShapeDtypeStruct((b, l, LANES), F32), jax.ShapeDtypeStruct((1, LANES), F32)],
        scratch_shapes=[pltpu.VMEM((1, LANES), F32)],
        compiler_params=_cparams(("arbitrary", "arbitrary")),
    )(ya, yp, gates, x, g1, sh2, sc2, *params)


DISPATCH_TILE = 2048


def _row_copy(src_row_ref, dst_row_ref, sem):
    return pltpu.make_async_copy(src_row_ref, dst_row_ref, sem)


def _dispatch_kernel(h_ref, dest_ref, xin_any, xout_any, idx_smem, sem_idx, sem_row):
    del xin_any
    n_tiles = h_ref.shape[0]
    cp = pltpu.make_async_copy(dest_ref.at[0, 0], idx_smem, sem_idx)
    cp.start()
    cp.wait()

    def issue(g, carry):
        for k in range(SUBLANES):
            r = g * SUBLANES + k
            src = h_ref.at[g, pl.ds(k, 1)]
            _row_copy(src, xout_any.at[pl.ds(idx_smem[2 * r], 1)], sem_row).start(priority=0)
            _row_copy(src, xout_any.at[pl.ds(idx_smem[2 * r + 1], 1)], sem_row).start(priority=1)
        return carry

    lax.fori_loop(0, n_tiles, issue, 0)

    def drain(g, carry):
        for _ in range(2 * SUBLANES):
            _row_copy(h_ref.at[0, pl.ds(0, 1)], xout_any.at[pl.ds(0, 1)], sem_row).wait()
        return carry

    lax.fori_loop(0, n_tiles, drain, 0)


def _dispatch(h2, dest, xin):
    t, d = h2.shape
    td = min(DISPATCH_TILE, t)
    dest3 = dest.reshape(t // td, 1, 2 * td)
    h2 = h2.reshape(t // SUBLANES, SUBLANES, d)
    return pl.pallas_call(
        _dispatch_kernel,
        grid=(t // td,),
        in_specs=[pl.BlockSpec((td // SUBLANES, SUBLANES, d), lambda i: (i, 0, 0)),
                  pl.BlockSpec((1, 1, 2 * td), lambda i: (i, 0, 0)),
                  pl.BlockSpec(memory_space=pl.ANY)],
        out_specs=pl.BlockSpec(memory_space=pl.ANY),
        out_shape=jax.ShapeDtypeStruct(xin.shape, xin.dtype),
        scratch_shapes=[pltpu.SMEM((2 * td,), I32), pltpu.SemaphoreType.DMA(()), pltpu.SemaphoreType.DMA(())],
        input_output_aliases={2: 0},
        compiler_params=_cparams(("arbitrary",)),
    )(h2, dest3, xin)


SC_CORES = 2
SC_SUBCORES = 16
SC_CHUNK = 128
ROW_TILE = 1024


def _sc_gather_rows(table, idx):
    m = idx.shape[0]
    d = table.shape[1]
    workers = SC_CORES * SC_SUBCORES
    n_chunks = m // (workers * SC_CHUNK)
    assert n_chunks * workers * SC_CHUNK == m, (m, workers, SC_CHUNK)
    mesh = plsc.VectorSubcoreMesh(core_axis_name="c", subcore_axis_name="s",
                                  num_cores=SC_CORES, num_subcores=SC_SUBCORES)

    @functools.partial(
        pl.kernel, mesh=mesh,
        out_type=jax.ShapeDtypeStruct((m, d), table.dtype),
        scratch_types=[pltpu.VMEM((SC_CHUNK,), I32), pltpu.VMEM((SC_CHUNK, d), table.dtype),
                       pltpu.SemaphoreType.DMA],
    )
    def gather(table_hbm, idx_hbm, out_hbm, idx_v, rows_v, sem):
        wid = lax.axis_index("s") * SC_CORES + lax.axis_index("c")

        @pl.loop(0, n_chunks)
        def _(j):
            base = (wid * n_chunks + j) * SC_CHUNK
            pltpu.sync_copy(idx_hbm.at[pl.ds(base, SC_CHUNK)], idx_v)
            pltpu.async_copy(table_hbm.at[idx_v], rows_v, sem).wait()
            pltpu.sync_copy(rows_v, out_hbm.at[pl.ds(base, SC_CHUNK)])

    return gather(table, idx)


def _expert_kernel(be_ref, nu_ref, x_ref, wg_ref, wu_ref, wd_ref, o_ref, wgb_ref, wub_ref, wdb_ref):
    i = pl.program_id(0)

    @pl.when((i == 0) | (be_ref[i] != be_ref[jnp.maximum(i - 1, 0)]))
    def _():
        wgb_ref[...] = wg_ref[0, 0].astype(BF16)
        wub_ref[...] = wu_ref[0, 0].astype(BF16)
        wdb_ref[...] = wd_ref[0, 0].astype(BF16)

    @pl.when(i < nu_ref[0])
    def _():
        xb = _unpack_rows(x_ref[...]).astype(BF16)
        gte = _dot(xb, wgb_ref[...])
        up = _dot(xb, wub_ref[...])
        o_ref[...] = _pack_rows(_dot((_silu(gte) * up).astype(BF16), wdb_ref[...]))

    @pl.when(i >= nu_ref[0])
    def _():
        o_ref[...] = jnp.zeros_like(o_ref)


def _experts(xin, block_e, n_used, w_eg, w_eu, w_ed, layer):
    n_rows, dh = xin.shape
    d = 2 * dh
    nb = n_rows // MOE_BLOCK
    grid_spec = pltpu.PrefetchScalarGridSpec(
        num_scalar_prefetch=2,
        grid=(nb,),
        in_specs=[pl.BlockSpec((MOE_BLOCK, dh), lambda i, be, nu: (i, 0)),
                  pl.BlockSpec((1, 1, d, D_EXPERT), lambda i, be, nu: (layer, be[i], 0, 0)),
                  pl.BlockSpec((1, 1, d, D_EXPERT), lambda i, be, nu: (layer, be[i], 0, 0)),
                  pl.BlockSpec((1, 1, D_EXPERT, d), lambda i, be, nu: (layer, be[i], 0, 0))],
        out_specs=pl.BlockSpec((MOE_BLOCK, dh), lambda i, be, nu: (i, 0)),
        scratch_shapes=[pltpu.VMEM((d, D_EXPERT), BF16), pltpu.VMEM((d, D_EXPERT), BF16),
                        pltpu.VMEM((D_EXPERT, d), BF16)],
    )
    return pl.pallas_call(
        _expert_kernel,
        grid_spec=grid_spec,
        out_shape=jax.ShapeDtypeStruct((n_rows, dh), I32),
        compiler_params=_cparams(("arbitrary",)),
    )(block_e, n_used, xin, w_eg, w_eu, w_ed)


def _combine_kernel(y0_ref, y1_ref, rt_ref, x_ref, g2_ref, lng_ref, lnb_ref, o_ref, *, alpha):
    rt = rt_ref[0]
    y = rt[:, 2:3] * _unpack_rows(y0_ref[...]) + rt[:, 3:4] * _unpack_rows(y1_ref[...])
    o_ref[0] = _layer_norm(alpha * x_ref[0] + g2_ref[0] * y, lng_ref[...], lnb_ref[...])


def _combine(planes, tok0, n_tok, route, x, g2, lp, alpha):
    b, l, d = x.shape
    tc = min(ROW_TILE, l)
    npb = l // tc
    blk0 = tok0 // tc
    blk1 = (n_tok + tok0) // tc
    assert blk0 * tc == tok0 and blk1 * tc == n_tok + tok0
    par = lambda a: pl.BlockSpec(a.shape, lambda i, j: (0,) * a.ndim)
    return pl.pallas_call(
        functools.partial(_combine_kernel, alpha=alpha),
        grid=(b, npb),
        in_specs=[pl.BlockSpec((tc, d // 2), lambda i, j: (blk0 + i * npb + j, 0)),
                  pl.BlockSpec((tc, d // 2), lambda i, j: (blk1 + i * npb + j, 0)),
                  pl.BlockSpec((1, tc, LANES), lambda i, j: (i, j, 0)),
                  pl.BlockSpec((1, tc, d), lambda i, j: (i, j, 0)),
                  pl.BlockSpec((1, 1, d), lambda i, j: (i, 0, 0)),
                  par(lp["ln2_g"]), par(lp["ln2_b"])],
        out_specs=pl.BlockSpec((1, tc, d), lambda i, j: (i, j, 0)),
        out_shape=jax.ShapeDtypeStruct((b, l, d), F32),
        compiler_params=_cparams(("arbitrary", "arbitrary")),
    )(planes, planes, route, x, g2, lp["ln2_g"], lp["ln2_b"])


def _moe_plan(routes, counts):
    cnt = counts[0, ROUTE_LANE0:ROUTE_LANE0 + MOE_EXPERTS].astype(I32)
    padded = (cnt + MOE_BLOCK - 1) // MOE_BLOCK * MOE_BLOCK
    pad_end = jnp.cumsum(padded)
    pad_start = pad_end - padded
    n_assign = sum(r.shape[0] * r.shape[1] for r in routes) * 2
    n_blocks = -(-n_assign // MOE_BLOCK) + MOE_EXPERTS
    first_row = jnp.arange(n_blocks, dtype=I32)[:, None] * MOE_BLOCK
    block_e = jnp.minimum(jnp.sum((pad_end[None, :] <= first_row).astype(I32), axis=1), MOE_EXPERTS - 1)
    n_used = (pad_end[-1:] // MOE_BLOCK).astype(I32)
    eidx = jnp.arange(MOE_EXPERTS, dtype=I32)
    dests = []
    for r in routes:
        e = r[..., 0:2].astype(I32)
        rank = r[..., 4:6].astype(I32)
        start = jnp.sum(jnp.where(e[..., None] == eidx, pad_start, 0), axis=-1)
        dests.append((start + rank).reshape(-1, 2))
    return dests, block_e, n_used, n_blocks * MOE_BLOCK


def _layer_params(i, p):
    w_in = p["w_in"][i]
    s0, s1, s2, s3 = D_INNER, D_INNER + D_XBC, D_INNER + D_XBC + 2 * HEADS, D_INNER + D_XBC + 2 * HEADS + D_MODEL
    pad = LANES - 2 * HEADS
    wdt = jnp.pad(w_in[:, s1:s2], ((0, 0), (0, pad)))
    w_r = jnp.concatenate([p["w_router_group"][i],
                           jnp.transpose(p["w_router_expert"][i], (1, 0, 2)).reshape(D_MODEL, MOE_EXPERTS)], axis=1)
    w_r = jnp.pad(w_r, ((0, 0), (0, LANES - w_r.shape[1])))
    w_r_hi = w_r.astype(BF16)
    b_r = jnp.concatenate([p["b_router_group"][i], p["b_router_expert"][i].reshape(-1)])
    return {
        "w_in": tuple(w.astype(BF16) for w in (w_in[:, :s0], w_in[:, s0:s1], w_in[:, s2:s3], w_in[:, s3:], wdt)),
        "conv_w": p["conv_w"][i], "conv_b": p["conv_b"][i][None],
        "dt_bias": jnp.pad(p["dt_bias"][i].reshape(1, -1), ((0, 0), (0, pad))),
        "a_log": jnp.pad(p["a_log"][i].reshape(1, -1), ((0, 0), (0, pad))),
        "d_skip_x": jnp.repeat(p["d_skip"][i], HEADDIM)[None],
        "ssd_norm_g": p["ssd_norm_g"][i][None],
        "pool_w": p["pool_w"][i].astype(BF16), "pool_scale": p["pool_scale"][i][None],
        "b_gate": p["b_gate"][i][None],
        "w_branch_a": p["w_branch_a"][i].astype(BF16), "w_branch_b": p["w_branch_b"][i].astype(BF16),
        "w_out": p["w_out"][i].astype(BF16),
        "ln1_g": p["ln1_g"][i][None], "ln1_b": p["ln1_b"][i][None],
        "ln2_g": p["ln2_g"][i][None], "ln2_b": p["ln2_b"][i][None],
        "w_r_hi": w_r_hi, "w_r_lo": (w_r - w_r_hi.astype(F32)).astype(BF16),
        "b_r": jnp.pad(b_r, (0, LANES - b_r.shape[0]))[None],
    }


def _head_expanders():
    col_head = jnp.arange(D_INNER, dtype=I32) // HEADDIM
    row = jnp.arange(LANES, dtype=I32)[:, None] % (2 * HEADS)
    return (row == col_head).astype(BF16), (row == col_head + HEADS).astype(BF16)


def kernel(x, c, ctx, c_ctx, w_ada, b_ada, w_in, b_gate, conv_w, conv_b, dt_bias, a_log, d_skip, ssd_norm_g, pool_w, pool_scale, w_branch_a, w_branch_b, w_out, ln1_g, ln1_b, ln2_g, ln2_b, w_router_group, b_router_group, w_router_expert, b_router_expert, w_expert_gate, w_expert_up, w_expert_down):
    p = dict(w_in=w_in, b_gate=b_gate, conv_w=conv_w, conv_b=conv_b, dt_bias=dt_bias, a_log=a_log, d_skip=d_skip,
             ssd_norm_g=ssd_norm_g, pool_w=pool_w, pool_scale=pool_scale, w_branch_a=w_branch_a,
             w_branch_b=w_branch_b, w_out=w_out, ln1_g=ln1_g, ln1_b=ln1_b, ln2_g=ln2_g, ln2_b=ln2_b,
             w_router_group=w_router_group, b_router_group=b_router_group, w_router_expert=w_router_expert,
             b_router_expert=b_router_expert, w_expert_gate=w_expert_gate, w_expert_up=w_expert_up,
             w_expert_down=w_expert_down)
    b, l, d = x.shape
    depth = w_ada.shape[0]
    alpha = (2.0 * depth) ** DEPTH_ALPHA_POW
    n_mod = -(-(b + 1) // 8) * 8
    cvec = jnp.zeros((n_mod, d), F32).at[:b].set(c).at[b].set(c_ctx)
    mods = _ada_mods(cvec, w_ada, b_ada)
    r_f, r_b = _head_expanders()
    zero_state = jnp.zeros((b, STATE, D_INNER), F32)
    zero_cnt = jnp.zeros((1, LANES), F32)
    xl, xc = x, ctx
    for i in range(depth):
        last = i == depth - 1
        lp = _layer_params(i, p)
        mod_l = [mods[i, :b, k * d:(k + 1) * d][:, None, :] for k in range(6)]
        mod_c = [jnp.broadcast_to(mods[i, b, k * d:(k + 1) * d], (b, 1, d)) for k in range(6)]

        zc, act_c, uc, gc, dtc = _in_proj(xc, mod_c[0], mod_c[1], lp["w_in"], lp["conv_w"], lp["conv_b"])
        if last:
            (s_f,) = _ssd_state_scan(act_c, dtc, zero_state, lp, r_f, rev=False, emit=False)
            (s_b,) = _ssd_state_scan(act_c, dtc, zero_state, lp, r_b, rev=True, emit=False)
        else:
            stb_c, s_b = _ssd_state_scan(act_c, dtc, zero_state, lp, r_b, rev=True, emit=True)
            ya_c, s_f = _ssd_main(act_c, dtc, zc, stb_c, zero_state, lp, r_f, r_b)
            yp_c = _pool_branch(uc, lp, False)

        zl, act_l, ul, gl, dtl = _in_proj(xl, mod_l[0], mod_l[1], lp["w_in"], lp["conv_w"], lp["conv_b"])
        stb_l, _ = _ssd_state_scan(act_l, dtl, s_b, lp, r_b, rev=True, emit=True)
        ya_l, _ = _ssd_main(act_l, dtl, zl, stb_l, s_f, lp, r_f, r_b)
        yp_l = _pool_branch(ul, lp, True)
        xl, h2_l, rt_l, cnt = _merge(ya_l, yp_l, gl, xl, mod_l[2], mod_l[3], mod_l[4], lp, zero_cnt, alpha)
        streams = [(h2_l, rt_l)]
        if not last:
            xc, h2_c, rt_c, cnt = _merge(ya_c, yp_c, gc, xc, mod_c[2], mod_c[3], mod_c[4], lp, cnt, alpha)
            streams.append((h2_c, rt_c))

        dests, block_e, n_used, n_rows = _moe_plan([rt for _, rt in streams], cnt)
        dest_all = jnp.concatenate(dests, axis=0)
        n_tok = dest_all.shape[0]
        xin = jnp.zeros((n_rows, d // 2), I32)
        for (h2, _), dest in zip(streams, dests):
            xin = _dispatch(h2.reshape(-1, d // 2), dest, xin)
        yb = _experts(xin, block_e, n_used, w_expert_gate, w_expert_up, w_expert_down, i)
        planes = _sc_gather_rows(yb, dest_all.T.reshape(-1))
        xl = _combine(planes, 0, n_tok, rt_l, xl, mod_l[5], lp, alpha)
        if not last:
            xc = _combine(planes, b * l, n_tok, rt_c, xc, mod_c[5], lp, alpha)
    return xl
```

```python
import functools
import math

import jax
import jax.numpy as jnp
from jax import lax
from jax.experimental import pallas as pl
from jax.experimental.pallas import tpu as pltpu
from jax.experimental.pallas import tpu_sc as plsc

F32 = jnp.float32
BF16 = jnp.bfloat16
I32 = jnp.int32

D_MODEL = 1024
D_INNER = 2048
HEADS = 32
HEADDIM = 64
GROUPS = 4
STATE = 128
D_BC = GROUPS * STATE
D_XBC = D_INNER + 2 * D_BC
CONV_K = 5
CHUNK = 128
GRID_W = 64
POOL_WINDOWS = (2, 4, 8, 16)
POOL_GROUP_DIM = 256
MOE_GROUPS = 4
MOE_EPG = 8
MOE_EXPERTS = 32
D_EXPERT = 512
MOE_BLOCK = 512
DEPTH_ALPHA_POW = 0.25
NORM_EPS = 1e-5
LANES = 128
SUBLANES = 8
ROUTE_LANE0 = MOE_GROUPS
VMEM_LIMIT = 56 * 1024 * 1024


def _cparams(sem, vmem=VMEM_LIMIT):
    return pltpu.CompilerParams(dimension_semantics=sem, vmem_limit_bytes=vmem)


def _sigmoid(v):
    return 0.5 + 0.5 * jnp.tanh(0.5 * v)


def _silu(v):
    h = 0.5 * v
    return h + h * jnp.tanh(h)


def _layer_norm(v, g, b):
    mu = jnp.mean(v, axis=-1, keepdims=True)
    d = v - mu
    var = jnp.mean(d * d, axis=-1, keepdims=True)
    return d * lax.rsqrt(var + NORM_EPS) * g + b


def _split_bf16(v):
    hi = v.astype(BF16)
    lo = (v - hi.astype(F32)).astype(BF16)
    return hi, lo


def _dot(a, b):
    return jnp.dot(a, b, preferred_element_type=F32)


U32 = jnp.uint32
HI16 = 0xFFFF0000


def _pack_rows(v):
    half = v.shape[1] // 2
    lo = lax.bitcast_convert_type(v[:, :half].astype(BF16).astype(F32), U32)
    hi = lax.bitcast_convert_type(v[:, half:].astype(BF16).astype(F32), U32)
    return lax.bitcast_convert_type((lo >> 16) | (hi & U32(HI16)), I32)


def _unpack_rows(w):
    w = lax.bitcast_convert_type(w, U32)
    lo = lax.bitcast_convert_type(w << 16, F32)
    hi = lax.bitcast_convert_type(w & U32(HI16), F32)
    return jnp.concatenate([lo, hi], axis=1)


def _ada_kernel(c_ref, w_ref, b_ref, o_ref):
    s = _silu(c_ref[...]).astype(BF16)
    o_ref[0] = _dot(s, w_ref[0].astype(BF16)) + b_ref[0]


def _ada_mods(cvec, w_ada, b_ada):
    depth, d, n = w_ada.shape
    r = cvec.shape[0]
    tn = D_MODEL
    return pl.pallas_call(
        _ada_kernel,
        grid=(depth, n // tn),
        in_specs=[pl.BlockSpec((r, d), lambda l, j: (0, 0)),
                  pl.BlockSpec((1, d, tn), lambda l, j: (l, 0, j)),
                  pl.BlockSpec((1, 1, tn), lambda l, j: (l, 0, j))],
        out_specs=pl.BlockSpec((1, r, tn), lambda l, j: (l, 0, j)),
        out_shape=jax.ShapeDtypeStruct((depth, r, n), F32),
        compiler_params=_cparams(("arbitrary", "arbitrary")),
    )(cvec, w_ada, b_ada.reshape(depth, 1, n))


CONV_HALO = 8
CONV_SLAB = 512


def _inproj_kernel(xp_ref, x_ref, xn_ref, sh_ref, sc_ref, wz_ref, wx_ref, wu_ref, wg_ref, wdt_ref, cw_ref, cb_ref,
                   z_ref, act_ref, u_ref, g_ref, dt_ref):
    j = pl.program_id(1)
    nj = pl.num_programs(1)
    tm = x_ref.shape[1]
    n = tm + 2 * CONV_HALO
    mod = lambda v: (v * (1.0 + sc_ref[0]) + sh_ref[0]).astype(BF16)
    h = mod(x_ref[0])
    h_ext = mod(jnp.concatenate([xp_ref[0], x_ref[0], xn_ref[0]], axis=0))

    row = lax.broadcasted_iota(I32, (n, 1), 0)
    inside = ((row >= CONV_HALO) | (j > 0)) & ((row < CONV_HALO + tm) | (j < nj - 1))

    z_ref[0] = _dot(h, wz_ref[...]).astype(BF16)
    u_ref[0] = _dot(h, wu_ref[...]).astype(BF16)
    g_ref[0] = _dot(h, wg_ref[...]).astype(BF16)
    dt_ref[0] = _dot(h, wdt_ref[...])
    down = lambda v: pltpu.roll(v, 1, 0)
    up = lambda v: pltpu.roll(v, n - 1, 0)
    for s in range(D_XBC // CONV_SLAB):
        cols = slice(s * CONV_SLAB, (s + 1) * CONV_SLAB)
        e = jnp.where(inside, _dot(h_ext, wx_ref[:, cols]), 0.0)
        y = [e * cw_ref[k:k + 1, cols] for k in range(CONV_K)]
        acc = y[2] + down(y[1] + down(y[0])) + up(y[3] + up(y[4]))
        act_ref[0, :, cols] = _silu(acc[CONV_HALO:CONV_HALO + tm] + cb_ref[:, cols]).astype(BF16)


def _in_proj(x, shift, scale, wts, conv_w, conv_b):
    b, l, d = x.shape
    tm = min(512, l)
    hpt = tm // CONV_HALO
    nh = l // CONV_HALO
    wz, wx, wu, wg, wdt = wts
    tok = lambda n: pl.BlockSpec((1, tm, n), lambda i, j: (i, j, 0))
    mod = pl.BlockSpec((1, 1, d), lambda i, j: (i, 0, 0))
    wspec = lambda w: pl.BlockSpec(w.shape, lambda i, j: (0, 0), pipeline_mode=pl.Buffered(1))
    par = lambda a: pl.BlockSpec(a.shape, lambda i, j: (0, 0))
    halo_prev = pl.BlockSpec((1, CONV_HALO, d), lambda i, j: (i, jnp.maximum(j * hpt - 1, 0), 0))
    halo_next = pl.BlockSpec((1, CONV_HALO, d), lambda i, j: (i, jnp.minimum((j + 1) * hpt, nh - 1), 0))
    return pl.pallas_call(
        _inproj_kernel,
        grid=(b, l // tm),
        in_specs=[halo_prev, tok(d), halo_next, mod, mod, wspec(wz), wspec(wx), wspec(wu), wspec(wg), wspec(wdt),
                  par(conv_w), par(conv_b)],
        out_specs=[tok(D_INNER), tok(D_XBC), tok(D_MODEL), tok(2 * D_MODEL), tok(LANES)],
        out_shape=[jax.ShapeDtypeStruct((b, l, D_INNER), BF16),
                   jax.ShapeDtypeStruct((b, l, D_XBC), BF16),
                   jax.ShapeDtypeStruct((b, l, D_MODEL), BF16),
                   jax.ShapeDtypeStruct((b, l, 2 * D_MODEL), BF16),
                   jax.ShapeDtypeStruct((b, l, LANES), F32)],
        compiler_params=_cparams(("arbitrary", "arbitrary")),
    )(x, x, x, shift, scale, wz, wx, wu, wg, wdt, conv_w, conv_b)


def _dt_prep(dt_raw, dtb_ref, alog_ref):
    q = CHUNK
    lane = lax.broadcasted_iota(I32, (q, LANES), 1)
    raw = dt_raw + dtb_ref[...]
    dt = jnp.maximum(raw, 0.0) + jnp.log1p(jnp.exp(-jnp.abs(raw)))
    dt = jnp.where(lane < 2 * HEADS, dt, 0.0)
    a = dt * (-jnp.exp(alog_ref[...]))
    ii = lax.broadcasted_iota(I32, (q, q), 0)
    jj = lax.broadcasted_iota(I32, (q, q), 1)
    tri = (jj <= ii).astype(BF16)
    a_hi, a_lo = _split_bf16(a)
    acs = _dot(tri, a_hi) + _dot(tri, a_lo)
    tot = acs[q - 1:q, :]
    return dt, a, acs, tot


def _expand(v, lane0, r_ref):
    return _dot(_expand_src(v, lane0), r_ref[...])


def _expand_src(v, lane0):
    lane = lax.broadcasted_iota(I32, v.shape, 1)
    vm = jnp.where((lane >= lane0) & (lane < lane0 + HEADS), v, 0.0)
    hi = vm.astype(BF16).astype(F32)
    return (hi + pltpu.roll(vm - hi, 2 * HEADS, 1)).astype(BF16)


def _state_update(st_ref, bm, xw_bf, decay_x):
    gw = D_INNER // GROUPS
    for g in range(GROUPS):
        bgt = bm[:, g * STATE:(g + 1) * STATE].T.astype(BF16)
        cols = slice(g * gw, (g + 1) * gw)
        st_ref[:, cols] = st_ref[:, cols] * decay_x[:, cols] + _dot(bgt, xw_bf[:, cols])


SCAN_CHUNKS = 8


def _ssd_state_kernel(*refs, rev, emit, nch):
    act_ref, dt_ref, h0_ref, dtb_ref, alog_ref, r_ref = refs[:6]
    if emit:
        stout_ref, sfin_ref, st_ref = refs[6:]
    else:
        sfin_ref, st_ref = refs[6:]
    s = pl.program_id(1)
    ns = pl.num_programs(1)

    @pl.when(s == 0)
    def _():
        st_ref[...] = h0_ref[0]

    for k in range(nch):
        cc = (nch - 1 - k) if rev else k
        rows = slice(cc * CHUNK, (cc + 1) * CHUNK)
        if emit:
            stout_ref[0, cc] = st_ref[...].astype(BF16)
        xs = act_ref[0, rows, :D_INNER].astype(F32)
        bm = act_ref[0, rows, D_INNER:D_INNER + D_BC].astype(F32)
        dt, a, acs, tot = _dt_prep(dt_ref[0, rows, :], dtb_ref, alog_ref)
        if rev:
            w = jnp.exp(acs - a) * dt
            lane0 = HEADS
        else:
            w = jnp.exp(tot - acs) * dt
            lane0 = 0
        w_x = _expand(w, lane0, r_ref)
        dec_x = _expand(jnp.broadcast_to(jnp.exp(tot), (16, LANES)), lane0, r_ref)[0:1]
        _state_update(st_ref, bm, (w_x * xs).astype(BF16), dec_x)

    @pl.when(s == ns - 1)
    def _():
        sfin_ref[0] = st_ref[...]


def _ssd_state_scan(act, dt, h0, lp, r_mat, *, rev, emit):
    b, l, _ = act.shape
    nc = l // CHUNK
    nch = min(SCAN_CHUNKS, nc)
    ns = nc // nch
    seq = (lambda s: ns - 1 - s) if rev else (lambda s: s)
    par = lambda shape: pl.BlockSpec(shape, lambda i, s: (0,) * len(shape))
    in_specs = [
        pl.BlockSpec((1, nch * CHUNK, D_XBC), lambda i, s: (i, seq(s), 0)),
        pl.BlockSpec((1, nch * CHUNK, LANES), lambda i, s: (i, seq(s), 0)),
        pl.BlockSpec((1, STATE, D_INNER), lambda i, s: (i, 0, 0)),
        par((1, LANES)), par((1, LANES)), par((LANES, D_INNER)),
    ]
    out_specs = [pl.BlockSpec((1, STATE, D_INNER), lambda i, s: (i, 0, 0))]
    out_shape = [jax.ShapeDtypeStruct((b, STATE, D_INNER), F32)]
    if emit:
        out_specs = [pl.BlockSpec((1, nch, STATE, D_INNER), lambda i, s: (i, seq(s), 0, 0))] + out_specs
        out_shape = [jax.ShapeDtypeStruct((b, nc, STATE, D_INNER), BF16)] + out_shape
    return pl.pallas_call(
        functools.partial(_ssd_state_kernel, rev=rev, emit=emit, nch=nch),
        grid=(b, ns),
        in_specs=in_specs, out_specs=out_specs, out_shape=out_shape,
        scratch_shapes=[pltpu.VMEM((STATE, D_INNER), F32)],
        compiler_params=_cparams(("arbitrary", "arbitrary")),
    )(act, dt, h0, lp["dt_bias"], lp["a_log"], r_mat)


MAIN_CHUNKS = 4


def _ssd_main_kernel(act_ref, dt_ref, z_ref, stb_ref, h0_ref,
                     dtb_ref, alog_ref, dsk_ref, ng_ref, rf_ref, rb_ref,
                     y_ref, sfin_ref, st_ref, ys_ref, *, nch):
    c = pl.program_id(1)
    nc = pl.num_programs(1)

    @pl.when(c == 0)
    def _():
        st_ref[...] = h0_ref[0]

    for k in range(nch):
        rows = pl.ds(k * CHUNK, CHUNK)
        _ssd_main_chunk(act_ref.at[0, rows], dt_ref.at[0, rows], z_ref.at[0, rows], stb_ref.at[0, k],
                        dtb_ref, alog_ref, dsk_ref, ng_ref, rf_ref, rb_ref, y_ref.at[0, rows], st_ref, ys_ref)

    @pl.when(c == nc - 1)
    def _():
        sfin_ref[0] = st_ref[...]


def _ssd_main_chunk(act_ref, dt_ref, z_ref, stb_ref, dtb_ref, alog_ref, dsk_ref, ng_ref, rf_ref, rb_ref,
                    y_ref, st_ref, ys_ref):
    q = CHUNK
    dt, a, acs, tot = _dt_prep(dt_ref[...], dtb_ref, alog_ref)
    eb = acs - a

    ef_s = _expand_src(jnp.exp(acs), 0)
    eb_s = _expand_src(jnp.exp(tot - eb), HEADS)
    tf_s = _expand_src(jnp.exp(tot - acs) * dt, 0)
    dec_s = _expand_src(jnp.broadcast_to(jnp.exp(tot), (16, LANES)), 0)

    lane = lax.broadcasted_iota(I32, (q, LANES), 1)
    ldt = jnp.log(dt)
    qm = jnp.where(lane < HEADS, acs - ldt, jnp.where(lane < 2 * HEADS, eb + ldt, pltpu.roll(dt, 2 * HEADS, 1)))
    qt = qm.T
    ii = lax.broadcasted_iota(I32, (q, q), 0)
    jj = lax.broadcasted_iota(I32, (q, q), 1)
    lower = jj <= ii
    diag = jj == ii
    lane2 = lax.broadcasted_iota(I32, (q, LANES), 1)
    left = lane2 < HEADDIM

    gw = D_INNER // GROUPS
    hpg = HEADS // GROUPS
    for g in range(GROUPS):
        bg = act_ref[:, D_INNER + g * STATE:D_INNER + (g + 1) * STATE]
        cg = act_ref[:, D_INNER + D_BC + g * STATE:D_INNER + D_BC + (g + 1) * STATE]
        cb = lax.dot_general(cg, bg, (((1,), (1,)), ((), ())), preferred_element_type=F32)
        cols = slice(g * gw, (g + 1) * gw)
        y_off = (_dot(ef_s, rf_ref[:, cols]) * _dot(cg, st_ref[:, cols].astype(BF16))
                 + _dot(eb_s, rb_ref[:, cols]) * _dot(cg, stb_ref[:, cols]))
        for pr in range(hpg // 2):
            ms = []
            for hh in range(2):
                h = g * hpg + pr * 2 + hh
                afc = acs[:, h:h + 1]
                afr = qt[h:h + 1, :]
                ebc = eb[:, HEADS + h:HEADS + h + 1]
                ebr = qt[HEADS + h:HEADS + h + 1, :]
                wb = qt[3 * HEADS + h:3 * HEADS + h + 1, :]
                arg = jnp.where(lower, afc - afr, ebr - ebc)
                ms.append((cb * (jnp.exp(arg) + jnp.where(diag, wb, 0.0))).astype(BF16))
            lhs = jnp.concatenate(ms, axis=1)
            c0 = g * gw + pr * 2 * HEADDIM
            xp = act_ref[:, c0:c0 + 2 * HEADDIM]
            zero = jnp.zeros_like(xp)
            rhs = jnp.concatenate([jnp.where(left, xp, zero), jnp.where(left, zero, xp)], axis=0)
            y_pair = _dot(lhs, rhs) + y_off[:, pr * 2 * HEADDIM:(pr + 1) * 2 * HEADDIM]
            y_pair = y_pair + dsk_ref[:, c0:c0 + 2 * HEADDIM] * xp.astype(F32)
            ys_ref[:, c0:c0 + 2 * HEADDIM] = y_pair

        xw = (_dot(tf_s, rf_ref[:, cols]) * act_ref[:, cols].astype(F32)).astype(BF16)
        bgt = bg.astype(F32).T.astype(BF16)
        st_ref[:, cols] = st_ref[:, cols] * _dot(dec_s, rf_ref[:, cols])[0:1] + _dot(bgt, xw)

    y = ys_ref[...]
    yz = y * _silu(z_ref[...].astype(F32))
    ms2 = jnp.mean(yz * yz, axis=-1, keepdims=True)
    y_ref[...] = (yz * lax.rsqrt(ms2 + NORM_EPS) * ng_ref[...]).astype(BF16)


def _ssd_main(act, dt, z, stb, h0, lp, r_f, r_b):
    b, l, _ = act.shape
    nc = l // CHUNK
    nch = min(MAIN_CHUNKS, nc)
    rows = nch * CHUNK
    par = lambda shape: pl.BlockSpec(shape, lambda i, s: (0,) * len(shape))
    in_specs = [
        pl.BlockSpec((1, rows, D_XBC), lambda i, s: (i, s, 0)),
        pl.BlockSpec((1, rows, LANES), lambda i, s: (i, s, 0)),
        pl.BlockSpec((1, rows, D_INNER), lambda i, s: (i, s, 0)),
        pl.BlockSpec((1, nch, STATE, D_INNER), lambda i, s: (i, s, 0, 0)),
        pl.BlockSpec((1, STATE, D_INNER), lambda i, s: (i, 0, 0)),
        par((1, LANES)), par((1, LANES)),
        par((1, D_INNER)), par((1, D_INNER)), par((LANES, D_INNER)), par((LANES, D_INNER)),
    ]
    return pl.pallas_call(
        functools.partial(_ssd_main_kernel, nch=nch),
        grid=(b, nc // nch),
        in_specs=in_specs,
        out_specs=[pl.BlockSpec((1, rows, D_INNER), lambda i, s: (i, s, 0)),
                   pl.BlockSpec((1, STATE, D_INNER), lambda i, s: (i, 0, 0))],
        out_shape=[jax.ShapeDtypeStruct((b, l, D_INNER), BF16),
                   jax.ShapeDtypeStruct((b, STATE, D_INNER), F32)],
        scratch_shapes=[pltpu.VMEM((STATE, D_INNER), F32), pltpu.VMEM((CHUNK, D_INNER), F32)],
        compiler_params=_cparams(("arbitrary", "arbitrary")),
    )(act, dt, z, stb, h0, lp["dt_bias"], lp["a_log"], lp["d_skip_x"], lp["ssd_norm_g"], r_f, r_b)


def _pool_kernel(*refs, width, vertical, halo, hblock):
    if vertical:
        prev_ref, cur_ref, next_ref, pw_ref, ps_ref, o_ref = refs
    else:
        cur_ref, pw_ref, ps_ref, o_ref = refs
    t = pl.program_id(1)
    nt = pl.num_programs(1)
    tp = cur_ref.shape[1]
    n_tok = nt * tp
    shift = int(math.log2(width))
    idx = lax.broadcasted_iota(I32, (tp, 1), 0) + t * tp
    colpos = idx & (width - 1)
    rowpos = idx >> shift
    n_rows = n_tok // width
    bi = lax.broadcasted_iota(I32, (hblock, hblock), 0)
    bj = lax.broadcasted_iota(I32, (hblock, hblock), 1)
    same_row = (bi >> shift) == (bj >> shift)
    cur = cur_ref[0].astype(F32)
    if vertical:
        prev = jnp.where(t > 0, prev_ref[0].astype(F32), 0.0)
        nxt = jnp.where(t < nt - 1, next_ref[0].astype(F32), 0.0)
    for gi, k in enumerate(POOL_WINDOWS):
        cols = slice(gi * POOL_GROUP_DIM, (gi + 1) * POOL_GROUP_DIM)
        ug = cur[:, cols]
        lo = k // 2
        if vertical:
            above, below = lo * width, (k - 1 - lo) * width
            parts = [prev[halo - above:, cols], ug] + ([nxt[:below, cols]] if below else [])
            e = jnp.concatenate(parts, axis=0)
            step = width
            m = 1
            while m < k:
                n = e.shape[0] - step
                e = e[0:n] + e[step:step + n]
                step *= 2
                m *= 2
            s = e
            cnt_r = (jnp.minimum(rowpos - lo + k, n_rows) - jnp.maximum(rowpos - lo, 0)).astype(F32)
        else:
            s = ug
            cnt_r = jnp.ones((tp, 1), F32)
        band = (same_row & (bj - bi >= -lo) & (bj - bi < k - lo)).astype(BF16)
        s_bf = s.astype(BF16)
        hs = [_dot(band, s_bf[r * hblock:(r + 1) * hblock]) for r in range(tp // hblock)]
        hsum = hs[0] if len(hs) == 1 else jnp.concatenate(hs, axis=0)
        cnt_c = (jnp.minimum(colpos - lo + k, width) - jnp.maximum(colpos - lo, 0)).astype(F32)
        mean = hsum / (cnt_r * cnt_c)
        y = _dot((mean - ug).astype(BF16), pw_ref[gi])
        o_ref[0, :, cols] = (y * ps_ref[:, cols]).astype(BF16)


def _pool_branch(u, lp, on_grid):
    b, l, d = u.shape
    pw, ps = lp["pool_w"], lp["pool_scale"]
    if on_grid:
        width, halo = GRID_W, (max(POOL_WINDOWS) // 2) * GRID_W
        tp = min(1024, l)
        hb = tp // halo
        nhb = l // halo
        in_specs = [pl.BlockSpec((1, halo, d), lambda i, t: (i, jnp.maximum(t * hb - 1, 0), 0)),
                    pl.BlockSpec((1, tp, d), lambda i, t: (i, t, 0)),
                    pl.BlockSpec((1, halo, d), lambda i, t: (i, jnp.minimum((t + 1) * hb, nhb - 1), 0))]
        args = (u, u, u)
        kern = functools.partial(_pool_kernel, width=width, vertical=True, halo=halo, hblock=LANES)
    else:
        tp = l
        in_specs = [pl.BlockSpec((1, tp, d), lambda i, t: (i, t, 0))]
        args = (u,)
        kern = functools.partial(_pool_kernel, width=l, vertical=False, halo=0, hblock=l)
    in_specs += [pl.BlockSpec(pw.shape, lambda i, t: (0, 0, 0)), pl.BlockSpec(ps.shape, lambda i, t: (0, 0))]
    return pl.pallas_call(
        kern,
        grid=(b, l // tp),
        in_specs=in_specs,
        out_specs=pl.BlockSpec((1, tp, d), lambda i, t: (i, t, 0)),
        out_shape=jax.ShapeDtypeStruct((b, l, d), BF16),
        compiler_params=_cparams(("arbitrary", "arbitrary")),
    )(*args, pw, ps)


def _merge_kernel(ya_ref, yp_ref, gt_ref, x_ref, g1_ref, sh2_ref, sc2_ref, bg_ref,
                  wa_ref, wb_ref, wo_ref, lng_ref, lnb_ref, wrh_ref, wrl_ref, br_ref, cin_ref,
                  xo_ref, h2_ref, rt_ref, cout_ref, cnt_ref, *, alpha):
    i = pl.program_id(0)
    j = pl.program_id(1)
    tm = x_ref.shape[1]

    @pl.when((i == 0) & (j == 0))
    def _():
        cnt_ref[...] = cin_ref[...]

    gate = _sigmoid(gt_ref[0].astype(F32) + bg_ref[...])
    merged = (gate[:, :D_MODEL] * _dot(ya_ref[0], wa_ref[...])
              + gate[:, D_MODEL:] * _dot(yp_ref[0], wb_ref[...]))
    out = _dot(merged.astype(BF16), wo_ref[...])
    xn = _layer_norm(alpha * x_ref[0] + g1_ref[0] * out, lng_ref[...], lnb_ref[...])
    xo_ref[0] = xn
    h2 = xn * (1.0 + sc2_ref[0]) + sh2_ref[0]
    h2_ref[0] = _pack_rows(h2)

    h_hi, h_lo = _split_bf16(h2)
    logits = _dot(h_hi, wrh_ref[...]) + _dot(h_lo, wrh_ref[...]) + _dot(h_hi, wrl_ref[...]) + br_ref[...]
    lane = lax.broadcasted_iota(I32, (tm, LANES), 1)
    neg = jnp.float32(-jnp.inf)
    lg = jnp.where(lane < MOE_GROUPS, logits, neg)
    mg = jnp.max(lg, axis=-1, keepdims=True)
    grp = jnp.min(jnp.where(lg == mg, lane, LANES), axis=-1, keepdims=True)
    p_grp = 1.0 / jnp.sum(jnp.exp(lg - mg), axis=-1, keepdims=True)
    lo_lane = ROUTE_LANE0 + grp * MOE_EPG
    le = jnp.where((lane >= lo_lane) & (lane < lo_lane + MOE_EPG), logits, neg)
    v1 = jnp.max(le, axis=-1, keepdims=True)
    i1 = jnp.min(jnp.where(le == v1, lane, LANES), axis=-1, keepdims=True)
    le2 = jnp.where(lane == i1, neg, le)
    v2 = jnp.max(le2, axis=-1, keepdims=True)
    i2 = jnp.min(jnp.where(le2 == v2, lane, LANES), axis=-1, keepdims=True)
    e2 = jnp.exp(v2 - v1)
    w1 = p_grp / (1.0 + e2)
    w2 = p_grp * e2 / (1.0 + e2)

    oh1 = (lane == i1).astype(F32)
    oh2 = (lane == i2).astype(F32)
    oh = oh1 + oh2
    ri = lax.broadcasted_iota(I32, (tm, tm), 0)
    rj = lax.broadcasted_iota(I32, (tm, tm), 1)
    before = _dot((rj < ri).astype(BF16), oh.astype(BF16)) + cnt_ref[...]
    rank1 = jnp.sum(oh1 * before, axis=-1, keepdims=True)
    rank2 = jnp.sum(oh2 * before, axis=-1, keepdims=True)
    cnt_ref[...] = cnt_ref[...] + jnp.sum(oh, axis=0, keepdims=True)
    cout_ref[...] = cnt_ref[...]

    slab = jnp.where(lane == 0, (i1 - ROUTE_LANE0).astype(F32),
           jnp.where(lane == 1, (i2 - ROUTE_LANE0).astype(F32),
           jnp.where(lane == 2, w1,
           jnp.where(lane == 3, w2,
           jnp.where(lane == 4, rank1,
           jnp.where(lane == 5, rank2, 0.0))))))
    rt_ref[0] = slab


def _merge(ya, yp, gates, x, g1, sh2, sc2, lp, cnt_in, alpha):
    b, l, d = x.shape
    tm = min(512, l)
    tok = lambda n: pl.BlockSpec((1, tm, n), lambda i, j: (i, j, 0))
    mod = pl.BlockSpec((1, 1, d), lambda i, j: (i, 0, 0))
    par = lambda a: pl.BlockSpec(a.shape, lambda i, j: (0,) * a.ndim)
    params = (lp["b_gate"], lp["w_branch_a"], lp["w_branch_b"], lp["w_out"], lp["ln1_g"], lp["ln1_b"],
              lp["w_r_hi"], lp["w_r_lo"], lp["b_r"], cnt_in)
    return pl.pallas_call(
        functools.partial(_merge_kernel, alpha=alpha),
        grid=(b, l // tm),
        in_specs=[tok(D_INNER), tok(d), tok(2 * d), tok(d), mod, mod, mod] + [par(a) for a in params],
        out_specs=[tok(d), tok(d // 2), tok(LANES), pl.BlockSpec((1, LANES), lambda i, j: (0, 0))],
        out_shape=[jax.ShapeDtypeStruct((b, l, d), F32), jax.ShapeDtypeStruct((b, l, d // 2), I32),
                   jax.ShapeDtypeStruct((b, l, LANES), F32), jax.ShapeDtypeStruct((1, LANES), F32)],
        scratch_shapes=[pltpu.VMEM((1, LANES), F32)],
        compiler_params=_cparams(("arbitrary", "arbitrary")),
    )(ya, yp, gates, x, g1, sh2, sc2, *params)


DISPATCH_TILE = 2048


def _row_copy(src_row_ref, dst_row_ref, sem):
    return pltpu.make_async_copy(src_row_ref, dst_row_ref, sem)


def _dispatch_kernel(h_ref, dest_ref, xin_any, xout_any, idx_smem, sem_idx, sem_row):
    del xin_any
    n_tiles = h_ref.shape[0]
    cp = pltpu.make_async_copy(dest_ref.at[0, 0], idx_smem, sem_idx)
    cp.start()
    cp.wait()

    def issue(g, carry):
        for k in range(SUBLANES):
            r = g * SUBLANES + k
            src = h_ref.at[g, pl.ds(k, 1)]
            _row_copy(src, xout_any.at[pl.ds(idx_smem[2 * r], 1)], sem_row).start(priority=0)
            _row_copy(src, xout_any.at[pl.ds(idx_smem[2 * r + 1], 1)], sem_row).start(priority=1)
        return carry

    lax.fori_loop(0, n_tiles, issue, 0)

    def drain(g, carry):
        for _ in range(2 * SUBLANES):
            _row_copy(h_ref.at[0, pl.ds(0, 1)], xout_any.at[pl.ds(0, 1)], sem_row).wait()
        return carry

    lax.fori_loop(0, n_tiles, drain, 0)


def _dispatch(h2, dest, xin):
    t, d = h2.shape
    td = min(DISPATCH_TILE, t)
    dest3 = dest.reshape(t // td, 1, 2 * td)
    h2 = h2.reshape(t // SUBLANES, SUBLANES, d)
    return pl.pallas_call(
        _dispatch_kernel,
        grid=(t // td,),
        in_specs=[pl.BlockSpec((td // SUBLANES, SUBLANES, d), lambda i: (i, 0, 0)),
                  pl.BlockSpec((1, 1, 2 * td), lambda i: (i, 0, 0)),
                  pl.BlockSpec(memory_space=pl.ANY)],
        out_specs=pl.BlockSpec(memory_space=pl.ANY),
        out_shape=jax.ShapeDtypeStruct(xin.shape, xin.dtype),
        scratch_shapes=[pltpu.SMEM((2 * td,), I32), pltpu.SemaphoreType.DMA(()), pltpu.SemaphoreType.DMA(())],
        input_output_aliases={2: 0},
        compiler_params=_cparams(("arbitrary",)),
    )(h2, dest3, xin)


SC_CORES = 2
SC_SUBCORES = 16
SC_CHUNK = 128
ROW_TILE = 1024


def _sc_gather_rows(table, idx):
    m = idx.shape[0]
    d = table.shape[1]
    workers = SC_CORES * SC_SUBCORES
    n_chunks = m // (workers * SC_CHUNK)
    assert n_chunks * workers * SC_CHUNK == m, (m, workers, SC_CHUNK)
    mesh = plsc.VectorSubcoreMesh(core_axis_name="c", subcore_axis_name="s",
                                  num_cores=SC_CORES, num_subcores=SC_SUBCORES)

    @functools.partial(
        pl.kernel, mesh=mesh,
        out_type=jax.ShapeDtypeStruct((m, d), table.dtype),
        scratch_types=[pltpu.VMEM((SC_CHUNK,), I32), pltpu.VMEM((SC_CHUNK, d), table.dtype),
                       pltpu.SemaphoreType.DMA],
    )
    def gather(table_hbm, idx_hbm, out_hbm, idx_v, rows_v, sem):
        wid = lax.axis_index("s") * SC_CORES + lax.axis_index("c")

        @pl.loop(0, n_chunks)
        def _(j):
            base = (wid * n_chunks + j) * SC_CHUNK
            pltpu.sync_copy(idx_hbm.at[pl.ds(base, SC_CHUNK)], idx_v)
            pltpu.async_copy(table_hbm.at[idx_v], rows_v, sem).wait()
            pltpu.sync_copy(rows_v, out_hbm.at[pl.ds(base, SC_CHUNK)])

    return gather(table, idx)


def _sc_dispatch_rows(rows, dest0, dest1, pad_rows, n_rows):
    n_tok, d = rows.shape
    workers = SC_CORES * SC_SUBCORES
    n_tc = n_tok // SC_CHUNK
    n_pc = pad_rows.shape[0] // (SC_CHUNK * workers)
    assert n_tc * SC_CHUNK == n_tok and n_pc * SC_CHUNK * workers == pad_rows.shape[0]
    steps = -(-n_tc // workers)
    zeros = jnp.zeros((SC_CHUNK, d), rows.dtype)
    mesh = plsc.VectorSubcoreMesh(core_axis_name="c", subcore_axis_name="s",
                                  num_cores=SC_CORES, num_subcores=SC_SUBCORES)

    @functools.partial(
        pl.kernel, mesh=mesh,
        out_type=jax.ShapeDtypeStruct((n_rows, d), rows.dtype),
        scratch_types=[pltpu.VMEM((SC_CHUNK,), I32), pltpu.VMEM((SC_CHUNK,), I32),
                       pltpu.VMEM((SC_CHUNK, d), rows.dtype)],
    )
    def scatter(rows_hbm, d0_hbm, d1_hbm, pad_hbm, zero_hbm, out_hbm, idx0_v, idx1_v, rows_v):
        wid = lax.axis_index("s") * SC_CORES + lax.axis_index("c")

        @pl.loop(0, steps)
        def _(j):
            c = j * workers + wid

            @pl.when(c < n_tc)
            def _():
                base = c * SC_CHUNK
                pltpu.sync_copy(d0_hbm.at[pl.ds(base, SC_CHUNK)], idx0_v)
                pltpu.sync_copy(d1_hbm.at[pl.ds(base, SC_CHUNK)], idx1_v)
                pltpu.sync_copy(rows_hbm.at[pl.ds(base, SC_CHUNK)], rows_v)
                pltpu.sync_copy(rows_v, out_hbm.at[idx0_v])
                pltpu.sync_copy(rows_v, out_hbm.at[idx1_v])

        pltpu.sync_copy(zero_hbm, rows_v)

        @pl.loop(0, n_pc)
        def _(j):
            base = (wid * n_pc + j) * SC_CHUNK
            pltpu.sync_copy(pad_hbm.at[pl.ds(base, SC_CHUNK)], idx0_v)
            pltpu.sync_copy(rows_v, out_hbm.at[idx0_v])

    return scatter(rows, dest0, dest1, pad_rows, zeros)


def _expert_kernel(be_ref, nu_ref, x_ref, wg_ref, wu_ref, wd_ref, o_ref, wgb_ref, wub_ref, wdb_ref):
    i = pl.program_id(0)

    @pl.when((i == 0) | (be_ref[i] != be_ref[jnp.maximum(i - 1, 0)]))
    def _():
        wgb_ref[...] = wg_ref[0, 0].astype(BF16)
        wub_ref[...] = wu_ref[0, 0].astype(BF16)
        wdb_ref[...] = wd_ref[0, 0].astype(BF16)

    @pl.when(i < nu_ref[0])
    def _():
        xb = _unpack_rows(x_ref[...]).astype(BF16)
        gte = _dot(xb, wgb_ref[...])
        up = _dot(xb, wub_ref[...])
        o_ref[...] = _pack_rows(_dot((_silu(gte) * up).astype(BF16), wdb_ref[...]))

    @pl.when(i >= nu_ref[0])
    def _():
        o_ref[...] = jnp.zeros_like(o_ref)


def _experts(xin, block_e, n_used, w_eg, w_eu, w_ed, layer):
    n_rows, dh = xin.shape
    d = 2 * dh
    nb = n_rows // MOE_BLOCK
    grid_spec = pltpu.PrefetchScalarGridSpec(
        num_scalar_prefetch=2,
        grid=(nb,),
        in_specs=[pl.BlockSpec((MOE_BLOCK, dh), lambda i, be, nu: (i, 0)),
                  pl.BlockSpec((1, 1, d, D_EXPERT), lambda i, be, nu: (layer, be[i], 0, 0)),
                  pl.BlockSpec((1, 1, d, D_EXPERT), lambda i, be, nu: (layer, be[i], 0, 0)),
                  pl.BlockSpec((1, 1, D_EXPERT, d), lambda i, be, nu: (layer, be[i], 0, 0))],
        out_specs=pl.BlockSpec((MOE_BLOCK, dh), lambda i, be, nu: (i, 0)),
        scratch_shapes=[pltpu.VMEM((d, D_EXPERT), BF16), pltpu.VMEM((d, D_EXPERT), BF16),
                        pltpu.VMEM((D_EXPERT, d), BF16)],
    )
    return pl.pallas_call(
        _expert_kernel,
        grid_spec=grid_spec,
        out_shape=jax.ShapeDtypeStruct((n_rows, dh), I32),
        compiler_params=_cparams(("arbitrary",)),
    )(block_e, n_used, xin, w_eg, w_eu, w_ed)


def _combine_kernel(y0_ref, y1_ref, rt_ref, x_ref, g2_ref, lng_ref, lnb_ref, o_ref, *, alpha):
    rt = rt_ref[0]
    y = rt[:, 2:3] * _unpack_rows(y0_ref[...]) + rt[:, 3:4] * _unpack_rows(y1_ref[...])
    o_ref[0] = _layer_norm(alpha * x_ref[0] + g2_ref[0] * y, lng_ref[...], lnb_ref[...])


def _combine(planes, tok0, n_tok, route, x, g2, lp, alpha):
    b, l, d = x.shape
    tc = min(ROW_TILE, l)
    npb = l // tc
    blk0 = tok0 // tc
    blk1 = (n_tok + tok0) // tc
    assert blk0 * tc == tok0 and blk1 * tc == n_tok + tok0
    par = lambda a: pl.BlockSpec(a.shape, lambda i, j: (0,) * a.ndim)
    return pl.pallas_call(
        functools.partial(_combine_kernel, alpha=alpha),
        grid=(b, npb),
        in_specs=[pl.BlockSpec((tc, d // 2), lambda i, j: (blk0 + i * npb + j, 0)),
                  pl.BlockSpec((tc, d // 2), lambda i, j: (blk1 + i * npb + j, 0)),
                  pl.BlockSpec((1, tc, LANES), lambda i, j: (i, j, 0)),
                  pl.BlockSpec((1, tc, d), lambda i, j: (i, j, 0)),
                  pl.BlockSpec((1, 1, d), lambda i, j: (i, 0, 0)),
                  par(lp["ln2_g"]), par(lp["ln2_b"])],
        out_specs=pl.BlockSpec((1, tc, d), lambda i, j: (i, j, 0)),
        out_shape=jax.ShapeDtypeStruct((b, l, d), F32),
        compiler_params=_cparams(("arbitrary", "arbitrary")),
    )(planes, planes, route, x, g2, lp["ln2_g"], lp["ln2_b"])


def _moe_plan(routes, counts):
    cnt = counts[0, ROUTE_LANE0:ROUTE_LANE0 + MOE_EXPERTS].astype(I32)
    padded = (cnt + MOE_BLOCK - 1) // MOE_BLOCK * MOE_BLOCK
    pad_end = jnp.cumsum(padded)
    pad_start = pad_end - padded
    n_assign = sum(r.shape[0] * r.shape[1] for r in routes) * 2
    n_blocks = -(-n_assign // MOE_BLOCK) + MOE_EXPERTS
    first_row = jnp.arange(n_blocks, dtype=I32)[:, None] * MOE_BLOCK
    block_e = jnp.minimum(jnp.sum((pad_end[None, :] <= first_row).astype(I32), axis=1), MOE_EXPERTS - 1)
    n_used = (pad_end[-1:] // MOE_BLOCK).astype(I32)
    eidx = jnp.arange(MOE_EXPERTS, dtype=I32)
    dests = []
    for r in routes:
        e = r[..., 0:2].astype(I32)
        rank = r[..., 4:6].astype(I32)
        start = jnp.sum(jnp.where(e[..., None] == eidx, pad_start, 0), axis=-1)
        dests.append((start + rank).reshape(-1, 2))
    n_rows = n_blocks * MOE_BLOCK
    gap = jnp.concatenate([padded - cnt, n_rows - pad_end[-1:]])
    gap_first = jnp.concatenate([pad_start + cnt, pad_end[-1:]])
    gap_cum = jnp.cumsum(gap)
    j = jnp.arange(n_rows - n_assign, dtype=I32)
    seg = jnp.sum((gap_cum[None, :] <= j[:, None]).astype(I32), axis=1)
    offset = jnp.sum(jnp.where(seg[:, None] == jnp.arange(MOE_EXPERTS + 1, dtype=I32),
                               gap_first - (gap_cum - gap), 0), axis=1)
    return dests, block_e, n_used, n_rows, (offset + j).astype(I32)


def _layer_params(i, p):
    w_in = p["w_in"][i]
    s0, s1, s2, s3 = D_INNER, D_INNER + D_XBC, D_INNER + D_XBC + 2 * HEADS, D_INNER + D_XBC + 2 * HEADS + D_MODEL
    pad = LANES - 2 * HEADS
    wdt = jnp.pad(w_in[:, s1:s2], ((0, 0), (0, pad)))
    w_r = jnp.concatenate([p["w_router_group"][i],
                           jnp.transpose(p["w_router_expert"][i], (1, 0, 2)).reshape(D_MODEL, MOE_EXPERTS)], axis=1)
    w_r = jnp.pad(w_r, ((0, 0), (0, LANES - w_r.shape[1])))
    w_r_hi = w_r.astype(BF16)
    b_r = jnp.concatenate([p["b_router_group"][i], p["b_router_expert"][i].reshape(-1)])
    return {
        "w_in": tuple(w.astype(BF16) for w in (w_in[:, :s0], w_in[:, s0:s1], w_in[:, s2:s3], w_in[:, s3:], wdt)),
        "conv_w": p["conv_w"][i], "conv_b": p["conv_b"][i][None],
        "dt_bias": jnp.pad(p["dt_bias"][i].reshape(1, -1), ((0, 0), (0, pad))),
        "a_log": jnp.pad(p["a_log"][i].reshape(1, -1), ((0, 0), (0, pad))),
        "d_skip_x": jnp.repeat(p["d_skip"][i], HEADDIM)[None],
        "ssd_norm_g": p["ssd_norm_g"][i][None],
        "pool_w": p["pool_w"][i].astype(BF16), "pool_scale": p["pool_scale"][i][None],
        "b_gate": p["b_gate"][i][None],
        "w_branch_a": p["w_branch_a"][i].astype(BF16), "w_branch_b": p["w_branch_b"][i].astype(BF16),
        "w_out": p["w_out"][i].astype(BF16),
        "ln1_g": p["ln1_g"][i][None], "ln1_b": p["ln1_b"][i][None],
        "ln2_g": p["ln2_g"][i][None], "ln2_b": p["ln2_b"][i][None],
        "w_r_hi": w_r_hi, "w_r_lo": (w_r - w_r_hi.astype(F32)).astype(BF16),
        "b_r": jnp.pad(b_r, (0, LANES - b_r.shape[0]))[None],
    }


def _head_expanders():
    col_head = jnp.arange(D_INNER, dtype=I32) // HEADDIM
    row = jnp.arange(LANES, dtype=I32)[:, None] % (2 * HEADS)
    return (row == col_head).astype(BF16), (row == col_head + HEADS).astype(BF16)


def kernel(x, c, ctx, c_ctx, w_ada, b_ada, w_in, b_gate, conv_w, conv_b, dt_bias, a_log, d_skip, ssd_norm_g, pool_w, pool_scale, w_branch_a, w_branch_b, w_out, ln1_g, ln1_b, ln2_g, ln2_b, w_router_group, b_router_group, w_router_expert, b_router_expert, w_expert_gate, w_expert_up, w_expert_down):
    p = dict(w_in=w_in, b_gate=b_gate, conv_w=conv_w, conv_b=conv_b, dt_bias=dt_bias, a_log=a_log, d_skip=d_skip,
             ssd_norm_g=ssd_norm_g, pool_w=pool_w, pool_scale=pool_scale, w_branch_a=w_branch_a,
             w_branch_b=w_branch_b, w_out=w_out, ln1_g=ln1_g, ln1_b=ln1_b, ln2_g=ln2_g, ln2_b=ln2_b,
             w_router_group=w_router_group, b_router_group=b_router_group, w_router_expert=w_router_expert,
             b_router_expert=b_router_expert, w_expert_gate=w_expert_gate, w_expert_up=w_expert_up,
             w_expert_down=w_expert_down)
    b, l, d = x.shape
    depth = w_ada.shape[0]
    alpha = (2.0 * depth) ** DEPTH_ALPHA_POW
    n_mod = -(-(b + 1) // 8) * 8
    cvec = jnp.zeros((n_mod, d), F32).at[:b].set(c).at[b].set(c_ctx)
    mods = _ada_mods(cvec, w_ada, b_ada)
    r_f, r_b = _head_expanders()
    zero_state = jnp.zeros((b, STATE, D_INNER), F32)
    zero_cnt = jnp.zeros((1, LANES), F32)
    xl, xc = x, ctx
    for i in range(depth):
        last = i == depth - 1
        lp = _layer_params(i, p)
        mod_l = [mods[i, :b, k * d:(k + 1) * d][:, None, :] for k in range(6)]
        mod_c = [jnp.broadcast_to(mods[i, b, k * d:(k + 1) * d], (b, 1, d)) for k in range(6)]

        zc, act_c, uc, gc, dtc = _in_proj(xc, mod_c[0], mod_c[1], lp["w_in"], lp["conv_w"], lp["conv_b"])
        if last:
            (s_f,) = _ssd_state_scan(act_c, dtc, zero_state, lp, r_f, rev=False, emit=False)
            (s_b,) = _ssd_state_scan(act_c, dtc, zero_state, lp, r_b, rev=True, emit=False)
        else:
            stb_c, s_b = _ssd_state_scan(act_c, dtc, zero_state, lp, r_b, rev=True, emit=True)
            ya_c, s_f = _ssd_main(act_c, dtc, zc, stb_c, zero_state, lp, r_f, r_b)
            yp_c = _pool_branch(uc, lp, False)

        zl, act_l, ul, gl, dtl = _in_proj(xl, mod_l[0], mod_l[1], lp["w_in"], lp["conv_w"], lp["conv_b"])
        stb_l, _ = _ssd_state_scan(act_l, dtl, s_b, lp, r_b, rev=True, emit=True)
        ya_l, _ = _ssd_main(act_l, dtl, zl, stb_l, s_f, lp, r_f, r_b)
        yp_l = _pool_branch(ul, lp, True)
        xl, h2_l, rt_l, cnt = _merge(ya_l, yp_l, gl, xl, mod_l[2], mod_l[3], mod_l[4], lp, zero_cnt, alpha)
        streams = [(h2_l, rt_l)]
        if not last:
            xc, h2_c, rt_c, cnt = _merge(ya_c, yp_c, gc, xc, mod_c[2], mod_c[3], mod_c[4], lp, cnt, alpha)
            streams.append((h2_c, rt_c))

        dests, block_e, n_used, n_rows, pad_rows = _moe_plan([rt for _, rt in streams], cnt)
        dest_t = jnp.concatenate(dests, axis=0).T
        n_tok = dest_t.shape[1]
        h2_all = jnp.concatenate([h2.reshape(-1, d // 2) for h2, _ in streams], axis=0)
        xin = _sc_dispatch_rows(h2_all, dest_t[0], dest_t[1], pad_rows, n_rows)
        yb = _experts(xin, block_e, n_used, w_expert_gate, w_expert_up, w_expert_down, i)
        planes = _sc_gather_rows(yb, dest_t.reshape(-1))
        xl = _combine(planes, 0, n_tok, rt_l, xl, mod_l[5], lp, alpha)
        if not last:
            xc = _combine(planes, b * l, n_tok, rt_c, xc, mod_c[5], lp, alpha)
    return xl
```

```python
import functools
import math

import jax
import jax.numpy as jnp
from jax import lax
from jax.experimental import pallas as pl
from jax.experimental.pallas import tpu as pltpu
from jax.experimental.pallas import tpu_sc as plsc

F32 = jnp.float32
BF16 = jnp.bfloat16
I32 = jnp.int32

D_MODEL = 1024
D_INNER = 2048
HEADS = 32
HEADDIM = 64
GROUPS = 4
STATE = 128
D_BC = GROUPS * STATE
D_XBC = D_INNER + 2 * D_BC
CONV_K = 5
CHUNK = 128
GRID_W = 64
POOL_WINDOWS = (2, 4, 8, 16)
POOL_GROUP_DIM = 256
MOE_GROUPS = 4
MOE_EPG = 8
MOE_EXPERTS = 32
D_EXPERT = 512
MOE_BLOCK = 512
DEPTH_ALPHA_POW = 0.25
NORM_EPS = 1e-5
LANES = 128
ROUTE_LANE0 = MOE_GROUPS
VMEM_LIMIT = 56 * 1024 * 1024


def _cparams(sem, vmem=VMEM_LIMIT):
    return pltpu.CompilerParams(dimension_semantics=sem, vmem_limit_bytes=vmem)


def _sigmoid(v):
    return 0.5 + 0.5 * jnp.tanh(0.5 * v)


def _silu(v):
    h = 0.5 * v
    return h + h * jnp.tanh(h)


def _layer_norm(v, g, b):
    mu = jnp.mean(v, axis=-1, keepdims=True)
    d = v - mu
    var = jnp.mean(d * d, axis=-1, keepdims=True)
    return d * lax.rsqrt(var + NORM_EPS) * g + b


def _split_bf16(v):
    hi = v.astype(BF16)
    lo = (v - hi.astype(F32)).astype(BF16)
    return hi, lo


def _dot(a, b):
    return jnp.dot(a, b, preferred_element_type=F32)


U32 = jnp.uint32
HI16 = 0xFFFF0000


def _pack_rows(v):
    half = v.shape[1] // 2
    lo = lax.bitcast_convert_type(v[:, :half].astype(BF16).astype(F32), U32)
    hi = lax.bitcast_convert_type(v[:, half:].astype(BF16).astype(F32), U32)
    return lax.bitcast_convert_type((lo >> 16) | (hi & U32(HI16)), I32)


def _unpack_rows(w):
    w = lax.bitcast_convert_type(w, U32)
    lo = lax.bitcast_convert_type(w << 16, F32)
    hi = lax.bitcast_convert_type(w & U32(HI16), F32)
    return jnp.concatenate([lo, hi], axis=1)


def _ada_kernel(c_ref, w_ref, b_ref, o_ref):
    s = _silu(c_ref[...]).astype(BF16)
    o_ref[0] = _dot(s, w_ref[0].astype(BF16)) + b_ref[0]


def _ada_mods(cvec, w_ada, b_ada):
    depth, d, n = w_ada.shape
    r = cvec.shape[0]
    tn = D_MODEL
    return pl.pallas_call(
        _ada_kernel,
        grid=(depth, n // tn),
        in_specs=[pl.BlockSpec((r, d), lambda l, j: (0, 0)),
                  pl.BlockSpec((1, d, tn), lambda l, j: (l, 0, j)),
                  pl.BlockSpec((1, 1, tn), lambda l, j: (l, 0, j))],
        out_specs=pl.BlockSpec((1, r, tn), lambda l, j: (l, 0, j)),
        out_shape=jax.ShapeDtypeStruct((depth, r, n), F32),
        compiler_params=_cparams(("arbitrary", "arbitrary")),
    )(cvec, w_ada, b_ada.reshape(depth, 1, n))


CONV_HALO = 8
CONV_SLAB = 512


def _inproj_kernel(xp_ref, x_ref, xn_ref, sh_ref, sc_ref, wz_ref, wx_ref, wu_ref, wg_ref, wdt_ref, cw_ref, cb_ref,
                   z_ref, act_ref, u_ref, g_ref, dt_ref):
    j = pl.program_id(1)
    nj = pl.num_programs(1)
    tm = x_ref.shape[1]
    n = tm + 2 * CONV_HALO
    mod = lambda v: (v * (1.0 + sc_ref[0]) + sh_ref[0]).astype(BF16)
    h = mod(x_ref[0])
    h_ext = mod(jnp.concatenate([xp_ref[0], x_ref[0], xn_ref[0]], axis=0))

    row = lax.broadcasted_iota(I32, (n, 1), 0)
    inside = ((row >= CONV_HALO) | (j > 0)) & ((row < CONV_HALO + tm) | (j < nj - 1))

    z_ref[0] = _dot(h, wz_ref[...]).astype(BF16)
    u_ref[0] = _dot(h, wu_ref[...]).astype(BF16)
    g_ref[0] = _dot(h, wg_ref[...]).astype(BF16)
    dt_ref[0] = _dot(h, wdt_ref[...])
    down = lambda v: pltpu.roll(v, 1, 0)
    up = lambda v: pltpu.roll(v, n - 1, 0)
    for s in range(D_XBC // CONV_SLAB):
        cols = slice(s * CONV_SLAB, (s + 1) * CONV_SLAB)
        e = jnp.where(inside, _dot(h_ext, wx_ref[:, cols]), 0.0)
        y = [e * cw_ref[k:k + 1, cols] for k in range(CONV_K)]
        acc = y[2] + down(y[1] + down(y[0])) + up(y[3] + up(y[4]))
        act_ref[0, :, cols] = _silu(acc[CONV_HALO:CONV_HALO + tm] + cb_ref[:, cols]).astype(BF16)


def _in_proj(x, shift, scale, wts, conv_w, conv_b):
    b, l, d = x.shape
    tm = min(512, l)
    hpt = tm // CONV_HALO
    nh = l // CONV_HALO
    wz, wx, wu, wg, wdt = wts
    tok = lambda n: pl.BlockSpec((1, tm, n), lambda i, j: (i, j, 0))
    mod = pl.BlockSpec((1, 1, d), lambda i, j: (i, 0, 0))
    wspec = lambda w: pl.BlockSpec(w.shape, lambda i, j: (0, 0), pipeline_mode=pl.Buffered(1))
    par = lambda a: pl.BlockSpec(a.shape, lambda i, j: (0, 0))
    halo_prev = pl.BlockSpec((1, CONV_HALO, d), lambda i, j: (i, jnp.maximum(j * hpt - 1, 0), 0))
    halo_next = pl.BlockSpec((1, CONV_HALO, d), lambda i, j: (i, jnp.minimum((j + 1) * hpt, nh - 1), 0))
    return pl.pallas_call(
        _inproj_kernel,
        grid=(b, l // tm),
        in_specs=[halo_prev, tok(d), halo_next, mod, mod, wspec(wz), wspec(wx), wspec(wu), wspec(wg), wspec(wdt),
                  par(conv_w), par(conv_b)],
        out_specs=[tok(D_INNER), tok(D_XBC), tok(D_MODEL), tok(2 * D_MODEL), tok(LANES)],
        out_shape=[jax.ShapeDtypeStruct((b, l, D_INNER), BF16),
                   jax.ShapeDtypeStruct((b, l, D_XBC), BF16),
                   jax.ShapeDtypeStruct((b, l, D_MODEL), BF16),
                   jax.ShapeDtypeStruct((b, l, 2 * D_MODEL), BF16),
                   jax.ShapeDtypeStruct((b, l, LANES), F32)],
        compiler_params=_cparams(("arbitrary", "arbitrary")),
    )(x, x, x, shift, scale, wz, wx, wu, wg, wdt, conv_w, conv_b)


def _dt_prep(dt_raw, dtb_ref, alog_ref):
    q = CHUNK
    lane = lax.broadcasted_iota(I32, (q, LANES), 1)
    raw = dt_raw + dtb_ref[...]
    dt = jnp.maximum(raw, 0.0) + jnp.log1p(jnp.exp(-jnp.abs(raw)))
    dt = jnp.where(lane < 2 * HEADS, dt, 0.0)
    a = dt * (-jnp.exp(alog_ref[...]))
    ii = lax.broadcasted_iota(I32, (q, q), 0)
    jj = lax.broadcasted_iota(I32, (q, q), 1)
    tri = (jj <= ii).astype(BF16)
    a_hi, a_lo = _split_bf16(a)
    acs = _dot(tri, a_hi) + _dot(tri, a_lo)
    tot = acs[q - 1:q, :]
    return dt, a, acs, tot


def _expand(v, lane0, r_ref):
    return _dot(_expand_src(v, lane0), r_ref[...])


def _expand_src(v, lane0):
    lane = lax.broadcasted_iota(I32, v.shape, 1)
    vm = jnp.where((lane >= lane0) & (lane < lane0 + HEADS), v, 0.0)
    hi = vm.astype(BF16).astype(F32)
    return (hi + pltpu.roll(vm - hi, 2 * HEADS, 1)).astype(BF16)


def _state_update(st_ref, bm, xw_bf, decay_x):
    gw = D_INNER // GROUPS
    for g in range(GROUPS):
        bgt = bm[:, g * STATE:(g + 1) * STATE].T.astype(BF16)
        cols = slice(g * gw, (g + 1) * gw)
        st_ref[:, cols] = st_ref[:, cols] * decay_x[:, cols] + _dot(bgt, xw_bf[:, cols])


SCAN_CHUNKS = 8


def _ssd_state_kernel(*refs, rev, emit, nch):
    act_ref, dt_ref, h0_ref, dtb_ref, alog_ref, r_ref = refs[:6]
    if emit:
        stout_ref, sfin_ref, st_ref = refs[6:]
    else:
        sfin_ref, st_ref = refs[6:]
    s = pl.program_id(1)
    ns = pl.num_programs(1)

    @pl.when(s == 0)
    def _():
        st_ref[...] = h0_ref[0]

    for k in range(nch):
        cc = (nch - 1 - k) if rev else k
        rows = slice(cc * CHUNK, (cc + 1) * CHUNK)
        if emit:
            stout_ref[0, cc] = st_ref[...].astype(BF16)
        xs = act_ref[0, rows, :D_INNER].astype(F32)
        bm = act_ref[0, rows, D_INNER:D_INNER + D_BC].astype(F32)
        dt, a, acs, tot = _dt_prep(dt_ref[0, rows, :], dtb_ref, alog_ref)
        if rev:
            w = jnp.exp(acs - a) * dt
            lane0 = HEADS
        else:
            w = jnp.exp(tot - acs) * dt
            lane0 = 0
        w_x = _expand(w, lane0, r_ref)
        dec_x = _expand(jnp.broadcast_to(jnp.exp(tot), (16, LANES)), lane0, r_ref)[0:1]
        _state_update(st_ref, bm, (w_x * xs).astype(BF16), dec_x)

    @pl.when(s == ns - 1)
    def _():
        sfin_ref[0] = st_ref[...]


def _ssd_state_scan(act, dt, h0, lp, r_mat, *, rev, emit):
    b, l, _ = act.shape
    nc = l // CHUNK
    nch = min(SCAN_CHUNKS, nc)
    ns = nc // nch
    seq = (lambda s: ns - 1 - s) if rev else (lambda s: s)
    par = lambda shape: pl.BlockSpec(shape, lambda i, s: (0,) * len(shape))
    in_specs = [
        pl.BlockSpec((1, nch * CHUNK, D_XBC), lambda i, s: (i, seq(s), 0)),
        pl.BlockSpec((1, nch * CHUNK, LANES), lambda i, s: (i, seq(s), 0)),
        pl.BlockSpec((1, STATE, D_INNER), lambda i, s: (i, 0, 0)),
        par((1, LANES)), par((1, LANES)), par((LANES, D_INNER)),
    ]
    out_specs = [pl.BlockSpec((1, STATE, D_INNER), lambda i, s: (i, 0, 0))]
    out_shape = [jax.ShapeDtypeStruct((b, STATE, D_INNER), F32)]
    if emit:
        out_specs = [pl.BlockSpec((1, nch, STATE, D_INNER), lambda i, s: (i, seq(s), 0, 0))] + out_specs
        out_shape = [jax.ShapeDtypeStruct((b, nc, STATE, D_INNER), BF16)] + out_shape
    return pl.pallas_call(
        functools.partial(_ssd_state_kernel, rev=rev, emit=emit, nch=nch),
        grid=(b, ns),
        in_specs=in_specs, out_specs=out_specs, out_shape=out_shape,
        scratch_shapes=[pltpu.VMEM((STATE, D_INNER), F32)],
        compiler_params=_cparams(("arbitrary", "arbitrary")),
    )(act, dt, h0, lp["dt_bias"], lp["a_log"], r_mat)


MAIN_CHUNKS = 4


def _ssd_main_kernel(act_ref, dt_ref, z_ref, stb_ref, h0_ref,
                     dtb_ref, alog_ref, dsk_ref, ng_ref, rf_ref, rb_ref,
                     y_ref, sfin_ref, st_ref, ys_ref, *, nch):
    c = pl.program_id(1)
    nc = pl.num_programs(1)

    @pl.when(c == 0)
    def _():
        st_ref[...] = h0_ref[0]

    for k in range(nch):
        rows = pl.ds(k * CHUNK, CHUNK)
        _ssd_main_chunk(act_ref.at[0, rows], dt_ref.at[0, rows], z_ref.at[0, rows], stb_ref.at[0, k],
                        dtb_ref, alog_ref, dsk_ref, ng_ref, rf_ref, rb_ref, y_ref.at[0, rows], st_ref, ys_ref)

    @pl.when(c == nc - 1)
    def _():
        sfin_ref[0] = st_ref[...]


def _ssd_main_chunk(act_ref, dt_ref, z_ref, stb_ref, dtb_ref, alog_ref, dsk_ref, ng_ref, rf_ref, rb_ref,
                    y_ref, st_ref, ys_ref):
    q = CHUNK
    dt, a, acs, tot = _dt_prep(dt_ref[...], dtb_ref, alog_ref)
    eb = acs - a

    ef_s = _expand_src(jnp.exp(acs), 0)
    eb_s = _expand_src(jnp.exp(tot - eb), HEADS)
    tf_s = _expand_src(jnp.exp(tot - acs) * dt, 0)
    dec_s = _expand_src(jnp.broadcast_to(jnp.exp(tot), (16, LANES)), 0)

    lane = lax.broadcasted_iota(I32, (q, LANES), 1)
    ldt = jnp.log(dt)
    qm = jnp.where(lane < HEADS, acs - ldt, jnp.where(lane < 2 * HEADS, eb + ldt, pltpu.roll(dt, 2 * HEADS, 1)))
    qt = qm.T
    ii = lax.broadcasted_iota(I32, (q, q), 0)
    jj = lax.broadcasted_iota(I32, (q, q), 1)
    lower = jj <= ii
    diag = jj == ii
    lane2 = lax.broadcasted_iota(I32, (q, LANES), 1)
    left = lane2 < HEADDIM

    gw = D_INNER // GROUPS
    hpg = HEADS // GROUPS
    for g in range(GROUPS):
        bg = act_ref[:, D_INNER + g * STATE:D_INNER + (g + 1) * STATE]
        cg = act_ref[:, D_INNER + D_BC + g * STATE:D_INNER + D_BC + (g + 1) * STATE]
        cb = lax.dot_general(cg, bg, (((1,), (1,)), ((), ())), preferred_element_type=F32)
        cols = slice(g * gw, (g + 1) * gw)
        y_off = (_dot(ef_s, rf_ref[:, cols]) * _dot(cg, st_ref[:, cols].astype(BF16))
                 + _dot(eb_s, rb_ref[:, cols]) * _dot(cg, stb_ref[:, cols]))
        for pr in range(hpg // 2):
            ms = []
            for hh in range(2):
                h = g * hpg + pr * 2 + hh
                afc = acs[:, h:h + 1]
                afr = qt[h:h + 1, :]
                ebc = eb[:, HEADS + h:HEADS + h + 1]
                ebr = qt[HEADS + h:HEADS + h + 1, :]
                wb = qt[3 * HEADS + h:3 * HEADS + h + 1, :]
                arg = jnp.where(lower, afc - afr, ebr - ebc)
                ms.append((cb * (jnp.exp(arg) + jnp.where(diag, wb, 0.0))).astype(BF16))
            lhs = jnp.concatenate(ms, axis=1)
            c0 = g * gw + pr * 2 * HEADDIM
            xp = act_ref[:, c0:c0 + 2 * HEADDIM]
            zero = jnp.zeros_like(xp)
            rhs = jnp.concatenate([jnp.where(left, xp, zero), jnp.where(left, zero, xp)], axis=0)
            y_pair = _dot(lhs, rhs) + y_off[:, pr * 2 * HEADDIM:(pr + 1) * 2 * HEADDIM]
            y_pair = y_pair + dsk_ref[:, c0:c0 + 2 * HEADDIM] * xp.astype(F32)
            ys_ref[:, c0:c0 + 2 * HEADDIM] = y_pair

        xw = (_dot(tf_s, rf_ref[:, cols]) * act_ref[:, cols].astype(F32)).astype(BF16)
        bgt = bg.astype(F32).T.astype(BF16)
        st_ref[:, cols] = st_ref[:, cols] * _dot(dec_s, rf_ref[:, cols])[0:1] + _dot(bgt, xw)

    y = ys_ref[...]
    yz = y * _silu(z_ref[...].astype(F32))
    ms2 = jnp.mean(yz * yz, axis=-1, keepdims=True)
    y_ref[...] = (yz * lax.rsqrt(ms2 + NORM_EPS) * ng_ref[...]).astype(BF16)


def _ssd_main(act, dt, z, stb, h0, lp, r_f, r_b):
    b, l, _ = act.shape
    nc = l // CHUNK
    nch = min(MAIN_CHUNKS, nc)
    rows = nch * CHUNK
    par = lambda shape: pl.BlockSpec(shape, lambda i, s: (0,) * len(shape))
    in_specs = [
        pl.BlockSpec((1, rows, D_XBC), lambda i, s: (i, s, 0)),
        pl.BlockSpec((1, rows, LANES), lambda i, s: (i, s, 0)),
        pl.BlockSpec((1, rows, D_INNER), lambda i, s: (i, s, 0)),
        pl.BlockSpec((1, nch, STATE, D_INNER), lambda i, s: (i, s, 0, 0)),
        pl.BlockSpec((1, STATE, D_INNER), lambda i, s: (i, 0, 0)),
        par((1, LANES)), par((1, LANES)),
        par((1, D_INNER)), par((1, D_INNER)), par((LANES, D_INNER)), par((LANES, D_INNER)),
    ]
    return pl.pallas_call(
        functools.partial(_ssd_main_kernel, nch=nch),
        grid=(b, nc // nch),
        in_specs=in_specs,
        out_specs=[pl.BlockSpec((1, rows, D_INNER), lambda i, s: (i, s, 0)),
                   pl.BlockSpec((1, STATE, D_INNER), lambda i, s: (i, 0, 0))],
        out_shape=[jax.ShapeDtypeStruct((b, l, D_INNER), BF16),
                   jax.ShapeDtypeStruct((b, STATE, D_INNER), F32)],
        scratch_shapes=[pltpu.VMEM((STATE, D_INNER), F32), pltpu.VMEM((CHUNK, D_INNER), F32)],
        compiler_params=_cparams(("arbitrary", "arbitrary")),
    )(act, dt, z, stb, h0, lp["dt_bias"], lp["a_log"], lp["d_skip_x"], lp["ssd_norm_g"], r_f, r_b)


def _pool_kernel(*refs, width, vertical, halo, hblock):
    if vertical:
        prev_ref, cur_ref, next_ref, pw_ref, ps_ref, o_ref = refs
    else:
        cur_ref, pw_ref, ps_ref, o_ref = refs
    t = pl.program_id(1)
    nt = pl.num_programs(1)
    tp = cur_ref.shape[1]
    n_tok = nt * tp
    shift = int(math.log2(width))
    idx = lax.broadcasted_iota(I32, (tp, 1), 0) + t * tp
    colpos = idx & (width - 1)
    rowpos = idx >> shift
    n_rows = n_tok // width
    bi = lax.broadcasted_iota(I32, (hblock, hblock), 0)
    bj = lax.broadcasted_iota(I32, (hblock, hblock), 1)
    same_row = (bi >> shift) == (bj >> shift)
    cur = cur_ref[0].astype(F32)
    if vertical:
        prev = jnp.where(t > 0, prev_ref[0].astype(F32), 0.0)
        nxt = jnp.where(t < nt - 1, next_ref[0].astype(F32), 0.0)
    for gi, k in enumerate(POOL_WINDOWS):
        cols = slice(gi * POOL_GROUP_DIM, (gi + 1) * POOL_GROUP_DIM)
        ug = cur[:, cols]
        lo = k // 2
        if vertical:
            above, below = lo * width, (k - 1 - lo) * width
            parts = [prev[halo - above:, cols], ug] + ([nxt[:below, cols]] if below else [])
            e = jnp.concatenate(parts, axis=0)
            step = width
            m = 1
            while m < k:
                n = e.shape[0] - step
                e = e[0:n] + e[step:step + n]
                step *= 2
                m *= 2
            s = e
            cnt_r = (jnp.minimum(rowpos - lo + k, n_rows) - jnp.maximum(rowpos - lo, 0)).astype(F32)
        else:
            s = ug
            cnt_r = jnp.ones((tp, 1), F32)
        band = (same_row & (bj - bi >= -lo) & (bj - bi < k - lo)).astype(BF16)
        s_bf = s.astype(BF16)
        hs = [_dot(band, s_bf[r * hblock:(r + 1) * hblock]) for r in range(tp // hblock)]
        hsum = hs[0] if len(hs) == 1 else jnp.concatenate(hs, axis=0)
        cnt_c = (jnp.minimum(colpos - lo + k, width) - jnp.maximum(colpos - lo, 0)).astype(F32)
        mean = hsum / (cnt_r * cnt_c)
        y = _dot((mean - ug).astype(BF16), pw_ref[gi])
        o_ref[0, :, cols] = (y * ps_ref[:, cols]).astype(BF16)


def _pool_branch(u, lp, on_grid):
    b, l, d = u.shape
    pw, ps = lp["pool_w"], lp["pool_scale"]
    if on_grid:
        width, halo = GRID_W, (max(POOL_WINDOWS) // 2) * GRID_W
        tp = min(1024, l)
        hb = tp // halo
        nhb = l // halo
        in_specs = [pl.BlockSpec((1, halo, d), lambda i, t: (i, jnp.maximum(t * hb - 1, 0), 0)),
                    pl.BlockSpec((1, tp, d), lambda i, t: (i, t, 0)),
                    pl.BlockSpec((1, halo, d), lambda i, t: (i, jnp.minimum((t + 1) * hb, nhb - 1), 0))]
        args = (u, u, u)
        kern = functools.partial(_pool_kernel, width=width, vertical=True, halo=halo, hblock=LANES)
    else:
        tp = l
        in_specs = [pl.BlockSpec((1, tp, d), lambda i, t: (i, t, 0))]
        args = (u,)
        kern = functools.partial(_pool_kernel, width=l, vertical=False, halo=0, hblock=l)
    in_specs += [pl.BlockSpec(pw.shape, lambda i, t: (0, 0, 0)), pl.BlockSpec(ps.shape, lambda i, t: (0, 0))]
    return pl.pallas_call(
        kern,
        grid=(b, l // tp),
        in_specs=in_specs,
        out_specs=pl.BlockSpec((1, tp, d), lambda i, t: (i, t, 0)),
        out_shape=jax.ShapeDtypeStruct((b, l, d), BF16),
        compiler_params=_cparams(("arbitrary", "arbitrary")),
    )(*args, pw, ps)


def _merge_kernel(ya_ref, yp_ref, gt_ref, x_ref, g1_ref, sh2_ref, sc2_ref, bg_ref,
                  wa_ref, wb_ref, wo_ref, lng_ref, lnb_ref, wrh_ref, wrl_ref, br_ref, cin_ref,
                  xo_ref, h2_ref, rt_ref, cout_ref, cnt_ref, *, alpha):
    i = pl.program_id(0)
    j = pl.program_id(1)
    tm = x_ref.shape[1]

    @pl.when((i == 0) & (j == 0))
    def _():
        cnt_ref[...] = cin_ref[...]

    gate = _sigmoid(gt_ref[0].astype(F32) + bg_ref[...])
    merged = (gate[:, :D_MODEL] * _dot(ya_ref[0], wa_ref[...])
              + gate[:, D_MODEL:] * _dot(yp_ref[0], wb_ref[...]))
    out = _dot(merged.astype(BF16), wo_ref[...])
    xn = _layer_norm(alpha * x_ref[0] + g1_ref[0] * out, lng_ref[...], lnb_ref[...])
    xo_ref[0] = xn
    h2 = xn * (1.0 + sc2_ref[0]) + sh2_ref[0]
    h2_ref[0] = _pack_rows(h2)

    h_hi, h_lo = _split_bf16(h2)
    logits = _dot(h_hi, wrh_ref[...]) + _dot(h_lo, wrh_ref[...]) + _dot(h_hi, wrl_ref[...]) + br_ref[...]
    lane = lax.broadcasted_iota(I32, (tm, LANES), 1)
    neg = jnp.float32(-jnp.inf)
    lg = jnp.where(lane < MOE_GROUPS, logits, neg)
    mg = jnp.max(lg, axis=-1, keepdims=True)
    grp = jnp.min(jnp.where(lg == mg, lane, LANES), axis=-1, keepdims=True)
    p_grp = 1.0 / jnp.sum(jnp.exp(lg - mg), axis=-1, keepdims=True)
    lo_lane = ROUTE_LANE0 + grp * MOE_EPG
    le = jnp.where((lane >= lo_lane) & (lane < lo_lane + MOE_EPG), logits, neg)
    v1 = jnp.max(le, axis=-1, keepdims=True)
    i1 = jnp.min(jnp.where(le == v1, lane, LANES), axis=-1, keepdims=True)
    le2 = jnp.where(lane == i1, neg, le)
    v2 = jnp.max(le2, axis=-1, keepdims=True)
    i2 = jnp.min(jnp.where(le2 == v2, lane, LANES), axis=-1, keepdims=True)
    e2 = jnp.exp(v2 - v1)
    w1 = p_grp / (1.0 + e2)
    w2 = p_grp * e2 / (1.0 + e2)

    oh1 = (lane == i1).astype(F32)
    oh2 = (lane == i2).astype(F32)
    oh = oh1 + oh2
    ri = lax.broadcasted_iota(I32, (tm, tm), 0)
    rj = lax.broadcasted_iota(I32, (tm, tm), 1)
    before = _dot((rj < ri).astype(BF16), oh.astype(BF16)) + cnt_ref[...]
    rank1 = jnp.sum(oh1 * before, axis=-1, keepdims=True)
    rank2 = jnp.sum(oh2 * before, axis=-1, keepdims=True)
    cnt_ref[...] = cnt_ref[...] + jnp.sum(oh, axis=0, keepdims=True)
    cout_ref[...] = cnt_ref[...]

    slab = jnp.where(lane == 0, (i1 - ROUTE_LANE0).astype(F32),
           jnp.where(lane == 1, (i2 - ROUTE_LANE0).astype(F32),
           jnp.where(lane == 2, w1,
           jnp.where(lane == 3, w2,
           jnp.where(lane == 4, rank1,
           jnp.where(lane == 5, rank2, 0.0))))))
    rt_ref[0] = slab


def _merge(ya, yp, gates, x, g1, sh2, sc2, lp, cnt_in, alpha):
    b, l, d = x.shape
    tm = min(512, l)
    tok = lambda n: pl.BlockSpec((1, tm, n), lambda i, j: (i, j, 0))
    mod = pl.BlockSpec((1, 1, d), lambda i, j: (i, 0, 0))
    par = lambda a: pl.BlockSpec(a.shape, lambda i, j: (0,) * a.ndim)
    params = (lp["b_gate"], lp["w_branch_a"], lp["w_branch_b"], lp["w_out"], lp["ln1_g"], lp["ln1_b"],
              lp["w_r_hi"], lp["w_r_lo"], lp["b_r"], cnt_in)
    return pl.pallas_call(
        functools.partial(_merge_kernel, alpha=alpha),
        grid=(b, l // tm),
        in_specs=[tok(D_INNER), tok(d), tok(2 * d), tok(d), mod, mod, mod] + [par(a) for a in params],
        out_specs=[tok(d), tok(d // 2), tok(LANES), pl.BlockSpec((1, LANES), lambda i, j: (0, 0))],
        out_shape=[jax.ShapeDtypeStruct((b, l, d), F32), jax.ShapeDtypeStruct((b, l, d // 2), I32),
                   jax.ShapeDtypeStruct((b, l, LANES), F32), jax.ShapeDtypeStruct((1, LANES), F32)],
        scratch_shapes=[pltpu.VMEM((1, LANES), F32)],
        compiler_params=_cparams(("arbitrary", "arbitrary")),
    )(ya, yp, gates, x, g1, sh2, sc2, *params)


SC_CORES = 2
SC_SUBCORES = 16
SC_CHUNK = 128
ROW_TILE = 1024


def _sc_gather_rows(table, idx):
    m = idx.shape[0]
    d = table.shape[1]
    workers = SC_CORES * SC_SUBCORES
    n_chunks = m // (workers * SC_CHUNK)
    assert n_chunks * workers * SC_CHUNK == m, (m, workers, SC_CHUNK)
    mesh = plsc.VectorSubcoreMesh(core_axis_name="c", subcore_axis_name="s",
                                  num_cores=SC_CORES, num_subcores=SC_SUBCORES)

    @functools.partial(
        pl.kernel, mesh=mesh,
        out_type=jax.ShapeDtypeStruct((m, d), table.dtype),
        scratch_types=[pltpu.VMEM((SC_CHUNK,), I32), pltpu.VMEM((SC_CHUNK, d), table.dtype),
                       pltpu.SemaphoreType.DMA],
    )
    def gather(table_hbm, idx_hbm, out_hbm, idx_v, rows_v, sem):
        wid = lax.axis_index("s") * SC_CORES + lax.axis_index("c")

        @pl.loop(0, n_chunks)
        def _(j):
            base = (wid * n_chunks + j) * SC_CHUNK
            pltpu.sync_copy(idx_hbm.at[pl.ds(base, SC_CHUNK)], idx_v)
            pltpu.async_copy(table_hbm.at[idx_v], rows_v, sem).wait()
            pltpu.sync_copy(rows_v, out_hbm.at[pl.ds(base, SC_CHUNK)])

    return gather(table, idx)


def _sc_dispatch_rows(rows, dest0, dest1, pad_rows, n_rows):
    n_tok, d = rows.shape
    workers = SC_CORES * SC_SUBCORES
    n_tc = n_tok // SC_CHUNK
    n_pc = pad_rows.shape[0] // (SC_CHUNK * workers)
    assert n_tc * SC_CHUNK == n_tok and n_pc * SC_CHUNK * workers == pad_rows.shape[0]
    steps = -(-n_tc // workers)
    zeros = jnp.zeros((SC_CHUNK, d), rows.dtype)
    mesh = plsc.VectorSubcoreMesh(core_axis_name="c", subcore_axis_name="s",
                                  num_cores=SC_CORES, num_subcores=SC_SUBCORES)

    @functools.partial(
        pl.kernel, mesh=mesh,
        out_type=jax.ShapeDtypeStruct((n_rows, d), rows.dtype),
        scratch_types=[pltpu.VMEM((SC_CHUNK,), I32), pltpu.VMEM((SC_CHUNK,), I32),
                       pltpu.VMEM((SC_CHUNK, d), rows.dtype)],
    )
    def scatter(rows_hbm, d0_hbm, d1_hbm, pad_hbm, zero_hbm, out_hbm, idx0_v, idx1_v, rows_v):
        wid = lax.axis_index("s") * SC_CORES + lax.axis_index("c")

        @pl.loop(0, steps)
        def _(j):
            c = j * workers + wid

            @pl.when(c < n_tc)
            def _():
                base = c * SC_CHUNK
                pltpu.sync_copy(d0_hbm.at[pl.ds(base, SC_CHUNK)], idx0_v)
                pltpu.sync_copy(d1_hbm.at[pl.ds(base, SC_CHUNK)], idx1_v)
                pltpu.sync_copy(rows_hbm.at[pl.ds(base, SC_CHUNK)], rows_v)
                pltpu.sync_copy(rows_v, out_hbm.at[idx0_v])
                pltpu.sync_copy(rows_v, out_hbm.at[idx1_v])

        pltpu.sync_copy(zero_hbm, rows_v)

        @pl.loop(0, n_pc)
        def _(j):
            base = (wid * n_pc + j) * SC_CHUNK
            pltpu.sync_copy(pad_hbm.at[pl.ds(base, SC_CHUNK)], idx0_v)
            pltpu.sync_copy(rows_v, out_hbm.at[idx0_v])

    return scatter(rows, dest0, dest1, pad_rows, zeros)


def _expert_kernel(be_ref, nu_ref, x_ref, wg_ref, wu_ref, wd_ref, o_ref, wgb_ref, wub_ref, wdb_ref):
    i = pl.program_id(0)

    @pl.when((i == 0) | (be_ref[i] != be_ref[jnp.maximum(i - 1, 0)]))
    def _():
        wgb_ref[...] = wg_ref[0, 0].astype(BF16)
        wub_ref[...] = wu_ref[0, 0].astype(BF16)
        wdb_ref[...] = wd_ref[0, 0].astype(BF16)

    @pl.when(i < nu_ref[0])
    def _():
        xb = _unpack_rows(x_ref[...]).astype(BF16)
        gte = _dot(xb, wgb_ref[...])
        up = _dot(xb, wub_ref[...])
        o_ref[...] = _pack_rows(_dot((_silu(gte) * up).astype(BF16), wdb_ref[...]))

    @pl.when(i >= nu_ref[0])
    def _():
        o_ref[...] = jnp.zeros_like(o_ref)


def _experts(xin, block_e, n_used, w_eg, w_eu, w_ed, layer):
    n_rows, dh = xin.shape
    d = 2 * dh
    nb = n_rows // MOE_BLOCK
    grid_spec = pltpu.PrefetchScalarGridSpec(
        num_scalar_prefetch=2,
        grid=(nb,),
        in_specs=[pl.BlockSpec((MOE_BLOCK, dh), lambda i, be, nu: (i, 0)),
                  pl.BlockSpec((1, 1, d, D_EXPERT), lambda i, be, nu: (layer, be[i], 0, 0)),
                  pl.BlockSpec((1, 1, d, D_EXPERT), lambda i, be, nu: (layer, be[i], 0, 0)),
                  pl.BlockSpec((1, 1, D_EXPERT, d), lambda i, be, nu: (layer, be[i], 0, 0))],
        out_specs=pl.BlockSpec((MOE_BLOCK, dh), lambda i, be, nu: (i, 0)),
        scratch_shapes=[pltpu.VMEM((d, D_EXPERT), BF16), pltpu.VMEM((d, D_EXPERT), BF16),
                        pltpu.VMEM((D_EXPERT, d), BF16)],
    )
    return pl.pallas_call(
        _expert_kernel,
        grid_spec=grid_spec,
        out_shape=jax.ShapeDtypeStruct((n_rows, dh), I32),
        compiler_params=_cparams(("arbitrary",)),
    )(block_e, n_used, xin, w_eg, w_eu, w_ed)


def _combine_kernel(y0_ref, y1_ref, rt_ref, x_ref, g2_ref, lng_ref, lnb_ref, o_ref, *, alpha):
    rt = rt_ref[0]
    y = rt[:, 2:3] * _unpack_rows(y0_ref[...]) + rt[:, 3:4] * _unpack_rows(y1_ref[...])
    o_ref[0] = _layer_norm(alpha * x_ref[0] + g2_ref[0] * y, lng_ref[...], lnb_ref[...])


def _combine(planes, tok0, n_tok, route, x, g2, lp, alpha):
    b, l, d = x.shape
    tc = min(ROW_TILE, l)
    npb = l // tc
    blk0 = tok0 // tc
    blk1 = (n_tok + tok0) // tc
    assert blk0 * tc == tok0 and blk1 * tc == n_tok + tok0
    par = lambda a: pl.BlockSpec(a.shape, lambda i, j: (0,) * a.ndim)
    return pl.pallas_call(
        functools.partial(_combine_kernel, alpha=alpha),
        grid=(b, npb),
        in_specs=[pl.BlockSpec((tc, d // 2), lambda i, j: (blk0 + i * npb + j, 0)),
                  pl.BlockSpec((tc, d // 2), lambda i, j: (blk1 + i * npb + j, 0)),
                  pl.BlockSpec((1, tc, LANES), lambda i, j: (i, j, 0)),
                  pl.BlockSpec((1, tc, d), lambda i, j: (i, j, 0)),
                  pl.BlockSpec((1, 1, d), lambda i, j: (i, 0, 0)),
                  par(lp["ln2_g"]), par(lp["ln2_b"])],
        out_specs=pl.BlockSpec((1, tc, d), lambda i, j: (i, j, 0)),
        out_shape=jax.ShapeDtypeStruct((b, l, d), F32),
        compiler_params=_cparams(("arbitrary", "arbitrary")),
    )(planes, planes, route, x, g2, lp["ln2_g"], lp["ln2_b"])


def _moe_plan(routes, counts):
    cnt = counts[0, ROUTE_LANE0:ROUTE_LANE0 + MOE_EXPERTS].astype(I32)
    padded = (cnt + MOE_BLOCK - 1) // MOE_BLOCK * MOE_BLOCK
    pad_end = jnp.cumsum(padded)
    pad_start = pad_end - padded
    n_assign = sum(r.shape[0] * r.shape[1] for r in routes) * 2
    n_blocks = -(-n_assign // MOE_BLOCK) + MOE_EXPERTS
    first_row = jnp.arange(n_blocks, dtype=I32)[:, None] * MOE_BLOCK
    block_e = jnp.minimum(jnp.sum((pad_end[None, :] <= first_row).astype(I32), axis=1), MOE_EXPERTS - 1)
    n_used = (pad_end[-1:] // MOE_BLOCK).astype(I32)
    eidx = jnp.arange(MOE_EXPERTS, dtype=I32)
    dests = []
    for r in routes:
        e = r[..., 0:2].astype(I32)
        rank = r[..., 4:6].astype(I32)
        start = jnp.sum(jnp.where(e[..., None] == eidx, pad_start, 0), axis=-1)
        dests.append((start + rank).reshape(-1, 2))
    n_rows = n_blocks * MOE_BLOCK
    gap = jnp.concatenate([padded - cnt, n_rows - pad_end[-1:]])
    gap_first = jnp.concatenate([pad_start + cnt, pad_end[-1:]])
    gap_cum = jnp.cumsum(gap)
    j = jnp.arange(n_rows - n_assign, dtype=I32)
    seg = jnp.sum((gap_cum[None, :] <= j[:, None]).astype(I32), axis=1)
    offset = jnp.sum(jnp.where(seg[:, None] == jnp.arange(MOE_EXPERTS + 1, dtype=I32),
                               gap_first - (gap_cum - gap), 0), axis=1)
    return dests, block_e, n_used, n_rows, (offset + j).astype(I32)


def _layer_params(i, p):
    w_in = p["w_in"][i]
    s0, s1, s2, s3 = D_INNER, D_INNER + D_XBC, D_INNER + D_XBC + 2 * HEADS, D_INNER + D_XBC + 2 * HEADS + D_MODEL
    pad = LANES - 2 * HEADS
    wdt = jnp.pad(w_in[:, s1:s2], ((0, 0), (0, pad)))
    w_r = jnp.concatenate([p["w_router_group"][i],
                           jnp.transpose(p["w_router_expert"][i], (1, 0, 2)).reshape(D_MODEL, MOE_EXPERTS)], axis=1)
    w_r = jnp.pad(w_r, ((0, 0), (0, LANES - w_r.shape[1])))
    w_r_hi = w_r.astype(BF16)
    b_r = jnp.concatenate([p["b_router_group"][i], p["b_router_expert"][i].reshape(-1)])
    return {
        "w_in": tuple(w.astype(BF16) for w in (w_in[:, :s0], w_in[:, s0:s1], w_in[:, s2:s3], w_in[:, s3:], wdt)),
        "conv_w": p["conv_w"][i], "conv_b": p["conv_b"][i][None],
        "dt_bias": jnp.pad(p["dt_bias"][i].reshape(1, -1), ((0, 0), (0, pad))),
        "a_log": jnp.pad(p["a_log"][i].reshape(1, -1), ((0, 0), (0, pad))),
        "d_skip_x": jnp.repeat(p["d_skip"][i], HEADDIM)[None],
        "ssd_norm_g": p["ssd_norm_g"][i][None],
        "pool_w": p["pool_w"][i].astype(BF16), "pool_scale": p["pool_scale"][i][None],
        "b_gate": p["b_gate"][i][None],
        "w_branch_a": p["w_branch_a"][i].astype(BF16), "w_branch_b": p["w_branch_b"][i].astype(BF16),
        "w_out": p["w_out"][i].astype(BF16),
        "ln1_g": p["ln1_g"][i][None], "ln1_b": p["ln1_b"][i][None],
        "ln2_g": p["ln2_g"][i][None], "ln2_b": p["ln2_b"][i][None],
        "w_r_hi": w_r_hi, "w_r_lo": (w_r - w_r_hi.astype(F32)).astype(BF16),
        "b_r": jnp.pad(b_r, (0, LANES - b_r.shape[0]))[None],
    }


def _head_expanders():
    col_head = jnp.arange(D_INNER, dtype=I32) // HEADDIM
    row = jnp.arange(LANES, dtype=I32)[:, None] % (2 * HEADS)
    return (row == col_head).astype(BF16), (row == col_head + HEADS).astype(BF16)


def kernel(x, c, ctx, c_ctx, w_ada, b_ada, w_in, b_gate, conv_w, conv_b, dt_bias, a_log, d_skip, ssd_norm_g, pool_w, pool_scale, w_branch_a, w_branch_b, w_out, ln1_g, ln1_b, ln2_g, ln2_b, w_router_group, b_router_group, w_router_expert, b_router_expert, w_expert_gate, w_expert_up, w_expert_down):
    p = dict(w_in=w_in, b_gate=b_gate, conv_w=conv_w, conv_b=conv_b, dt_bias=dt_bias, a_log=a_log, d_skip=d_skip,
             ssd_norm_g=ssd_norm_g, pool_w=pool_w, pool_scale=pool_scale, w_branch_a=w_branch_a,
             w_branch_b=w_branch_b, w_out=w_out, ln1_g=ln1_g, ln1_b=ln1_b, ln2_g=ln2_g, ln2_b=ln2_b,
             w_router_group=w_router_group, b_router_group=b_router_group, w_router_expert=w_router_expert,
             b_router_expert=b_router_expert, w_expert_gate=w_expert_gate, w_expert_up=w_expert_up,
             w_expert_down=w_expert_down)
    b, l, d = x.shape
    depth = w_ada.shape[0]
    alpha = (2.0 * depth) ** DEPTH_ALPHA_POW
    n_mod = -(-(b + 1) // 8) * 8
    cvec = jnp.zeros((n_mod, d), F32).at[:b].set(c).at[b].set(c_ctx)
    mods = _ada_mods(cvec, w_ada, b_ada)
    r_f, r_b = _head_expanders()
    zero_state = jnp.zeros((b, STATE, D_INNER), F32)
    zero_cnt = jnp.zeros((1, LANES), F32)
    xl, xc = x, ctx
    for i in range(depth):
        last = i == depth - 1
        lp = _layer_params(i, p)
        mod_l = [mods[i, :b, k * d:(k + 1) * d][:, None, :] for k in range(6)]
        mod_c = [jnp.broadcast_to(mods[i, b, k * d:(k + 1) * d], (b, 1, d)) for k in range(6)]

        zc, act_c, uc, gc, dtc = _in_proj(xc, mod_c[0], mod_c[1], lp["w_in"], lp["conv_w"], lp["conv_b"])
        if last:
            (s_f,) = _ssd_state_scan(act_c, dtc, zero_state, lp, r_f, rev=False, emit=False)
            (s_b,) = _ssd_state_scan(act_c, dtc, zero_state, lp, r_b, rev=True, emit=False)
        else:
            stb_c, s_b = _ssd_state_scan(act_c, dtc, zero_state, lp, r_b, rev=True, emit=True)
            ya_c, s_f = _ssd_main(act_c, dtc, zc, stb_c, zero_state, lp, r_f, r_b)
            yp_c = _pool_branch(uc, lp, False)

        zl, act_l, ul, gl, dtl = _in_proj(xl, mod_l[0], mod_l[1], lp["w_in"], lp["conv_w"], lp["conv_b"])
        stb_l, _ = _ssd_state_scan(act_l, dtl, s_b, lp, r_b, rev=True, emit=True)
        ya_l, _ = _ssd_main(act_l, dtl, zl, stb_l, s_f, lp, r_f, r_b)
        yp_l = _pool_branch(ul, lp, True)
        xl, h2_l, rt_l, cnt = _merge(ya_l, yp_l, gl, xl, mod_l[2], mod_l[3], mod_l[4], lp, zero_cnt, alpha)
        streams = [(h2_l, rt_l)]
        if not last:
            xc, h2_c, rt_c, cnt = _merge(ya_c, yp_c, gc, xc, mod_c[2], mod_c[3], mod_c[4], lp, cnt, alpha)
            streams.append((h2_c, rt_c))

        dests, block_e, n_used, n_rows, pad_rows = _moe_plan([rt for _, rt in streams], cnt)
        dest_t = jnp.concatenate(dests, axis=0).T
        n_tok = dest_t.shape[1]
        h2_all = jnp.concatenate([h2.reshape(-1, d // 2) for h2, _ in streams], axis=0)
        xin = _sc_dispatch_rows(h2_all, dest_t[0], dest_t[1], pad_rows, n_rows)
        yb = _experts(xin, block_e, n_used, w_expert_gate, w_expert_up, w_expert_down, i)
        planes = _sc_gather_rows(yb, dest_t.reshape(-1))
        xl = _combine(planes, 0, n_tok, rt_l, xl, mod_l[5], lp, alpha)
        if not last:
            xc = _combine(planes, b * l, n_tok, rt_c, xc, mod_c[5], lp, alpha)
    return xl
```

```python
import functools
import math

import jax
import jax.numpy as jnp
from jax import lax
from jax.experimental import pallas as pl
from jax.experimental.pallas import tpu as pltpu
from jax.experimental.pallas import tpu_sc as plsc

F32 = jnp.float32
BF16 = jnp.bfloat16
I32 = jnp.int32

D_MODEL = 1024
D_INNER = 2048
HEADS = 32
HEADDIM = 64
GROUPS = 4
STATE = 128
D_BC = GROUPS * STATE
D_XBC = D_INNER + 2 * D_BC
CONV_K = 5
CHUNK = 128
GRID_W = 64
POOL_WINDOWS = (2, 4, 8, 16)
POOL_GROUP_DIM = 256
MOE_GROUPS = 4
MOE_EPG = 8
MOE_EXPERTS = 32
D_EXPERT = 512
MOE_BLOCK = 512
DEPTH_ALPHA_POW = 0.25
NORM_EPS = 1e-5
LANES = 128
ROUTE_LANE0 = MOE_GROUPS
VMEM_LIMIT = 56 * 1024 * 1024


def _cparams(sem, vmem=VMEM_LIMIT):
    return pltpu.CompilerParams(dimension_semantics=sem, vmem_limit_bytes=vmem)


def _sigmoid(v):
    return 0.5 + 0.5 * jnp.tanh(0.5 * v)


def _silu(v):
    h = 0.5 * v
    return h + h * jnp.tanh(h)


def _layer_norm(v, g, b):
    mu = jnp.mean(v, axis=-1, keepdims=True)
    d = v - mu
    var = jnp.mean(d * d, axis=-1, keepdims=True)
    return d * lax.rsqrt(var + NORM_EPS) * g + b


def _split_bf16(v):
    hi = v.astype(BF16)
    lo = (v - hi.astype(F32)).astype(BF16)
    return hi, lo


def _dot(a, b):
    return jnp.dot(a, b, preferred_element_type=F32)


U32 = jnp.uint32
HI16 = 0xFFFF0000


def _pack_rows(v):
    half = v.shape[1] // 2
    lo = lax.bitcast_convert_type(v[:, :half].astype(BF16).astype(F32), U32)
    hi = lax.bitcast_convert_type(v[:, half:].astype(BF16).astype(F32), U32)
    return lax.bitcast_convert_type((lo >> 16) | (hi & U32(HI16)), I32)


def _unpack_rows(w):
    w = lax.bitcast_convert_type(w, U32)
    lo = lax.bitcast_convert_type(w << 16, F32)
    hi = lax.bitcast_convert_type(w & U32(HI16), F32)
    return jnp.concatenate([lo, hi], axis=1)


def _ada_kernel(c_ref, w_ref, b_ref, o_ref):
    s = _silu(c_ref[...]).astype(BF16)
    o_ref[0] = _dot(s, w_ref[0].astype(BF16)) + b_ref[0]


def _ada_mods(cvec, w_ada, b_ada):
    depth, d, n = w_ada.shape
    r = cvec.shape[0]
    tn = D_MODEL
    return pl.pallas_call(
        _ada_kernel,
        grid=(depth, n // tn),
        in_specs=[pl.BlockSpec((r, d), lambda l, j: (0, 0)),
                  pl.BlockSpec((1, d, tn), lambda l, j: (l, 0, j)),
                  pl.BlockSpec((1, 1, tn), lambda l, j: (l, 0, j))],
        out_specs=pl.BlockSpec((1, r, tn), lambda l, j: (l, 0, j)),
        out_shape=jax.ShapeDtypeStruct((depth, r, n), F32),
        compiler_params=_cparams(("arbitrary", "arbitrary")),
    )(cvec, w_ada, b_ada.reshape(depth, 1, n))


CONV_HALO = 8
CONV_SLAB = 512


def _inproj_kernel(xp_ref, x_ref, xn_ref, sh_ref, sc_ref, wz_ref, wx_ref, wu_ref, wg_ref, wdt_ref, cw_ref, cb_ref,
                   z_ref, act_ref, u_ref, g_ref, dt_ref):
    j = pl.program_id(1)
    nj = pl.num_programs(1)
    tm = x_ref.shape[1]
    n = tm + 2 * CONV_HALO
    mod = lambda v: (v * (1.0 + sc_ref[0]) + sh_ref[0]).astype(BF16)
    h = mod(x_ref[0])
    h_ext = mod(jnp.concatenate([xp_ref[0], x_ref[0], xn_ref[0]], axis=0))

    row = lax.broadcasted_iota(I32, (n, 1), 0)
    inside = ((row >= CONV_HALO) | (j > 0)) & ((row < CONV_HALO + tm) | (j < nj - 1))

    z_ref[0] = _dot(h, wz_ref[...]).astype(BF16)
    u_ref[0] = _dot(h, wu_ref[...]).astype(BF16)
    g_ref[0] = _dot(h, wg_ref[...]).astype(BF16)
    dt_ref[0] = _dot(h, wdt_ref[...])
    down = lambda v: pltpu.roll(v, 1, 0)
    up = lambda v: pltpu.roll(v, n - 1, 0)
    for s in range(D_XBC // CONV_SLAB):
        cols = slice(s * CONV_SLAB, (s + 1) * CONV_SLAB)
        e = jnp.where(inside, _dot(h_ext, wx_ref[:, cols]), 0.0)
        y = [e * cw_ref[k:k + 1, cols] for k in range(CONV_K)]
        acc = y[2] + down(y[1] + down(y[0])) + up(y[3] + up(y[4]))
        act_ref[0, :, cols] = _silu(acc[CONV_HALO:CONV_HALO + tm] + cb_ref[:, cols]).astype(BF16)


def _in_proj(x, shift, scale, wts, conv_w, conv_b):
    b, l, d = x.shape
    tm = min(512, l)
    hpt = tm // CONV_HALO
    nh = l // CONV_HALO
    wz, wx, wu, wg, wdt = wts
    tok = lambda n: pl.BlockSpec((1, tm, n), lambda i, j: (i, j, 0))
    mod = pl.BlockSpec((1, 1, d), lambda i, j: (i, 0, 0))
    wspec = lambda w: pl.BlockSpec(w.shape, lambda i, j: (0, 0), pipeline_mode=pl.Buffered(1))
    par = lambda a: pl.BlockSpec(a.shape, lambda i, j: (0, 0))
    halo_prev = pl.BlockSpec((1, CONV_HALO, d), lambda i, j: (i, jnp.maximum(j * hpt - 1, 0), 0))
    halo_next = pl.BlockSpec((1, CONV_HALO, d), lambda i, j: (i, jnp.minimum((j + 1) * hpt, nh - 1), 0))
    return pl.pallas_call(
        _inproj_kernel,
        grid=(b, l // tm),
        in_specs=[halo_prev, tok(d), halo_next, mod, mod, wspec(wz), wspec(wx), wspec(wu), wspec(wg), wspec(wdt),
                  par(conv_w), par(conv_b)],
        out_specs=[tok(D_INNER), tok(D_XBC), tok(D_MODEL), tok(2 * D_MODEL), tok(LANES)],
        out_shape=[jax.ShapeDtypeStruct((b, l, D_INNER), BF16),
                   jax.ShapeDtypeStruct((b, l, D_XBC), BF16),
                   jax.ShapeDtypeStruct((b, l, D_MODEL), BF16),
                   jax.ShapeDtypeStruct((b, l, 2 * D_MODEL), BF16),
                   jax.ShapeDtypeStruct((b, l, LANES), F32)],
        compiler_params=_cparams(("arbitrary", "arbitrary")),
    )(x, x, x, shift, scale, wz, wx, wu, wg, wdt, conv_w, conv_b)


def _dt_prep(dt_raw, dtb_ref, alog_ref):
    q = CHUNK
    lane = lax.broadcasted_iota(I32, (q, LANES), 1)
    raw = dt_raw + dtb_ref[...]
    dt = jnp.maximum(raw, 0.0) + jnp.log1p(jnp.exp(-jnp.abs(raw)))
    dt = jnp.where(lane < 2 * HEADS, dt, 0.0)
    a = dt * (-jnp.exp(alog_ref[...]))
    ii = lax.broadcasted_iota(I32, (q, q), 0)
    jj = lax.broadcasted_iota(I32, (q, q), 1)
    tri = (jj <= ii).astype(BF16)
    a_hi, a_lo = _split_bf16(a)
    acs = _dot(tri, a_hi) + _dot(tri, a_lo)
    tot = acs[q - 1:q, :]
    return dt, a, acs, tot


def _expand(v, lane0, r_ref):
    return _dot(_expand_src(v, lane0), r_ref[...])


def _expand_src(v, lane0):
    lane = lax.broadcasted_iota(I32, v.shape, 1)
    vm = jnp.where((lane >= lane0) & (lane < lane0 + HEADS), v, 0.0)
    hi = vm.astype(BF16).astype(F32)
    return (hi + pltpu.roll(vm - hi, 2 * HEADS, 1)).astype(BF16)


def _state_update(st_ref, bm, xw_bf, decay_x):
    gw = D_INNER // GROUPS
    for g in range(GROUPS):
        bgt = bm[:, g * STATE:(g + 1) * STATE].T.astype(BF16)
        cols = slice(g * gw, (g + 1) * gw)
        st_ref[:, cols] = st_ref[:, cols] * decay_x[:, cols] + _dot(bgt, xw_bf[:, cols])


SCAN_CHUNKS = 8


def _ssd_state_kernel(*refs, rev, emit, nch):
    act_ref, dt_ref, h0_ref, dtb_ref, alog_ref, r_ref = refs[:6]
    if emit:
        stout_ref, sfin_ref, st_ref = refs[6:]
    else:
        sfin_ref, st_ref = refs[6:]
    s = pl.program_id(1)
    ns = pl.num_programs(1)

    @pl.when(s == 0)
    def _():
        st_ref[...] = h0_ref[0]

    for k in range(nch):
        cc = (nch - 1 - k) if rev else k
        rows = slice(cc * CHUNK, (cc + 1) * CHUNK)
        if emit:
            stout_ref[0, cc] = st_ref[...].astype(BF16)
        xs = act_ref[0, rows, :D_INNER].astype(F32)
        bm = act_ref[0, rows, D_INNER:D_INNER + D_BC].astype(F32)
        dt, a, acs, tot = _dt_prep(dt_ref[0, rows, :], dtb_ref, alog_ref)
        if rev:
            w = jnp.exp(acs - a) * dt
            lane0 = HEADS
        else:
            w = jnp.exp(tot - acs) * dt
            lane0 = 0
        w_x = _expand(w, lane0, r_ref)
        dec_x = _expand(jnp.broadcast_to(jnp.exp(tot), (16, LANES)), lane0, r_ref)[0:1]
        _state_update(st_ref, bm, (w_x * xs).astype(BF16), dec_x)

    @pl.when(s == ns - 1)
    def _():
        sfin_ref[0] = st_ref[...]


def _ssd_state_scan(act, dt, h0, lp, r_mat, *, rev, emit):
    b, l, _ = act.shape
    nc = l // CHUNK
    nch = min(SCAN_CHUNKS, nc)
    ns = nc // nch
    seq = (lambda s: ns - 1 - s) if rev else (lambda s: s)
    par = lambda shape: pl.BlockSpec(shape, lambda i, s: (0,) * len(shape))
    in_specs = [
        pl.BlockSpec((1, nch * CHUNK, D_XBC), lambda i, s: (i, seq(s), 0)),
        pl.BlockSpec((1, nch * CHUNK, LANES), lambda i, s: (i, seq(s), 0)),
        pl.BlockSpec((1, STATE, D_INNER), lambda i, s: (i, 0, 0)),
        par((1, LANES)), par((1, LANES)), par((LANES, D_INNER)),
    ]
    out_specs = [pl.BlockSpec((1, STATE, D_INNER), lambda i, s: (i, 0, 0))]
    out_shape = [jax.ShapeDtypeStruct((b, STATE, D_INNER), F32)]
    if emit:
        out_specs = [pl.BlockSpec((1, nch, STATE, D_INNER), lambda i, s: (i, seq(s), 0, 0))] + out_specs
        out_shape = [jax.ShapeDtypeStruct((b, nc, STATE, D_INNER), BF16)] + out_shape
    return pl.pallas_call(
        functools.partial(_ssd_state_kernel, rev=rev, emit=emit, nch=nch),
        grid=(b, ns),
        in_specs=in_specs, out_specs=out_specs, out_shape=out_shape,
        scratch_shapes=[pltpu.VMEM((STATE, D_INNER), F32)],
        compiler_params=_cparams(("arbitrary", "arbitrary")),
    )(act, dt, h0, lp["dt_bias"], lp["a_log"], r_mat)


MAIN_CHUNKS = 4


def _ssd_main_kernel(act_ref, dt_ref, z_ref, stb_ref, h0_ref,
                     dtb_ref, alog_ref, dsk_ref, ng_ref, rf_ref, rb_ref,
                     y_ref, sfin_ref, st_ref, ys_ref, *, nch):
    c = pl.program_id(1)
    nc = pl.num_programs(1)

    @pl.when(c == 0)
    def _():
        st_ref[...] = h0_ref[0]

    for k in range(nch):
        rows = pl.ds(k * CHUNK, CHUNK)
        _ssd_main_chunk(act_ref.at[0, rows], dt_ref.at[0, rows], z_ref.at[0, rows], stb_ref.at[0, k],
                        dtb_ref, alog_ref, dsk_ref, ng_ref, rf_ref, rb_ref, y_ref.at[0, rows], st_ref, ys_ref)

    @pl.when(c == nc - 1)
    def _():
        sfin_ref[0] = st_ref[...]


def _ssd_main_chunk(act_ref, dt_ref, z_ref, stb_ref, dtb_ref, alog_ref, dsk_ref, ng_ref, rf_ref, rb_ref,
                    y_ref, st_ref, ys_ref):
    q = CHUNK
    dt, a, acs, tot = _dt_prep(dt_ref[...], dtb_ref, alog_ref)
    eb = acs - a

    ef_s = _expand_src(jnp.exp(acs), 0)
    eb_s = _expand_src(jnp.exp(tot - eb), HEADS)
    tf_s = _expand_src(jnp.exp(tot - acs) * dt, 0)
    dec_s = _expand_src(jnp.broadcast_to(jnp.exp(tot), (16, LANES)), 0)

    lane = lax.broadcasted_iota(I32, (q, LANES), 1)
    ldt = jnp.log(dt)
    qm = jnp.where(lane < HEADS, acs - ldt, jnp.where(lane < 2 * HEADS, eb + ldt, pltpu.roll(dt, 2 * HEADS, 1)))
    qt = qm.T
    ii = lax.broadcasted_iota(I32, (q, q), 0)
    jj = lax.broadcasted_iota(I32, (q, q), 1)
    lower = jj <= ii
    diag = jj == ii
    lane2 = lax.broadcasted_iota(I32, (q, LANES), 1)
    left = lane2 < HEADDIM

    gw = D_INNER // GROUPS
    hpg = HEADS // GROUPS
    for g in range(GROUPS):
        bg = act_ref[:, D_INNER + g * STATE:D_INNER + (g + 1) * STATE]
        cg = act_ref[:, D_INNER + D_BC + g * STATE:D_INNER + D_BC + (g + 1) * STATE]
        cb = lax.dot_general(cg, bg, (((1,), (1,)), ((), ())), preferred_element_type=F32)
        cols = slice(g * gw, (g + 1) * gw)
        y_off = (_dot(ef_s, rf_ref[:, cols]) * _dot(cg, st_ref[:, cols].astype(BF16))
                 + _dot(eb_s, rb_ref[:, cols]) * _dot(cg, stb_ref[:, cols]))
        for pr in range(hpg // 2):
            ms = []
            for hh in range(2):
                h = g * hpg + pr * 2 + hh
                afc = acs[:, h:h + 1]
                afr = qt[h:h + 1, :]
                ebc = eb[:, HEADS + h:HEADS + h + 1]
                ebr = qt[HEADS + h:HEADS + h + 1, :]
                wb = qt[3 * HEADS + h:3 * HEADS + h + 1, :]
                arg = jnp.where(lower, afc - afr, ebr - ebc)
                ms.append((cb * (jnp.exp(arg) + jnp.where(diag, wb, 0.0))).astype(BF16))
            lhs = jnp.concatenate(ms, axis=1)
            c0 = g * gw + pr * 2 * HEADDIM
            xp = act_ref[:, c0:c0 + 2 * HEADDIM]
            zero = jnp.zeros_like(xp)
            rhs = jnp.concatenate([jnp.where(left, xp, zero), jnp.where(left, zero, xp)], axis=0)
            y_pair = _dot(lhs, rhs) + y_off[:, pr * 2 * HEADDIM:(pr + 1) * 2 * HEADDIM]
            y_pair = y_pair + dsk_ref[:, c0:c0 + 2 * HEADDIM] * xp.astype(F32)
            ys_ref[:, c0:c0 + 2 * HEADDIM] = y_pair

        xw = (_dot(tf_s, rf_ref[:, cols]) * act_ref[:, cols].astype(F32)).astype(BF16)
        bgt = bg.astype(F32).T.astype(BF16)
        st_ref[:, cols] = st_ref[:, cols] * _dot(dec_s, rf_ref[:, cols])[0:1] + _dot(bgt, xw)

    y = ys_ref[...]
    yz = y * _silu(z_ref[...].astype(F32))
    ms2 = jnp.mean(yz * yz, axis=-1, keepdims=True)
    y_ref[...] = (yz * lax.rsqrt(ms2 + NORM_EPS) * ng_ref[...]).astype(BF16)


def _ssd_main(act, dt, z, stb, h0, lp, r_f, r_b):
    b, l, _ = act.shape
    nc = l // CHUNK
    nch = min(MAIN_CHUNKS, nc)
    rows = nch * CHUNK
    par = lambda shape: pl.BlockSpec(shape, lambda i, s: (0,) * len(shape))
    in_specs = [
        pl.BlockSpec((1, rows, D_XBC), lambda i, s: (i, s, 0)),
        pl.BlockSpec((1, rows, LANES), lambda i, s: (i, s, 0)),
        pl.BlockSpec((1, rows, D_INNER), lambda i, s: (i, s, 0)),
        pl.BlockSpec((1, nch, STATE, D_INNER), lambda i, s: (i, s, 0, 0)),
        pl.BlockSpec((1, STATE, D_INNER), lambda i, s: (i, 0, 0)),
        par((1, LANES)), par((1, LANES)),
        par((1, D_INNER)), par((1, D_INNER)), par((LANES, D_INNER)), par((LANES, D_INNER)),
    ]
    return pl.pallas_call(
        functools.partial(_ssd_main_kernel, nch=nch),
        grid=(b, nc // nch),
        in_specs=in_specs,
        out_specs=[pl.BlockSpec((1, rows, D_INNER), lambda i, s: (i, s, 0)),
                   pl.BlockSpec((1, STATE, D_INNER), lambda i, s: (i, 0, 0))],
        out_shape=[jax.ShapeDtypeStruct((b, l, D_INNER), BF16),
                   jax.ShapeDtypeStruct((b, STATE, D_INNER), F32)],
        scratch_shapes=[pltpu.VMEM((STATE, D_INNER), F32), pltpu.VMEM((CHUNK, D_INNER), F32)],
        compiler_params=_cparams(("arbitrary", "arbitrary")),
    )(act, dt, z, stb, h0, lp["dt_bias"], lp["a_log"], lp["d_skip_x"], lp["ssd_norm_g"], r_f, r_b)


def _pool_kernel(*refs, width, vertical, halo, hblock):
    if vertical:
        prev_ref, cur_ref, next_ref, pw_ref, ps_ref, o_ref = refs
    else:
        cur_ref, pw_ref, ps_ref, o_ref = refs
    t = pl.program_id(1)
    nt = pl.num_programs(1)
    tp = cur_ref.shape[1]
    n_tok = nt * tp
    shift = int(math.log2(width))
    idx = lax.broadcasted_iota(I32, (tp, 1), 0) + t * tp
    colpos = idx & (width - 1)
    rowpos = idx >> shift
    n_rows = n_tok // width
    bi = lax.broadcasted_iota(I32, (hblock, hblock), 0)
    bj = lax.broadcasted_iota(I32, (hblock, hblock), 1)
    same_row = (bi >> shift) == (bj >> shift)
    cur = cur_ref[0].astype(F32)
    if vertical:
        prev = jnp.where(t > 0, prev_ref[0].astype(F32), 0.0)
        nxt = jnp.where(t < nt - 1, next_ref[0].astype(F32), 0.0)
    for gi, k in enumerate(POOL_WINDOWS):
        cols = slice(gi * POOL_GROUP_DIM, (gi + 1) * POOL_GROUP_DIM)
        ug = cur[:, cols]
        lo = k // 2
        if vertical:
            above, below = lo * width, (k - 1 - lo) * width
            parts = [prev[halo - above:, cols], ug] + ([nxt[:below, cols]] if below else [])
            e = jnp.concatenate(parts, axis=0)
            step = width
            m = 1
            while m < k:
                n = e.shape[0] - step
                e = e[0:n] + e[step:step + n]
                step *= 2
                m *= 2
            s = e
            cnt_r = (jnp.minimum(rowpos - lo + k, n_rows) - jnp.maximum(rowpos - lo, 0)).astype(F32)
        else:
            s = ug
            cnt_r = jnp.ones((tp, 1), F32)
        band = (same_row & (bj - bi >= -lo) & (bj - bi < k - lo)).astype(BF16)
        s_bf = s.astype(BF16)
        hs = [_dot(band, s_bf[r * hblock:(r + 1) * hblock]) for r in range(tp // hblock)]
        hsum = hs[0] if len(hs) == 1 else jnp.concatenate(hs, axis=0)
        cnt_c = (jnp.minimum(colpos - lo + k, width) - jnp.maximum(colpos - lo, 0)).astype(F32)
        mean = hsum / (cnt_r * cnt_c)
        y = _dot((mean - ug).astype(BF16), pw_ref[gi])
        o_ref[0, :, cols] = (y * ps_ref[:, cols]).astype(BF16)


def _pool_branch(u, lp, on_grid):
    b, l, d = u.shape
    pw, ps = lp["pool_w"], lp["pool_scale"]
    if on_grid:
        width, halo = GRID_W, (max(POOL_WINDOWS) // 2) * GRID_W
        tp = min(1024, l)
        hb = tp // halo
        nhb = l // halo
        in_specs = [pl.BlockSpec((1, halo, d), lambda i, t: (i, jnp.maximum(t * hb - 1, 0), 0)),
                    pl.BlockSpec((1, tp, d), lambda i, t: (i, t, 0)),
                    pl.BlockSpec((1, halo, d), lambda i, t: (i, jnp.minimum((t + 1) * hb, nhb - 1), 0))]
        args = (u, u, u)
        kern = functools.partial(_pool_kernel, width=width, vertical=True, halo=halo, hblock=LANES)
    else:
        tp = l
        in_specs = [pl.BlockSpec((1, tp, d), lambda i, t: (i, t, 0))]
        args = (u,)
        kern = functools.partial(_pool_kernel, width=l, vertical=False, halo=0, hblock=l)
    in_specs += [pl.BlockSpec(pw.shape, lambda i, t: (0, 0, 0)), pl.BlockSpec(ps.shape, lambda i, t: (0, 0))]
    return pl.pallas_call(
        kern,
        grid=(b, l // tp),
        in_specs=in_specs,
        out_specs=pl.BlockSpec((1, tp, d), lambda i, t: (i, t, 0)),
        out_shape=jax.ShapeDtypeStruct((b, l, d), BF16),
        compiler_params=_cparams(("arbitrary", "arbitrary")),
    )(*args, pw, ps)


def _merge_kernel(ya_ref, yp_ref, gt_ref, x_ref, g1_ref, sh2_ref, sc2_ref, bg_ref,
                  wa_ref, wb_ref, wo_ref, lng_ref, lnb_ref, wrh_ref, wrl_ref, br_ref, cin_ref,
                  xo_ref, h2_ref, rt_ref, cout_ref, cnt_ref, *, alpha):
    i = pl.program_id(0)
    j = pl.program_id(1)
    tm = x_ref.shape[1]

    @pl.when((i == 0) & (j == 0))
    def _():
        cnt_ref[...] = cin_ref[...]

    gate = _sigmoid(gt_ref[0].astype(F32) + bg_ref[...])
    merged = (gate[:, :D_MODEL] * _dot(ya_ref[0], wa_ref[...])
              + gate[:, D_MODEL:] * _dot(yp_ref[0], wb_ref[...]))
    out = _dot(merged.astype(BF16), wo_ref[...])
    xn = _layer_norm(alpha * x_ref[0] + g1_ref[0] * out, lng_ref[...], lnb_ref[...])
    xo_ref[0] = xn
    h2 = xn * (1.0 + sc2_ref[0]) + sh2_ref[0]
    h2_ref[0] = _pack_rows(h2)

    h_hi, h_lo = _split_bf16(h2)
    logits = _dot(h_hi, wrh_ref[...]) + _dot(h_lo, wrh_ref[...]) + _dot(h_hi, wrl_ref[...]) + br_ref[...]
    lane = lax.broadcasted_iota(I32, (tm, LANES), 1)
    neg = jnp.float32(-jnp.inf)
    lg = jnp.where(lane < MOE_GROUPS, logits, neg)
    mg = jnp.max(lg, axis=-1, keepdims=True)
    grp = jnp.min(jnp.where(lg == mg, lane, LANES), axis=-1, keepdims=True)
    p_grp = 1.0 / jnp.sum(jnp.exp(lg - mg), axis=-1, keepdims=True)
    lo_lane = ROUTE_LANE0 + grp * MOE_EPG
    le = jnp.where((lane >= lo_lane) & (lane < lo_lane + MOE_EPG), logits, neg)
    v1 = jnp.max(le, axis=-1, keepdims=True)
    i1 = jnp.min(jnp.where(le == v1, lane, LANES), axis=-1, keepdims=True)
    le2 = jnp.where(lane == i1, neg, le)
    v2 = jnp.max(le2, axis=-1, keepdims=True)
    i2 = jnp.min(jnp.where(le2 == v2, lane, LANES), axis=-1, keepdims=True)
    e2 = jnp.exp(v2 - v1)
    w1 = p_grp / (1.0 + e2)
    w2 = p_grp * e2 / (1.0 + e2)

    oh1 = (lane == i1).astype(F32)
    oh2 = (lane == i2).astype(F32)
    oh = oh1 + oh2
    ri = lax.broadcasted_iota(I32, (tm, tm), 0)
    rj = lax.broadcasted_iota(I32, (tm, tm), 1)
    before = _dot((rj < ri).astype(BF16), oh.astype(BF16)) + cnt_ref[...]
    rank1 = jnp.sum(oh1 * before, axis=-1, keepdims=True)
    rank2 = jnp.sum(oh2 * before, axis=-1, keepdims=True)
    cnt_ref[...] = cnt_ref[...] + jnp.sum(oh, axis=0, keepdims=True)
    cout_ref[...] = cnt_ref[...]

    slab = jnp.where(lane == 0, (i1 - ROUTE_LANE0).astype(F32),
           jnp.where(lane == 1, (i2 - ROUTE_LANE0).astype(F32),
           jnp.where(lane == 2, w1,
           jnp.where(lane == 3, w2,
           jnp.where(lane == 4, rank1,
           jnp.where(lane == 5, rank2, 0.0))))))
    rt_ref[0] = slab


def _merge(ya, yp, gates, x, g1, sh2, sc2, lp, cnt_in, alpha):
    b, l, d = x.shape
    tm = min(512, l)
    tok = lambda n: pl.BlockSpec((1, tm, n), lambda i, j: (i, j, 0))
    mod = pl.BlockSpec((1, 1, d), lambda i, j: (i, 0, 0))
    par = lambda a: pl.BlockSpec(a.shape, lambda i, j: (0,) * a.ndim)
    params = (lp["b_gate"], lp["w_branch_a"], lp["w_branch_b"], lp["w_out"], lp["ln1_g"], lp["ln1_b"],
              lp["w_r_hi"], lp["w_r_lo"], lp["b_r"], cnt_in)
    return pl.pallas_call(
        functools.partial(_merge_kernel, alpha=alpha),
        grid=(b, l // tm),
        in_specs=[tok(D_INNER), tok(d), tok(2 * d), tok(d), mod, mod, mod] + [par(a) for a in params],
        out_specs=[tok(d), tok(d // 2), tok(LANES), pl.BlockSpec((1, LANES), lambda i, j: (0, 0))],
        out_shape=[jax.ShapeDtypeStruct((b, l, d), F32), jax.ShapeDtypeStruct((b, l, d // 2), I32),
                   jax.ShapeDtypeStruct((b, l, LANES), F32), jax.ShapeDtypeStruct((1, LANES), F32)],
        scratch_shapes=[pltpu.VMEM((1, LANES), F32)],
        compiler_params=_cparams(("arbitrary", "arbitrary")),
    )(ya, yp, gates, x, g1, sh2, sc2, *params)


SC_CORES = 2
SC_SUBCORES = 16
SC_CHUNK = 128
ROW_TILE = 1024


def _sc_gather_rows(table, idx):
    m = idx.shape[0]
    d = table.shape[1]
    workers = SC_CORES * SC_SUBCORES
    n_chunks = m // (workers * SC_CHUNK)
    assert n_chunks * workers * SC_CHUNK == m, (m, workers, SC_CHUNK)
    mesh = plsc.VectorSubcoreMesh(core_axis_name="c", subcore_axis_name="s",
                                  num_cores=SC_CORES, num_subcores=SC_SUBCORES)

    @functools.partial(
        pl.kernel, mesh=mesh,
        out_type=jax.ShapeDtypeStruct((m, d), table.dtype),
        scratch_types=[pltpu.VMEM((n_chunks, SC_CHUNK), I32), pltpu.VMEM((SC_CHUNK, d), table.dtype),
                       pltpu.SemaphoreType.DMA],
    )
    def gather(table_hbm, idx_hbm, out_hbm, idx_v, rows_v, sem):
        wid = lax.axis_index("s") * SC_CORES + lax.axis_index("c")
        pltpu.sync_copy(idx_hbm.at[wid], idx_v)

        @pl.loop(0, n_chunks)
        def _(j):
            base = (wid * n_chunks + j) * SC_CHUNK
            pltpu.async_copy(table_hbm.at[idx_v.at[j]], rows_v, sem).wait()
            pltpu.sync_copy(rows_v, out_hbm.at[pl.ds(base, SC_CHUNK)])

    return gather(table, idx.reshape(workers, n_chunks, SC_CHUNK))


def _sc_dispatch_rows(rows, dest0, dest1, pad_rows, n_rows):
    n_tok, d = rows.shape
    workers = SC_CORES * SC_SUBCORES
    n_tc = n_tok // SC_CHUNK
    n_pc = pad_rows.shape[0] // (SC_CHUNK * workers)
    assert n_tc * SC_CHUNK == n_tok and n_pc * SC_CHUNK * workers == pad_rows.shape[0]
    steps = -(-n_tc // workers)
    zeros = jnp.zeros((SC_CHUNK, d), rows.dtype)
    mesh = plsc.VectorSubcoreMesh(core_axis_name="c", subcore_axis_name="s",
                                  num_cores=SC_CORES, num_subcores=SC_SUBCORES)

    @functools.partial(
        pl.kernel, mesh=mesh,
        out_type=jax.ShapeDtypeStruct((n_rows, d), rows.dtype),
        scratch_types=[pltpu.VMEM((SC_CHUNK,), I32), pltpu.VMEM((SC_CHUNK,), I32),
                       pltpu.VMEM((SC_CHUNK, d), rows.dtype)],
    )
    def scatter(rows_hbm, d0_hbm, d1_hbm, pad_hbm, zero_hbm, out_hbm, idx0_v, idx1_v, rows_v):
        wid = lax.axis_index("s") * SC_CORES + lax.axis_index("c")

        @pl.loop(0, steps)
        def _(j):
            c = j * workers + wid

            @pl.when(c < n_tc)
            def _():
                base = c * SC_CHUNK
                pltpu.sync_copy(d0_hbm.at[pl.ds(base, SC_CHUNK)], idx0_v)
                pltpu.sync_copy(d1_hbm.at[pl.ds(base, SC_CHUNK)], idx1_v)
                pltpu.sync_copy(rows_hbm.at[pl.ds(base, SC_CHUNK)], rows_v)
                pltpu.sync_copy(rows_v, out_hbm.at[idx0_v])
                pltpu.sync_copy(rows_v, out_hbm.at[idx1_v])

        pltpu.sync_copy(zero_hbm, rows_v)

        @pl.loop(0, n_pc)
        def _(j):
            base = (wid * n_pc + j) * SC_CHUNK
            pltpu.sync_copy(pad_hbm.at[pl.ds(base, SC_CHUNK)], idx0_v)
            pltpu.sync_copy(rows_v, out_hbm.at[idx0_v])

    return scatter(rows, dest0, dest1, pad_rows, zeros)


def _expert_kernel(be_ref, nu_ref, x_ref, wg_ref, wu_ref, wd_ref, o_ref, wgb_ref, wub_ref, wdb_ref):
    i = pl.program_id(0)

    @pl.when((i == 0) | (be_ref[i] != be_ref[jnp.maximum(i - 1, 0)]))
    def _():
        wgb_ref[...] = wg_ref[0, 0].astype(BF16)
        wub_ref[...] = wu_ref[0, 0].astype(BF16)
        wdb_ref[...] = wd_ref[0, 0].astype(BF16)

    @pl.when(i < nu_ref[0])
    def _():
        xb = _unpack_rows(x_ref[...]).astype(BF16)
        gte = _dot(xb, wgb_ref[...])
        up = _dot(xb, wub_ref[...])
        o_ref[...] = _pack_rows(_dot((_silu(gte) * up).astype(BF16), wdb_ref[...]))

    @pl.when(i >= nu_ref[0])
    def _():
        o_ref[...] = jnp.zeros_like(o_ref)


def _experts(xin, block_e, n_used, w_eg, w_eu, w_ed, layer):
    n_rows, dh = xin.shape
    d = 2 * dh
    nb = n_rows // MOE_BLOCK
    grid_spec = pltpu.PrefetchScalarGridSpec(
        num_scalar_prefetch=2,
        grid=(nb,),
        in_specs=[pl.BlockSpec((MOE_BLOCK, dh), lambda i, be, nu: (i, 0)),
                  pl.BlockSpec((1, 1, d, D_EXPERT), lambda i, be, nu: (layer, be[i], 0, 0)),
                  pl.BlockSpec((1, 1, d, D_EXPERT), lambda i, be, nu: (layer, be[i], 0, 0)),
                  pl.BlockSpec((1, 1, D_EXPERT, d), lambda i, be, nu: (layer, be[i], 0, 0))],
        out_specs=pl.BlockSpec((MOE_BLOCK, dh), lambda i, be, nu: (i, 0)),
        scratch_shapes=[pltpu.VMEM((d, D_EXPERT), BF16), pltpu.VMEM((d, D_EXPERT), BF16),
                        pltpu.VMEM((D_EXPERT, d), BF16)],
    )
    return pl.pallas_call(
        _expert_kernel,
        grid_spec=grid_spec,
        out_shape=jax.ShapeDtypeStruct((n_rows, dh), I32),
        compiler_params=_cparams(("arbitrary",)),
    )(block_e, n_used, xin, w_eg, w_eu, w_ed)


def _combine_kernel(y0_ref, y1_ref, rt_ref, x_ref, g2_ref, lng_ref, lnb_ref, o_ref, *, alpha):
    rt = rt_ref[0]
    y = rt[:, 2:3] * _unpack_rows(y0_ref[...]) + rt[:, 3:4] * _unpack_rows(y1_ref[...])
    o_ref[0] = _layer_norm(alpha * x_ref[0] + g2_ref[0] * y, lng_ref[...], lnb_ref[...])


def _combine(planes, tok0, n_tok, route, x, g2, lp, alpha):
    b, l, d = x.shape
    tc = min(ROW_TILE, l)
    npb = l // tc
    blk0 = tok0 // tc
    blk1 = (n_tok + tok0) // tc
    assert blk0 * tc == tok0 and blk1 * tc == n_tok + tok0
    par = lambda a: pl.BlockSpec(a.shape, lambda i, j: (0,) * a.ndim)
    return pl.pallas_call(
        functools.partial(_combine_kernel, alpha=alpha),
        grid=(b, npb),
        in_specs=[pl.BlockSpec((tc, d // 2), lambda i, j: (blk0 + i * npb + j, 0)),
                  pl.BlockSpec((tc, d // 2), lambda i, j: (blk1 + i * npb + j, 0)),
                  pl.BlockSpec((1, tc, LANES), lambda i, j: (i, j, 0)),
                  pl.BlockSpec((1, tc, d), lambda i, j: (i, j, 0)),
                  pl.BlockSpec((1, 1, d), lambda i, j: (i, 0, 0)),
                  par(lp["ln2_g"]), par(lp["ln2_b"])],
        out_specs=pl.BlockSpec((1, tc, d), lambda i, j: (i, j, 0)),
        out_shape=jax.ShapeDtypeStruct((b, l, d), F32),
        compiler_params=_cparams(("arbitrary", "arbitrary")),
    )(planes, planes, route, x, g2, lp["ln2_g"], lp["ln2_b"])


def _moe_plan(routes, counts):
    cnt = counts[0, ROUTE_LANE0:ROUTE_LANE0 + MOE_EXPERTS].astype(I32)
    padded = (cnt + MOE_BLOCK - 1) // MOE_BLOCK * MOE_BLOCK
    pad_end = jnp.cumsum(padded)
    pad_start = pad_end - padded
    n_assign = sum(r.shape[0] * r.shape[1] for r in routes) * 2
    n_blocks = -(-n_assign // MOE_BLOCK) + MOE_EXPERTS
    first_row = jnp.arange(n_blocks, dtype=I32)[:, None] * MOE_BLOCK
    block_e = jnp.minimum(jnp.sum((pad_end[None, :] <= first_row).astype(I32), axis=1), MOE_EXPERTS - 1)
    n_used = (pad_end[-1:] // MOE_BLOCK).astype(I32)
    eidx = jnp.arange(MOE_EXPERTS, dtype=I32)
    dests = []
    for r in routes:
        e = r[..., 0:2].astype(I32)
        rank = r[..., 4:6].astype(I32)
        start = jnp.sum(jnp.where(e[..., None] == eidx, pad_start, 0), axis=-1)
        dests.append((start + rank).reshape(-1, 2))
    n_rows = n_blocks * MOE_BLOCK
    gap = jnp.concatenate([padded - cnt, n_rows - pad_end[-1:]])
    gap_first = jnp.concatenate([pad_start + cnt, pad_end[-1:]])
    gap_cum = jnp.cumsum(gap)
    j = jnp.arange(n_rows - n_assign, dtype=I32)
    seg = jnp.sum((gap_cum[None, :] <= j[:, None]).astype(I32), axis=1)
    offset = jnp.sum(jnp.where(seg[:, None] == jnp.arange(MOE_EXPERTS + 1, dtype=I32),
                               gap_first - (gap_cum - gap), 0), axis=1)
    return dests, block_e, n_used, n_rows, (offset + j).astype(I32)


def _layer_params(i, p):
    w_in = p["w_in"][i]
    s0, s1, s2, s3 = D_INNER, D_INNER + D_XBC, D_INNER + D_XBC + 2 * HEADS, D_INNER + D_XBC + 2 * HEADS + D_MODEL
    pad = LANES - 2 * HEADS
    wdt = jnp.pad(w_in[:, s1:s2], ((0, 0), (0, pad)))
    w_r = jnp.concatenate([p["w_router_group"][i],
                           jnp.transpose(p["w_router_expert"][i], (1, 0, 2)).reshape(D_MODEL, MOE_EXPERTS)], axis=1)
    w_r = jnp.pad(w_r, ((0, 0), (0, LANES - w_r.shape[1])))
    w_r_hi = w_r.astype(BF16)
    b_r = jnp.concatenate([p["b_router_group"][i], p["b_router_expert"][i].reshape(-1)])
    return {
        "w_in": tuple(w.astype(BF16) for w in (w_in[:, :s0], w_in[:, s0:s1], w_in[:, s2:s3], w_in[:, s3:], wdt)),
        "conv_w": p["conv_w"][i], "conv_b": p["conv_b"][i][None],
        "dt_bias": jnp.pad(p["dt_bias"][i].reshape(1, -1), ((0, 0), (0, pad))),
        "a_log": jnp.pad(p["a_log"][i].reshape(1, -1), ((0, 0), (0, pad))),
        "d_skip_x": jnp.repeat(p["d_skip"][i], HEADDIM)[None],
        "ssd_norm_g": p["ssd_norm_g"][i][None],
        "pool_w": p["pool_w"][i].astype(BF16), "pool_scale": p["pool_scale"][i][None],
        "b_gate": p["b_gate"][i][None],
        "w_branch_a": p["w_branch_a"][i].astype(BF16), "w_branch_b": p["w_branch_b"][i].astype(BF16),
        "w_out": p["w_out"][i].astype(BF16),
        "ln1_g": p["ln1_g"][i][None], "ln1_b": p["ln1_b"][i][None],
        "ln2_g": p["ln2_g"][i][None], "ln2_b": p["ln2_b"][i][None],
        "w_r_hi": w_r_hi, "w_r_lo": (w_r - w_r_hi.astype(F32)).astype(BF16),
        "b_r": jnp.pad(b_r, (0, LANES - b_r.shape[0]))[None],
    }


def _head_expanders():
    col_head = jnp.arange(D_INNER, dtype=I32) // HEADDIM
    row = jnp.arange(LANES, dtype=I32)[:, None] % (2 * HEADS)
    return (row == col_head).astype(BF16), (row == col_head + HEADS).astype(BF16)


def kernel(x, c, ctx, c_ctx, w_ada, b_ada, w_in, b_gate, conv_w, conv_b, dt_bias, a_log, d_skip, ssd_norm_g, pool_w, pool_scale, w_branch_a, w_branch_b, w_out, ln1_g, ln1_b, ln2_g, ln2_b, w_router_group, b_router_group, w_router_expert, b_router_expert, w_expert_gate, w_expert_up, w_expert_down):
    p = dict(w_in=w_in, b_gate=b_gate, conv_w=conv_w, conv_b=conv_b, dt_bias=dt_bias, a_log=a_log, d_skip=d_skip,
             ssd_norm_g=ssd_norm_g, pool_w=pool_w, pool_scale=pool_scale, w_branch_a=w_branch_a,
             w_branch_b=w_branch_b, w_out=w_out, ln1_g=ln1_g, ln1_b=ln1_b, ln2_g=ln2_g, ln2_b=ln2_b,
             w_router_group=w_router_group, b_router_group=b_router_group, w_router_expert=w_router_expert,
             b_router_expert=b_router_expert, w_expert_gate=w_expert_gate, w_expert_up=w_expert_up,
             w_expert_down=w_expert_down)
    b, l, d = x.shape
    depth = w_ada.shape[0]
    alpha = (2.0 * depth) ** DEPTH_ALPHA_POW
    n_mod = -(-(b + 1) // 8) * 8
    cvec = jnp.zeros((n_mod, d), F32).at[:b].set(c).at[b].set(c_ctx)
    mods = _ada_mods(cvec, w_ada, b_ada)
    r_f, r_b = _head_expanders()
    zero_state = jnp.zeros((b, STATE, D_INNER), F32)
    zero_cnt = jnp.zeros((1, LANES), F32)
    xl, xc = x, ctx
    for i in range(depth):
        last = i == depth - 1
        lp = _layer_params(i, p)
        mod_l = [mods[i, :b, k * d:(k + 1) * d][:, None, :] for k in range(6)]
        mod_c = [jnp.broadcast_to(mods[i, b, k * d:(k + 1) * d], (b, 1, d)) for k in range(6)]

        zc, act_c, uc, gc, dtc = _in_proj(xc, mod_c[0], mod_c[1], lp["w_in"], lp["conv_w"], lp["conv_b"])
        if last:
            (s_f,) = _ssd_state_scan(act_c, dtc, zero_state, lp, r_f, rev=False, emit=False)
            (s_b,) = _ssd_state_scan(act_c, dtc, zero_state, lp, r_b, rev=True, emit=False)
        else:
            stb_c, s_b = _ssd_state_scan(act_c, dtc, zero_state, lp, r_b, rev=True, emit=True)
            ya_c, s_f = _ssd_main(act_c, dtc, zc, stb_c, zero_state, lp, r_f, r_b)
            yp_c = _pool_branch(uc, lp, False)

        zl, act_l, ul, gl, dtl = _in_proj(xl, mod_l[0], mod_l[1], lp["w_in"], lp["conv_w"], lp["conv_b"])
        stb_l, _ = _ssd_state_scan(act_l, dtl, s_b, lp, r_b, rev=True, emit=True)
        ya_l, _ = _ssd_main(act_l, dtl, zl, stb_l, s_f, lp, r_f, r_b)
        yp_l = _pool_branch(ul, lp, True)
        xl, h2_l, rt_l, cnt = _merge(ya_l, yp_l, gl, xl, mod_l[2], mod_l[3], mod_l[4], lp, zero_cnt, alpha)
        streams = [(h2_l, rt_l)]
        if not last:
            xc, h2_c, rt_c, cnt = _merge(ya_c, yp_c, gc, xc, mod_c[2], mod_c[3], mod_c[4], lp, cnt, alpha)
            streams.append((h2_c, rt_c))

        dests, block_e, n_used, n_rows, pad_rows = _moe_plan([rt for _, rt in streams], cnt)
        dest_t = jnp.concatenate(dests, axis=0).T
        n_tok = dest_t.shape[1]
        h2_all = jnp.concatenate([h2.reshape(-1, d // 2) for h2, _ in streams], axis=0)
        xin = _sc_dispatch_rows(h2_all, dest_t[0], dest_t[1], pad_rows, n_rows)
        yb = _experts(xin, block_e, n_used, w_expert_gate, w_expert_up, w_expert_down, i)
        planes = _sc_gather_rows(yb, dest_t.reshape(-1))
        xl = _combine(planes, 0, n_tok, rt_l, xl, mod_l[5], lp, alpha)
        if not last:
            xc = _combine(planes, b * l, n_tok, rt_c, xc, mod_c[5], lp, alpha)
    return xl
```
